```python
import jax
import jax.numpy as jnp
from jax import lax
import numpy as np

D_MODEL = 2048
BATCH = 1
SEQ = 8192
DEPTH = 4

GRID_W = 64
NA_HEADS = 16
NA_HEAD_DIM = 64
NA_WIDTH = NA_HEADS * NA_HEAD_DIM
WIN_ROWS = 8
WIN_COLS = 16
F_GROUPS = 4
F_GROUP_DIM = 256
F_WIDTH = F_GROUPS * F_GROUP_DIM
MIX_IN_WIDTH = 3 * NA_WIDTH + F_WIDTH
N_BRANCHES = 2
N_GROUPS = 4
EXPERTS_PER_GROUP = 8
N_EXPERTS = N_GROUPS * EXPERTS_PER_GROUP
TOP_K = 2
D_EXPERT = 512
EXPERT_BLOCK = 128
N_MOD = 6
EPS = 1e-6
NEG_INF = -1e30

kernel_name = 'hybrid_na_fnet_hmoe_encoder'


def rmsnorm(x, g):
    xf = x.astype(jnp.float32)
    y = xf * lax.rsqrt(jnp.mean(xf * xf, axis=-1, keepdims=True) + EPS)
    return (y * g.astype(jnp.float32)).astype(x.dtype)


def modulate(h, shift, scale):
    return h * (1 + scale[:, None, :]) + shift[:, None, :]


def neighborhood_attention(q, k, v, rpb):
    b, s, nh, dh = q.shape
    rows = s // GRID_W
    kh = min(WIN_ROWS, rows)
    r = jnp.arange(rows)
    row_start = jnp.clip(r - kh // 2, 0, rows - kh)
    key_rows = row_start[:, None] + jnp.arange(kh)[None, :]
    col = jnp.arange(GRID_W)
    col_start = jnp.clip(col - WIN_COLS // 2, 0, GRID_W - WIN_COLS)
    col_mask = (col[None, :] >= col_start[:, None]) & (col[None, :] < col_start[:, None] + WIN_COLS)
    qg = q.reshape(b, rows, GRID_W, nh, dh) * (dh ** -0.5)
    kr = k.reshape(b, rows, GRID_W, nh, dh)[:, key_rows]
    vr = v.reshape(b, rows, GRID_W, nh, dh)[:, key_rows]
    scores = jnp.einsum('brchd,brkjhd->bhrckj', qg, kr).astype(jnp.float32)
    dr = key_rows - r[:, None] + (WIN_ROWS - 1)
    dc = jnp.clip(col[None, :] - col[:, None], -(WIN_COLS - 1), WIN_COLS - 1) + (WIN_COLS - 1)
    bias = jnp.take(rpb[:, dr], dc, axis=3)
    bias = jnp.transpose(bias, (0, 1, 3, 2, 4)).astype(jnp.float32)
    scores = jnp.where(col_mask[:, None, :], scores + bias[None], NEG_INF)
    probs = jax.nn.softmax(scores, axis=(-2, -1)).astype(v.dtype)
    out = jnp.einsum('bhrckj,brkjhd->brchd', probs, vr)
    return out.reshape(b, s, nh * dh)


def fourier_mix(u):
    b, s, _ = u.shape
    ug = u.reshape(b, s, F_GROUPS, F_GROUP_DIM).astype(jnp.float32)
    y = jnp.fft.fft2(ug, axes=(1, 3), norm='ortho').real
    return y.reshape(b, s, F_WIDTH).astype(u.dtype)


def mixer(h, w_in, rpb, w_na_o, w_f_o, w_bg, b_bg, w_out):
    b, s, _ = h.shape
    proj = h @ w_in
    q, k, v, u = jnp.split(proj, [NA_WIDTH, 2 * NA_WIDTH, 3 * NA_WIDTH], axis=-1)
    q = q.reshape(b, s, NA_HEADS, NA_HEAD_DIM)
    k = k.reshape(b, s, NA_HEADS, NA_HEAD_DIM)
    v = v.reshape(b, s, NA_HEADS, NA_HEAD_DIM)
    y_na = neighborhood_attention(q, k, v, rpb) @ w_na_o
    y_f = fourier_mix(u) @ w_f_o
    gates = jax.nn.sigmoid(h @ w_bg + b_bg)
    g_na, g_f = jnp.split(gates, N_BRANCHES, axis=-1)
    return (g_na * y_na + g_f * y_f) @ w_out


def grouped_expert_ffn(hf, e_flat, w_flat, w_gate, w_up, w_down):
    t, d = hf.shape
    a = e_flat.shape[0]
    tok = jnp.repeat(jnp.arange(t, dtype=jnp.int32), TOP_K)
    order = jnp.argsort(e_flat)
    se = e_flat[order]
    stok = tok[order]
    sw = w_flat[order]
    counts = jnp.bincount(e_flat, length=N_EXPERTS)
    starts = jnp.cumsum(counts) - counts
    pcounts = (counts + EXPERT_BLOCK - 1) // EXPERT_BLOCK * EXPERT_BLOCK
    pends = jnp.cumsum(pcounts)
    pstarts = pends - pcounts
    rank = jnp.arange(a) - starts[se]
    dest = pstarts[se] + rank
    p = a + N_EXPERTS * EXPERT_BLOCK
    nb = p // EXPERT_BLOCK
    buf_tok = jnp.full((p,), t, dtype=jnp.int32).at[dest].set(stok)
    xpad = jnp.concatenate([hf, jnp.zeros((1, d), hf.dtype)], axis=0)
    xb = xpad[buf_tok].reshape(nb, EXPERT_BLOCK, d)
    block_e = jnp.minimum(jnp.searchsorted(pends, jnp.arange(nb) * EXPERT_BLOCK, side='right'), N_EXPERTS - 1)

    def expert_block(args):
        xblk, e = args
        gate = xblk @ w_gate[e]
        up = xblk @ w_up[e]
        return (jax.nn.silu(gate) * up) @ w_down[e]

    yb = lax.map(expert_block, (xb, block_e)).reshape(p, d)
    contrib = yb[dest] * sw[:, None].astype(yb.dtype)
    return jax.ops.segment_sum(contrib, stok, num_segments=t)


def hier_moe(h, wg_r, bg_r, we_r, be_r, w_gate, w_up, w_down):
    b, s, d = h.shape
    t = b * s
    hf = h.reshape(t, d)
    g_probs = jax.nn.softmax((hf @ wg_r + bg_r).astype(jnp.float32), axis=-1)
    g_top_p, g_top = lax.top_k(g_probs, 1)
    e_logits = (hf @ we_r + be_r).astype(jnp.float32).reshape(t, N_GROUPS, EXPERTS_PER_GROUP)
    sel = e_logits[jnp.arange(t), g_top[:, 0]]
    l_top, l_idx = lax.top_k(sel, TOP_K)
    weights = g_top_p * jax.nn.softmax(l_top, axis=-1)
    experts = (g_top * EXPERTS_PER_GROUP + l_idx).astype(jnp.int32)
    y = grouped_expert_ffn(hf, experts.reshape(-1), weights.reshape(-1), w_gate, w_up, w_down)
    return y.reshape(b, s, d).astype(h.dtype)


def _normal(k, shape, scale):
    return jax.random.normal(k, shape, jnp.float32) * scale


def setup_inputs(seed: int = 0) -> dict:
    key = jax.random.key(seed)
    ks = jax.random.split(key, 24)
    L, D = DEPTH, D_MODEL
    return {
        'x': _normal(ks[0], (BATCH, SEQ, D), 1.0),
        'c': _normal(ks[1], (BATCH, D), 1.0),
        'norm1_g': 1.0 + _normal(ks[2], (L, D), 0.02),
        'norm2_g': 1.0 + _normal(ks[3], (L, D), 0.02),
        'ada_w': _normal(ks[4], (L, D, N_MOD * D), 0.5 * D ** -0.5),
        'ada_b': _normal(ks[5], (L, N_MOD * D), 0.02),
        'mix_in_w': _normal(ks[6], (L, D, MIX_IN_WIDTH), D ** -0.5),
        'na_rpb': _normal(ks[7], (L, NA_HEADS, 2 * WIN_ROWS - 1, 2 * WIN_COLS - 1), 0.1),
        'na_out_w': _normal(ks[8], (L, NA_WIDTH, D), NA_WIDTH ** -0.5),
        'fourier_out_w': _normal(ks[9], (L, F_WIDTH, D), F_WIDTH ** -0.5),
        'branch_gate_w': _normal(ks[10], (L, D, N_BRANCHES * D), D ** -0.5),
        'branch_gate_b': _normal(ks[11], (L, N_BRANCHES * D), 0.02),
        'mix_out_w': _normal(ks[12], (L, D, D), D ** -0.5),
        'router_group_w': _normal(ks[13], (L, D, N_GROUPS), D ** -0.5),
        'router_group_b': _normal(ks[14], (L, N_GROUPS), 0.01),
        'router_expert_w': _normal(ks[15], (L, D, N_EXPERTS), D ** -0.5),
        'router_expert_b': _normal(ks[16], (L, N_EXPERTS), 0.01),
        'expert_w_gate': _normal(ks[17], (L, N_EXPERTS, D, D_EXPERT), D ** -0.5),
        'expert_w_up': _normal(ks[18], (L, N_EXPERTS, D, D_EXPERT), D ** -0.5),
        'expert_w_down': _normal(ks[19], (L, N_EXPERTS, D_EXPERT, D), D_EXPERT ** -0.5),
        'final_g': 1.0 + _normal(ks[20], (D,), 0.02),
    }


def reference(x, c, norm1_g, norm2_g, ada_w, ada_b, mix_in_w, na_rpb, na_out_w, fourier_out_w,
              branch_gate_w, branch_gate_b, mix_out_w, router_group_w, router_group_b,
              router_expert_w, router_expert_b, expert_w_gate, expert_w_up, expert_w_down, final_g):
    cond = jax.nn.silu(c)
    for l in range(DEPTH):
        mod = cond @ ada_w[l] + ada_b[l]
        sh1, sc1, g1, sh2, sc2, g2 = jnp.split(mod, N_MOD, axis=-1)
        h = modulate(rmsnorm(x, norm1_g[l]), sh1, sc1)
        x = x + g1[:, None, :] * mixer(h, mix_in_w[l], na_rpb[l], na_out_w[l], fourier_out_w[l],
                                       branch_gate_w[l], branch_gate_b[l], mix_out_w[l])
        h = modulate(rmsnorm(x, norm2_g[l]), sh2, sc2)
        x = x + g2[:, None, :] * hier_moe(h, router_group_w[l], router_group_b[l], router_expert_w[l],
                                          router_expert_b[l], expert_w_gate[l], expert_w_up[l],
                                          expert_w_down[l])
    return rmsnorm(x, final_g)
```

```python
import functools
import math

import numpy as np
import jax
import jax.numpy as jnp
from jax import lax
from jax.experimental import pallas as pl
from jax.experimental.pallas import tpu as pltpu

F32 = jnp.float32
BF16 = jnp.bfloat16

GRID_W = 64
NA_HEADS = 16
NA_HEAD_DIM = 64
WIN_ROWS = 8
WIN_COLS = 16
F_GROUPS = 4
F_GROUP_DIM = 256
N_GROUPS = 4
EXPERTS_PER_GROUP = 8
N_EXPERTS = N_GROUPS * EXPERTS_PER_GROUP
TOP_K = 2
N_MOD = 6
EPS = 1e-6
NEG_INF = -1e30

LANES = 128
SUBLANES = 8

ADA_TN = 1536
ADA_ROWS = 256
PRE_TM = 256
MM_TM = 1024
MM_TN = 512
ATTN_ROWS = 8
EXPERT_TB = 256
ROUTER_LANES = LANES

VMEM_LIMIT = 56 * 1024 * 1024


def _cparams(*sem):
    return pltpu.CompilerParams(dimension_semantics=sem, vmem_limit_bytes=VMEM_LIMIT)


def _sigmoid(x):
    return 1.0 / (1.0 + jnp.exp(-x))


def _ada_kernel(c_ref, w_ref, b_ref, o_ref, cs_ref):
    c = c_ref[...]
    cs_ref[...] = c * _sigmoid(c)
    d, tn = w_ref.shape

    def body(i, acc):
        r0 = pl.multiple_of(i * ADA_ROWS, ADA_ROWS)
        prod = w_ref[pl.ds(r0, ADA_ROWS), :] * cs_ref[pl.ds(r0, ADA_ROWS), :]
        return acc + jnp.sum(prod.reshape(ADA_ROWS // SUBLANES, SUBLANES, tn), axis=0)

    acc = lax.fori_loop(0, d // ADA_ROWS, body, jnp.zeros((SUBLANES, tn), F32))
    o_ref[...] = jnp.sum(acc, axis=0, keepdims=True) + b_ref[...]


def _ada_all(c, ada_w, ada_b):
    nl, d, n = ada_w.shape
    tn = min(ADA_TN, n)
    return pl.pallas_call(
        _ada_kernel,
        grid=(nl, n // tn),
        in_specs=[
            pl.BlockSpec((d, 1), lambda l, j: (0, 0)),
            pl.BlockSpec((None, d, tn), lambda l, j: (l, 0, j)),
            pl.BlockSpec((None, 1, tn), lambda l, j: (l, 0, j)),
        ],
        out_specs=pl.BlockSpec((None, 1, tn), lambda l, j: (l, 0, j)),
        out_shape=jax.ShapeDtypeStruct((nl, 1, n), F32),
        scratch_shapes=[pltpu.VMEM((d, 1), F32)],
        compiler_params=_cparams("parallel", "parallel"),
        name="ada_proj",
    )(c.reshape(d, 1), ada_w, ada_b.reshape(nl, 1, n))


def _rms(x, g):
    return x * lax.rsqrt(jnp.mean(x * x, axis=-1, keepdims=True) + EPS) * g


def _router(h, wr_ref, br_ref):
    w = wr_ref[...]
    h_hi = h.astype(BF16)
    h_lo = (h - h_hi.astype(F32)).astype(BF16)
    w_hi = w.astype(BF16)
    w_lo = (w - w_hi.astype(F32)).astype(BF16)
    dot = functools.partial(jnp.dot, preferred_element_type=F32)
    logits = dot(h_hi, w_hi) + (dot(h_hi, w_lo) + dot(h_lo, w_hi)) + br_ref[...]
    lane = lax.broadcasted_iota(jnp.int32, logits.shape, 1)
    big = jnp.int32(ROUTER_LANES)
    is_g = lane < N_GROUPS
    gl = jnp.where(is_g, logits, NEG_INF)
    gmax = jnp.max(gl, axis=-1, keepdims=True)
    gsum = jnp.sum(jnp.where(is_g, jnp.exp(gl - gmax), 0.0), axis=-1, keepdims=True)
    g_top_p = 1.0 / gsum
    g_top = jnp.min(jnp.where(is_g & (gl == gmax), lane, big), axis=-1, keepdims=True)
    e_lane = lane - N_GROUPS
    sel = (e_lane >= 0) & (e_lane < N_EXPERTS) & ((e_lane >> 3) == g_top)
    l1 = jnp.where(sel, logits, NEG_INF)
    m1 = jnp.max(l1, axis=-1, keepdims=True)
    i1 = jnp.min(jnp.where(sel & (l1 == m1), lane, big), axis=-1, keepdims=True)
    sel2 = sel & (lane != i1)
    l2 = jnp.where(sel2, logits, NEG_INF)
    m2 = jnp.max(l2, axis=-1, keepdims=True)
    i2 = jnp.min(jnp.where(sel2 & (l2 == m2), lane, big), axis=-1, keepdims=True)
    e21 = jnp.exp(m2 - m1)
    p1 = 1.0 / (1.0 + e21)
    p2 = e21 / (1.0 + e21)
    wts = jnp.where(lane == 0, g_top_p * p1, jnp.where(lane == 1, g_top_p * p2, 0.0))
    ids = jnp.where(lane == 0, i1 - N_GROUPS, jnp.where(lane == 1, i2 - N_GROUPS, 0))
    return wts, ids


def _pre_kernel(*refs, combine, modulate, router, emit_x, h_dtype):
    it = iter(refs)
    x_ref = next(it)
    if combine:
        o_ref, g2_ref = next(it), next(it)
    g_ref = next(it)
    if modulate:
        sh_ref, sc_ref = next(it), next(it)
    if router:
        wr_ref, br_ref = next(it), next(it)
    x = x_ref[...]
    if combine:
        x = x + g2_ref[...] * (o_ref[0] + o_ref[1])
    if emit_x:
        next(it)[...] = x
    h = _rms(x, g_ref[...])
    if modulate:
        h = h * (1.0 + sc_ref[...]) + sh_ref[...]
    next(it)[...] = h.astype(h_dtype)
    if router:
        wts, ids = _router(h, wr_ref, br_ref)
        next(it)[...] = wts
        next(it)[...] = ids


def _pre(x, g, *, o2=None, g2=None, shift=None, scale=None, wr=None, br=None, h_dtype=BF16):
    t, d = x.shape
    tm = min(PRE_TM, t)
    combine, modulate, router = o2 is not None, shift is not None, wr is not None
    row = pl.BlockSpec((tm, d), lambda i: (i, 0))
    vec = pl.BlockSpec((1, d), lambda i: (0, 0))
    args, specs = [x], [row]
    if combine:
        args += [o2.reshape(2, t, d), g2]
        specs += [pl.BlockSpec((2, tm, d), lambda i: (0, i, 0)), vec]
    args.append(g.reshape(1, d))
    specs.append(vec)
    if modulate:
        args += [shift, scale]
        specs += [vec, vec]
    if router:
        args += [wr, br]
        specs += [pl.BlockSpec(wr.shape, lambda i: (0, 0)), pl.BlockSpec(br.shape, lambda i: (0, 0))]
    out_shape, out_specs = [], []
    if combine:
        out_shape.append(jax.ShapeDtypeStruct((t, d), F32))
        out_specs.append(row)
    out_shape.append(jax.ShapeDtypeStruct((t, d), h_dtype))
    out_specs.append(row)
    if router:
        lane_row = pl.BlockSpec((tm, ROUTER_LANES), lambda i: (i, 0))
        out_shape += [jax.ShapeDtypeStruct((t, ROUTER_LANES), F32), jax.ShapeDtypeStruct((t, ROUTER_LANES), jnp.int32)]
        out_specs += [lane_row, lane_row]
    kern = functools.partial(_pre_kernel, combine=combine, modulate=modulate, router=router,
                             emit_x=combine, h_dtype=h_dtype)
    return pl.pallas_call(
        kern, grid=(t // tm,), in_specs=specs, out_specs=out_specs, out_shape=out_shape,
        compiler_params=_cparams("parallel"), name="pre_norm",
    )(*args)


def _mm_kernel(*refs, n_pairs, epilogue):
    o_ref = refs[-1]
    acc = None
    for p in range(n_pairs):
        part = jnp.dot(refs[2 * p][...].astype(BF16), refs[2 * p + 1][...].astype(BF16),
                       preferred_element_type=F32)
        acc = part if acc is None else (acc, part)
    extras = [r[...] for r in refs[2 * n_pairs:-1]]
    o_ref[...] = epilogue(acc, *extras).astype(o_ref.dtype)


def _mm(pairs, extras, epilogue, n, out_dtype, name):
    m = pairs[0][0].shape[0]
    tm, tn = min(MM_TM, m), min(MM_TN, n)
    args, specs = [], []
    for a, w, off in pairs:
        k = a.shape[1]
        args += [a, w]
        specs += [pl.BlockSpec((tm, k), lambda i, j: (i, 0)),
                  pl.BlockSpec((k, tn), lambda i, j, off=off: (0, j + off // tn))]
    for arr, bshape, imap in extras:
        args.append(arr)
        specs.append(pl.BlockSpec(bshape(tm, tn), imap(tn)))
    return pl.pallas_call(
        functools.partial(_mm_kernel, n_pairs=len(pairs), epilogue=epilogue),
        grid=(m // tm, n // tn), in_specs=specs,
        out_specs=pl.BlockSpec((tm, tn), lambda i, j: (i, j)),
        out_shape=jax.ShapeDtypeStruct((m, n), out_dtype),
        compiler_params=_cparams("parallel", "parallel"), name=name,
    )(*args)


def _tile(tm, tn):
    return (tm, tn)


def _rowvec(tm, tn):
    return (1, tn)


def _at_cols(off):
    return lambda tn: (lambda i, j: (i, j + off // tn))


def _vec_cols(off):
    return lambda tn: (lambda i, j: (0, j + off // tn))


def _attn_kernel(q_ref, k_ref, v_ref, b_ref, o_ref, *, n_rows):
    rb = pl.program_id(1)
    wkeys = WIN_ROWS * GRID_W
    dh = NA_HEAD_DIM

    def row(i, carry):
        r = rb * ATTN_ROWS + i
        rs = jnp.clip(r - WIN_ROWS // 2, 0, n_rows - WIN_ROWS)
        var = rs - r + (WIN_ROWS - 1)
        kw = k_ref[pl.ds(rs, WIN_ROWS)].reshape(wkeys, 2 * dh)
        vw = v_ref[pl.ds(rs, WIN_ROWS)].reshape(wkeys, 2 * dh)
        q = q_ref[i]
        outs = []
        for hh in range(2):
            sl = slice(dh * hh, dh * (hh + 1))
            s = lax.dot_general(q[:, sl], kw[:, sl], (((1,), (1,)), ((), ())),
                                preferred_element_type=F32) * (dh ** -0.5)
            b = b_ref[hh, var]
            s = jnp.where(b > 0.5 * NEG_INF, s + b, NEG_INF)
            m = jnp.max(s, axis=-1, keepdims=True)
            p = jnp.exp(s - m)
            l = jnp.sum(p, axis=-1, keepdims=True)
            outs.append(jnp.dot(p.astype(BF16), vw[:, sl], preferred_element_type=F32) / l)
        o_ref[i] = jnp.concatenate(outs, axis=-1).astype(o_ref.dtype)
        return carry

    lax.fori_loop(0, ATTN_ROWS, row, 0)


def _na_bias_table(rpb):
    h = rpb.shape[0]
    qc = jnp.arange(GRID_W)[:, None]
    kc = jnp.arange(GRID_W)[None, :]
    cs = jnp.clip(qc - WIN_COLS // 2, 0, GRID_W - WIN_COLS)
    cmask = (kc >= cs) & (kc < cs + WIN_COLS)
    dc = jnp.clip(kc - qc, -(WIN_COLS - 1), WIN_COLS - 1) + (WIN_COLS - 1)
    dr = jnp.arange(WIN_ROWS)[:, None] + jnp.arange(WIN_ROWS)[None, :]
    t = rpb[:, dr][:, :, :, dc]
    t = jnp.where(cmask, t, NEG_INF)
    return jnp.transpose(t, (0, 1, 3, 2, 4)).reshape(h, WIN_ROWS, GRID_W, WIN_ROWS * GRID_W)


def _attention(proj, bias_tab):
    t, width = proj.shape
    n_rows = t // GRID_W
    pairs = NA_HEADS // 2
    p3 = proj.reshape(n_rows, GRID_W, width)
    blk = 2 * NA_HEAD_DIM
    rows = min(ATTN_ROWS, n_rows)
    out = pl.pallas_call(
        functools.partial(_attn_kernel, n_rows=n_rows),
        grid=(pairs, n_rows // rows),
        in_specs=[
            pl.BlockSpec((rows, GRID_W, blk), lambda hp, rb: (rb, 0, hp)),
            pl.BlockSpec((n_rows, GRID_W, blk), lambda hp, rb: (0, 0, pairs + hp)),
            pl.BlockSpec((n_rows, GRID_W, blk), lambda hp, rb: (0, 0, 2 * pairs + hp)),
            pl.BlockSpec((2, WIN_ROWS, GRID_W, WIN_ROWS * GRID_W), lambda hp, rb: (hp, 0, 0, 0)),
        ],
        out_specs=pl.BlockSpec((rows, GRID_W, blk), lambda hp, rb: (rb, 0, hp)),
        out_shape=jax.ShapeDtypeStruct((n_rows, GRID_W, NA_HEADS * NA_HEAD_DIM), BF16),
        compiler_params=_cparams("parallel", "arbitrary"), name="na_attention",
    )(p3, p3, p3, bias_tab)
    return out.reshape(t, NA_HEADS * NA_HEAD_DIM)


def _dft_tables(n1, n2, dc):
    s = n1 * n2
    a = np.arange(dc, dtype=np.float64)
    ang_c = 2 * np.pi * np.outer(a, a) / dc
    w_c = np.concatenate([np.cos(ang_c), -np.sin(ang_c)], axis=1)
    a1 = np.arange(n1, dtype=np.float64)
    ang1 = 2 * np.pi * np.outer(a1, a1) / n1
    c1, s1 = np.cos(ang1), np.sin(ang1)
    m1 = np.block([[c1, s1], [-s1, c1]])
    a2 = np.arange(n2, dtype=np.float64)
    ang_t = 2 * np.pi * np.outer(a2, a1) / s
    tw_c = np.repeat(np.cos(ang_t)[:, :, None], LANES, axis=2)
    tw_s = np.repeat(np.sin(ang_t)[:, :, None], LANES, axis=2)
    ang2 = 2 * np.pi * np.outer(a2, a2) / n2
    m2 = np.concatenate([np.cos(ang2), np.sin(ang2)], axis=1)
    return (jnp.asarray(w_c, BF16), jnp.asarray(m1, BF16), jnp.asarray(tw_c, F32), jnp.asarray(tw_s, F32),
            jnp.asarray(m2, BF16))


def _fft1_kernel(ua_ref, ub_ref, wc_ref, m1_ref, tc_ref, ts_ref, br_ref, bi_ref, *, n1, n2):
    j = pl.program_id(2)
    ch = br_ref.shape[1]
    rows = pl.ds(j, n1, stride=n2)
    u = jnp.concatenate([ua_ref[rows, :], ub_ref[rows, :]], axis=1).astype(BF16)
    z = jnp.dot(u, wc_ref[...], preferred_element_type=F32)
    zz = jnp.concatenate([z[:, :ch], z[:, ch:]], axis=0).astype(BF16)
    a = jnp.dot(m1_ref[...], zz, preferred_element_type=F32)
    ar, ai = a[:n1], a[n1:]
    tc, ts = tc_ref[...], ts_ref[...]
    br_ref[rows, :] = ar * tc + ai * ts
    bi_ref[rows, :] = ai * tc - ar * ts


def _fft2_kernel(br_ref, bi_ref, m2_ref, o_ref, *, n1, n2, scale):
    k1 = pl.program_id(1)
    bb = jnp.concatenate([br_ref[...], bi_ref[...]], axis=0).astype(BF16)
    y = jnp.dot(m2_ref[...], bb, preferred_element_type=F32)
    o_ref[pl.ds(k1, n2, stride=n1), :] = y * scale


def _fourier(u):
    t, fw = u.shape
    n2 = GRID_W
    n1 = t // n2
    dc = F_GROUP_DIM
    ch = LANES
    halves = dc // ch
    w_c, m1, tw_c, tw_s, m2 = _dft_tables(n1, n2, dc)
    w_c = w_c.reshape(dc, 2, halves, ch).transpose(2, 0, 1, 3).reshape(halves, dc, 2 * ch)
    assert halves == 2
    out_blk = pl.BlockSpec((t, ch), lambda g, hf, j: (0, g * halves + hf))
    b_re, b_im = pl.pallas_call(
        functools.partial(_fft1_kernel, n1=n1, n2=n2),
        grid=(F_GROUPS, halves, n2),
        in_specs=[
            pl.BlockSpec((t, ch), lambda g, hf, j: (0, halves * g)),
            pl.BlockSpec((t, ch), lambda g, hf, j: (0, halves * g + 1)),
            pl.BlockSpec((None, dc, 2 * ch), lambda g, hf, j: (hf, 0, 0)),
            pl.BlockSpec((2 * n1, 2 * n1), lambda g, hf, j: (0, 0)),
            pl.BlockSpec((None, n1, LANES), lambda g, hf, j: (j, 0, 0)),
            pl.BlockSpec((None, n1, LANES), lambda g, hf, j: (j, 0, 0)),
        ],
        out_specs=[out_blk, out_blk],
        out_shape=[jax.ShapeDtypeStruct((t, fw), F32)] * 2,
        compiler_params=_cparams("parallel", "parallel", "arbitrary"), name="fourier_stage1",
    )(u, u, w_c, m1, tw_c, tw_s)
    scale = 1.0 / math.sqrt(t * dc)
    in_blk = pl.BlockSpec((n2, ch), lambda cb, k: (k, cb))
    return pl.pallas_call(
        functools.partial(_fft2_kernel, n1=n1, n2=n2, scale=scale),
        grid=(fw // ch, n1),
        in_specs=[in_blk, in_blk, pl.BlockSpec((n2, 2 * n2), lambda cb, k: (0, 0))],
        out_specs=pl.BlockSpec((t, ch), lambda cb, k: (0, cb)),
        out_shape=jax.ShapeDtypeStruct((t, fw), F32),
        compiler_params=_cparams("parallel", "arbitrary"), name="fourier_stage2",
    )(b_re, b_im, m2)


def _expert_kernel(tok_ref, row_ref, be_ref, nv_ref, na_ref,
                   h_hbm, sw_ref, wg_ref, wu_ref, wd_ref, o_hbm,
                   xbuf, ybuf, gsem, ssem):
    b = pl.program_id(0)
    n_act = na_ref[0]
    slot = b % 2
    tb = xbuf.shape[1]

    def gather_copy(blk, sl, s):
        tok = tok_ref[blk * tb + s]
        return pltpu.make_async_copy(h_hbm.at[pl.ds(tok, 1)], xbuf.at[sl, pl.ds(s, 1)], gsem.at[sl])

    def scatter_copy(blk, sl, s):
        r = row_ref[blk * tb + s]
        return pltpu.make_async_copy(ybuf.at[sl, pl.ds(s, 1)], o_hbm.at[pl.ds(r, 1)], ssem.at[sl])

    def for_valid(blk, fn):
        lax.fori_loop(0, nv_ref[blk], lambda s, c: (fn(s), c)[1], 0)

    @pl.when(b == 0)
    def _():
        xbuf[...] = jnp.zeros_like(xbuf)
        for_valid(0, lambda s: gather_copy(0, 0, s).start())

    @pl.when(b < n_act)
    def _():
        @pl.when(b + 1 < n_act)
        def _():
            for_valid(b + 1, lambda s: gather_copy(b + 1, 1 - slot, s).start())

        for_valid(b, lambda s: gather_copy(b, slot, s).wait())
        x = xbuf[slot].astype(BF16)
        gate = jnp.dot(x, wg_ref[...].astype(BF16), preferred_element_type=F32)
        up = jnp.dot(x, wu_ref[...].astype(BF16), preferred_element_type=F32)
        act = (gate * _sigmoid(gate) * up).astype(BF16)
        y = jnp.dot(act, wd_ref[...].astype(BF16), preferred_element_type=F32)
        ybuf[slot] = y * sw_ref[...]

        @pl.when(b >= 1)
        def _():
            for_valid(b - 1, lambda s: scatter_copy(b - 1, 1 - slot, s).wait())

        for_valid(b, lambda s: scatter_copy(b, slot, s).start())

        @pl.when(b == n_act - 1)
        def _():
            for_valid(b, lambda s: scatter_copy(b, slot, s).wait())


def _moe_plan(experts, weights, t, tb):
    a = t * TOP_K
    e_flat = experts.reshape(a)
    w_flat = weights.reshape(a)
    onehot = (e_flat[:, None] == jnp.arange(N_EXPERTS, dtype=jnp.int32)[None, :]).astype(jnp.int32)
    csum = jnp.cumsum(onehot, axis=0)
    counts = csum[-1]
    rank = jnp.sum((csum - 1) * onehot, axis=1)
    pcounts = (counts + tb - 1) // tb * tb
    pends = jnp.cumsum(pcounts)
    pstarts = pends - pcounts
    dest = pstarts[e_flat] + rank
    p = a + N_EXPERTS * tb
    nb = p // tb
    ids = jnp.arange(a, dtype=jnp.int32)
    tok = jnp.zeros((p,), jnp.int32).at[dest].set(ids // TOP_K)
    out_row = jnp.zeros((p,), jnp.int32).at[dest].set((ids % TOP_K) * t + ids // TOP_K)
    sw = jnp.zeros((p,), F32).at[dest].set(w_flat)
    n_act = (pends[-1] // tb).astype(jnp.int32)
    blk = jnp.arange(nb, dtype=jnp.int32)
    be = jnp.minimum(jnp.searchsorted(pends, blk * tb, side='right'), N_EXPERTS - 1).astype(jnp.int32)
    be = jnp.where(blk < n_act, be, be[jnp.maximum(n_act - 1, 0)])
    nvalid = jnp.clip(counts[be] - (blk * tb - pstarts[be]), 0, tb)
    nvalid = jnp.where(blk < n_act, nvalid, 0).astype(jnp.int32)
    return tok, out_row, be, nvalid, n_act.reshape(1), sw.reshape(p, 1)


def _experts(h, experts, weights, w_gate, w_up, w_down):
    t, d = h.shape
    de = w_gate.shape[2]
    tb = EXPERT_TB
    tok, out_row, be, nvalid, n_act, sw = _moe_plan(experts, weights, t, tb)
    nb = tok.shape[0] // tb
    grid_spec = pltpu.PrefetchScalarGridSpec(
        num_scalar_prefetch=5,
        grid=(nb,),
        in_specs=[
            pl.BlockSpec(memory_space=pl.ANY),
            pl.BlockSpec((tb, 1), lambda b, *_: (b, 0)),
            pl.BlockSpec((None, d, de), lambda b, tok, row, be, nv, na: (be[b], 0, 0)),
            pl.BlockSpec((None, d, de), lambda b, tok, row, be, nv, na: (be[b], 0, 0)),
            pl.BlockSpec((None, de, d), lambda b, tok, row, be, nv, na: (be[b], 0, 0)),
        ],
        out_specs=pl.BlockSpec(memory_space=pl.ANY),
        scratch_shapes=[
            pltpu.VMEM((2, tb, d), F32),
            pltpu.VMEM((2, tb, d), F32),
            pltpu.SemaphoreType.DMA((2,)),
            pltpu.SemaphoreType.DMA((2,)),
        ],
    )
    return pl.pallas_call(
        _expert_kernel, grid_spec=grid_spec,
        out_shape=jax.ShapeDtypeStruct((TOP_K * t, d), F32),
        compiler_params=_cparams("arbitrary"), name="moe_experts",
    )(tok, out_row, be, nvalid, n_act, h, sw, w_gate, w_up, w_down)


def kernel(x, c, norm1_g, norm2_g, ada_w, ada_b, mix_in_w, na_rpb, na_out_w, fourier_out_w, branch_gate_w,
           branch_gate_b, mix_out_w, router_group_w, router_group_b, router_expert_w, router_expert_b,
           expert_w_gate, expert_w_up, expert_w_down, final_g):
    bsz, s, d = x.shape
    assert bsz == 1 and s % GRID_W == 0
    depth = ada_w.shape[0]
    na_w = NA_HEADS * NA_HEAD_DIM
    f_w = F_GROUPS * F_GROUP_DIM
    xs = x.reshape(s, d)
    mod = _ada_all(c, ada_w, ada_b)
    o2 = g2 = None
    for l in range(depth):
        sh1, sc1, g1, sh2, sc2, g2_l = [mod[l, :, i * d:(i + 1) * d] for i in range(N_MOD)]
        if o2 is None:
            (h,) = _pre(xs, norm1_g[l], shift=sh1, scale=sc1)
        else:
            xs, h = _pre(xs, norm1_g[l], o2=o2, g2=g2, shift=sh1, scale=sc1)
        qkv = _mm([(h, mix_in_w[l], 0)], [], lambda acc: acc, 3 * na_w, BF16, "mix_in_qkv")
        u = _mm([(h, mix_in_w[l], 3 * na_w)], [], lambda acc: acc, f_w, F32, "mix_in_fourier")
        gates = _mm([(h, branch_gate_w[l], 0)],
                    [(branch_gate_b[l].reshape(1, -1), _rowvec, _vec_cols(0))],
                    lambda acc, bias: _sigmoid(acc + bias), branch_gate_w.shape[2], F32, "branch_gates")
        y_att = _attention(qkv, _na_bias_table(na_rpb[l]))
        y_fft = _fourier(u)
        mixed = _mm([(y_att, na_out_w[l], 0), (y_fft, fourier_out_w[l], 0)],
                    [(gates, _tile, _at_cols(0)), (gates, _tile, _at_cols(d))],
                    lambda acc, ga, gf: ga * acc[0] + gf * acc[1], d, BF16, "branch_mix")
        xs = _mm([(mixed, mix_out_w[l], 0)],
                 [(xs, _tile, _at_cols(0)), (g1, _rowvec, _vec_cols(0))],
                 lambda acc, xr, g: xr + g * acc, d, F32, "mix_out")
        wr = jnp.concatenate([router_group_w[l], router_expert_w[l],
                              jnp.zeros((d, ROUTER_LANES - N_GROUPS - N_EXPERTS), F32)], axis=1)
        br = jnp.concatenate([router_group_b[l], router_expert_b[l],
                              jnp.zeros((ROUTER_LANES - N_GROUPS - N_EXPERTS,), F32)]).reshape(1, ROUTER_LANES)
        h2, wts, ids = _pre(xs, norm2_g[l], shift=sh2, scale=sc2, wr=wr, br=br, h_dtype=F32)
        o2 = _experts(h2, ids[:, :TOP_K], wts[:, :TOP_K], expert_w_gate[l], expert_w_up[l], expert_w_down[l])
        g2 = g2_l
    _, out = _pre(xs, final_g, o2=o2, g2=g2, h_dtype=F32)
    return out.reshape(bsz, s, d)
```

```python
import functools
import math

import numpy as np
import jax
import jax.numpy as jnp
from jax import lax
from jax.experimental import pallas as pl
from jax.experimental.pallas import tpu as pltpu

F32 = jnp.float32
BF16 = jnp.bfloat16

GRID_W = 64
NA_HEADS = 16
NA_HEAD_DIM = 64
WIN_ROWS = 8
WIN_COLS = 16
F_GROUPS = 4
F_GROUP_DIM = 256
N_GROUPS = 4
EXPERTS_PER_GROUP = 8
N_EXPERTS = N_GROUPS * EXPERTS_PER_GROUP
TOP_K = 2
N_MOD = 6
EPS = 1e-6
NEG_INF = -1e30

LANES = 128
SUBLANES = 8

ADA_TN = 1536
ADA_ROWS = 256
PRE_TM = 256
MM_TM = 1024
MM_TN = 512
ATTN_ROWS = 8
EXPERT_TB = 256
FFT_JB = 4
FFT_KB = 8
ROUTER_LANES = LANES

VMEM_LIMIT = 56 * 1024 * 1024


def _cparams(*sem):
    return pltpu.CompilerParams(dimension_semantics=sem, vmem_limit_bytes=VMEM_LIMIT)


def _sigmoid(x):
    return 1.0 / (1.0 + jnp.exp(-x))


def _ada_kernel(c_ref, w_ref, b_ref, o_ref, cs_ref):
    c = c_ref[...]
    cs_ref[...] = c * _sigmoid(c)
    d, tn = w_ref.shape

    def body(i, acc):
        r0 = pl.multiple_of(i * ADA_ROWS, ADA_ROWS)
        prod = w_ref[pl.ds(r0, ADA_ROWS), :] * cs_ref[pl.ds(r0, ADA_ROWS), :]
        return acc + jnp.sum(prod.reshape(ADA_ROWS // SUBLANES, SUBLANES, tn), axis=0)

    acc = lax.fori_loop(0, d // ADA_ROWS, body, jnp.zeros((SUBLANES, tn), F32))
    o_ref[...] = jnp.sum(acc, axis=0, keepdims=True) + b_ref[...]


def _ada_all(c, ada_w, ada_b):
    nl, d, n = ada_w.shape
    tn = min(ADA_TN, n)
    return pl.pallas_call(
        _ada_kernel,
        grid=(nl, n // tn),
        in_specs=[
            pl.BlockSpec((d, 1), lambda l, j: (0, 0)),
            pl.BlockSpec((None, d, tn), lambda l, j: (l, 0, j)),
            pl.BlockSpec((None, 1, tn), lambda l, j: (l, 0, j)),
        ],
        out_specs=pl.BlockSpec((None, 1, tn), lambda l, j: (l, 0, j)),
        out_shape=jax.ShapeDtypeStruct((nl, 1, n), F32),
        scratch_shapes=[pltpu.VMEM((d, 1), F32)],
        compiler_params=_cparams("parallel", "parallel"),
        name="ada_proj",
    )(c.reshape(d, 1), ada_w, ada_b.reshape(nl, 1, n))


def _rms(x, g):
    return x * lax.rsqrt(jnp.mean(x * x, axis=-1, keepdims=True) + EPS) * g


def _router(h, wr_ref, br_ref):
    w = wr_ref[...]
    h_hi = h.astype(BF16)
    h_lo = (h - h_hi.astype(F32)).astype(BF16)
    w_hi = w.astype(BF16)
    w_lo = (w - w_hi.astype(F32)).astype(BF16)
    dot = functools.partial(jnp.dot, preferred_element_type=F32)
    logits = dot(h_hi, w_hi) + (dot(h_hi, w_lo) + dot(h_lo, w_hi)) + br_ref[...]
    lane = lax.broadcasted_iota(jnp.int32, logits.shape, 1)
    big = jnp.int32(ROUTER_LANES)
    is_g = lane < N_GROUPS
    gl = jnp.where(is_g, logits, NEG_INF)
    gmax = jnp.max(gl, axis=-1, keepdims=True)
    gsum = jnp.sum(jnp.where(is_g, jnp.exp(gl - gmax), 0.0), axis=-1, keepdims=True)
    g_top_p = 1.0 / gsum
    g_top = jnp.min(jnp.where(is_g & (gl == gmax), lane, big), axis=-1, keepdims=True)
    e_lane = lane - N_GROUPS
    sel = (e_lane >= 0) & (e_lane < N_EXPERTS) & ((e_lane >> 3) == g_top)
    l1 = jnp.where(sel, logits, NEG_INF)
    m1 = jnp.max(l1, axis=-1, keepdims=True)
    i1 = jnp.min(jnp.where(sel & (l1 == m1), lane, big), axis=-1, keepdims=True)
    sel2 = sel & (lane != i1)
    l2 = jnp.where(sel2, logits, NEG_INF)
    m2 = jnp.max(l2, axis=-1, keepdims=True)
    i2 = jnp.min(jnp.where(sel2 & (l2 == m2), lane, big), axis=-1, keepdims=True)
    e21 = jnp.exp(m2 - m1)
    p1 = 1.0 / (1.0 + e21)
    p2 = e21 / (1.0 + e21)
    wts = jnp.where(lane == 0, g_top_p * p1, jnp.where(lane == 1, g_top_p * p2, 0.0))
    ids = jnp.where(lane == 0, i1 - N_GROUPS, jnp.where(lane == 1, i2 - N_GROUPS, 0))
    return wts, ids


def _pre_kernel(*refs, combine, modulate, router, emit_h, h_dtype):
    it = iter(refs)
    if combine:
        dest_ref = next(it)
    x_ref = next(it)
    if combine:
        y_hbm, cw_ref, g2_ref = next(it), next(it), next(it)
    g_ref = next(it)
    if modulate:
        sh_ref, sc_ref = next(it), next(it)
    if router:
        wr_ref, br_ref = next(it), next(it)
    outs = [next(it) for _ in range(int(combine) + int(emit_h) + 2 * int(router))]
    x = x_ref[...]
    if combine:
        gbuf, gsem = next(it), next(it)
        i = pl.program_id(0)
        slot = i % 2
        tm = x_ref.shape[0]

        def row_copy(step, sl, r, k):
            src = dest_ref[TOP_K * (step * tm + r) + k]
            return pltpu.make_async_copy(y_hbm.at[pl.ds(src, 1)], gbuf.at[sl, k, pl.ds(r, 1)], gsem.at[sl])

        def for_rows(fn):
            def body(r, carry):
                for k in range(TOP_K):
                    fn(r, k)
                return carry
            lax.fori_loop(0, tm, body, 0)

        @pl.when(i == 0)
        def _():
            for_rows(lambda r, k: row_copy(0, 0, r, k).start())

        @pl.when(i + 1 < pl.num_programs(0))
        def _():
            for_rows(lambda r, k: row_copy(i + 1, 1 - slot, r, k).start())

        for_rows(lambda r, k: row_copy(i, slot, r, k).wait())
        cw = cw_ref[...]
        x = x + g2_ref[...] * (cw[:, 0:1] * gbuf[slot, 0] + cw[:, 1:2] * gbuf[slot, 1])
        outs.pop(0)[...] = x
    h = _rms(x, g_ref[...])
    if modulate:
        h = h * (1.0 + sc_ref[...]) + sh_ref[...]
    if emit_h:
        outs.pop(0)[...] = h.astype(h_dtype)
    if router:
        wts, ids = _router(h, wr_ref, br_ref)
        outs.pop(0)[...] = wts
        outs.pop(0)[...] = ids


def _pre(x, g, *, moe=None, g2=None, shift=None, scale=None, wr=None, br=None, h_dtype=BF16):
    t, d = x.shape
    tm = min(PRE_TM, t)
    combine, modulate, router = moe is not None, shift is not None, wr is not None
    row = pl.BlockSpec((tm, d), lambda i, *_: (i, 0))
    vec = pl.BlockSpec((1, d), lambda i, *_: (0, 0))
    lane_row = pl.BlockSpec((tm, ROUTER_LANES), lambda i, *_: (i, 0))
    prefetch, args, specs, scratch = [], [x], [row], []
    if combine:
        yb, dest, cw = moe
        prefetch.append(dest)
        args += [yb, cw, g2]
        specs += [pl.BlockSpec(memory_space=pl.ANY), lane_row, vec]
        scratch = [pltpu.VMEM((2, TOP_K, tm, d), F32), pltpu.SemaphoreType.DMA((2,))]
    args.append(g.reshape(1, d))
    specs.append(vec)
    if modulate:
        args += [shift, scale]
        specs += [vec, vec]
    if router:
        args += [wr, br]
        specs += [pl.BlockSpec(wr.shape, lambda i, *_: (0, 0)), pl.BlockSpec(br.shape, lambda i, *_: (0, 0))]
    out_shape, out_specs = [], []
    if combine:
        out_shape.append(jax.ShapeDtypeStruct((t, d), F32))
        out_specs.append(row)
    out_shape.append(jax.ShapeDtypeStruct((t, d), h_dtype))
    out_specs.append(row)
    if router:
        out_shape += [jax.ShapeDtypeStruct((t, ROUTER_LANES), F32), jax.ShapeDtypeStruct((t, ROUTER_LANES), jnp.int32)]
        out_specs += [lane_row, lane_row]
    kern = functools.partial(_pre_kernel, combine=combine, modulate=modulate, router=router,
                             emit_h=True, h_dtype=h_dtype)
    grid_spec = pltpu.PrefetchScalarGridSpec(
        num_scalar_prefetch=len(prefetch), grid=(t // tm,), in_specs=specs, out_specs=out_specs,
        scratch_shapes=scratch)
    return pl.pallas_call(
        kern, grid_spec=grid_spec, out_shape=out_shape,
        compiler_params=_cparams("arbitrary" if combine else "parallel"), name="pre_norm",
    )(*prefetch, *args)


def _mm_kernel(*refs, n_pairs, epilogue):
    o_ref = refs[-1]
    acc = None
    for p in range(n_pairs):
        part = jnp.dot(refs[2 * p][...].astype(BF16), refs[2 * p + 1][...].astype(BF16),
                       preferred_element_type=F32)
        acc = part if acc is None else (acc, part)
    extras = [r[...] for r in refs[2 * n_pairs:-1]]
    o_ref[...] = epilogue(acc, *extras).astype(o_ref.dtype)


def _mm(pairs, extras, epilogue, n, out_dtype, name):
    m = pairs[0][0].shape[0]
    tm, tn = min(MM_TM, m), min(MM_TN, n)
    args, specs = [], []
    for a, w, layer, off in pairs:
        k = a.shape[1]
        args += [a, w]
        specs += [pl.BlockSpec((tm, k), lambda i, j: (i, 0)),
                  pl.BlockSpec((None, k, tn), lambda i, j, layer=layer, off=off: (layer, 0, j + off // tn))]
    for arr, bshape, imap in extras:
        args.append(arr)
        specs.append(pl.BlockSpec(bshape(tm, tn), imap(tn)))
    return pl.pallas_call(
        functools.partial(_mm_kernel, n_pairs=len(pairs), epilogue=epilogue),
        grid=(m // tm, n // tn), in_specs=specs,
        out_specs=pl.BlockSpec((tm, tn), lambda i, j: (i, j)),
        out_shape=jax.ShapeDtypeStruct((m, n), out_dtype),
        compiler_params=_cparams("parallel", "parallel"), name=name,
    )(*args)


def _tile(tm, tn):
    return (tm, tn)


def _rowvec(tm, tn):
    return (1, tn)


def _at_cols(off):
    return lambda tn: (lambda i, j: (i, j + off // tn))


def _vec_cols(off):
    return lambda tn: (lambda i, j: (0, j + off // tn))


def _attn_kernel(q_ref, k_ref, v_ref, b_ref, o_ref, *, n_rows):
    rb = pl.program_id(1)
    wkeys = WIN_ROWS * GRID_W
    dh = NA_HEAD_DIM

    def row(i, carry):
        r = rb * ATTN_ROWS + i
        rs = jnp.clip(r - WIN_ROWS // 2, 0, n_rows - WIN_ROWS)
        var = rs - r + (WIN_ROWS - 1)
        kw = k_ref[pl.ds(rs, WIN_ROWS)].reshape(wkeys, 2 * dh)
        vw = v_ref[pl.ds(rs, WIN_ROWS)].reshape(wkeys, 2 * dh)
        q = q_ref[i]
        outs = []
        for hh in range(2):
            sl = slice(dh * hh, dh * (hh + 1))
            s = lax.dot_general(q[:, sl], kw[:, sl], (((1,), (1,)), ((), ())),
                                preferred_element_type=F32) * (dh ** -0.5)
            b = b_ref[hh, var]
            s = jnp.where(b > 0.5 * NEG_INF, s + b, NEG_INF)
            m = jnp.max(s, axis=-1, keepdims=True)
            p = jnp.exp(s - m)
            l = jnp.sum(p, axis=-1, keepdims=True)
            outs.append(jnp.dot(p.astype(BF16), vw[:, sl], preferred_element_type=F32) / l)
        o_ref[i] = jnp.concatenate(outs, axis=-1).astype(o_ref.dtype)
        return carry

    lax.fori_loop(0, ATTN_ROWS, row, 0)


def _na_bias_table(rpb):
    h = rpb.shape[0]
    qc = jnp.arange(GRID_W)[:, None]
    kc = jnp.arange(GRID_W)[None, :]
    cs = jnp.clip(qc - WIN_COLS // 2, 0, GRID_W - WIN_COLS)
    cmask = (kc >= cs) & (kc < cs + WIN_COLS)
    dc = jnp.clip(kc - qc, -(WIN_COLS - 1), WIN_COLS - 1) + (WIN_COLS - 1)
    dr = jnp.arange(WIN_ROWS)[:, None] + jnp.arange(WIN_ROWS)[None, :]
    t = rpb[:, dr][:, :, :, dc]
    t = jnp.where(cmask, t, NEG_INF)
    return jnp.transpose(t, (0, 1, 3, 2, 4)).reshape(h, WIN_ROWS, GRID_W, WIN_ROWS * GRID_W)


def _attention(proj, bias_tab, layer):
    t, width = proj.shape
    n_rows = t // GRID_W
    pairs = NA_HEADS // 2
    p3 = proj.reshape(n_rows, GRID_W, width)
    blk = 2 * NA_HEAD_DIM
    rows = min(ATTN_ROWS, n_rows)
    out = pl.pallas_call(
        functools.partial(_attn_kernel, n_rows=n_rows),
        grid=(pairs, n_rows // rows),
        in_specs=[
            pl.BlockSpec((rows, GRID_W, blk), lambda hp, rb: (rb, 0, hp)),
            pl.BlockSpec((n_rows, GRID_W, blk), lambda hp, rb: (0, 0, pairs + hp)),
            pl.BlockSpec((n_rows, GRID_W, blk), lambda hp, rb: (0, 0, 2 * pairs + hp)),
            pl.BlockSpec((None, 2, WIN_ROWS, GRID_W, WIN_ROWS * GRID_W), lambda hp, rb: (layer, hp, 0, 0, 0)),
        ],
        out_specs=pl.BlockSpec((rows, GRID_W, blk), lambda hp, rb: (rb, 0, hp)),
        out_shape=jax.ShapeDtypeStruct((n_rows, GRID_W, NA_HEADS * NA_HEAD_DIM), BF16),
        compiler_params=_cparams("parallel", "arbitrary"), name="na_attention",
    )(p3, p3, p3, bias_tab)
    return out.reshape(t, NA_HEADS * NA_HEAD_DIM)


def _dft_tables(n1, n2, dc):
    s = n1 * n2
    a = np.arange(dc, dtype=np.float64)
    ang_c = 2 * np.pi * np.outer(a, a) / dc
    w_c = np.concatenate([np.cos(ang_c), -np.sin(ang_c)], axis=1)
    a1 = np.arange(n1, dtype=np.float64)
    ang1 = 2 * np.pi * np.outer(a1, a1) / n1
    c1, s1 = np.cos(ang1), np.sin(ang1)
    m1 = np.block([[c1, s1], [-s1, c1]])
    a2 = np.arange(n2, dtype=np.float64)
    ang_t = 2 * np.pi * np.outer(a2, a1) / s
    tw_c = np.repeat(np.cos(ang_t)[:, :, None], LANES, axis=2)
    tw_s = np.repeat(np.sin(ang_t)[:, :, None], LANES, axis=2)
    ang2 = 2 * np.pi * np.outer(a2, a2) / n2
    m2 = np.concatenate([np.cos(ang2), np.sin(ang2)], axis=1)
    return (jnp.asarray(w_c, BF16), jnp.asarray(m1, BF16), jnp.asarray(tw_c, F32), jnp.asarray(tw_s, F32),
            jnp.asarray(m2, BF16))


def _fft1_kernel(u_hbm, wc_ref, m1_ref, tc_ref, ts_ref, br_hbm, bi_hbm,
                 ubuf, zz_ref, obuf, isem, osem, *, n1, dc):
    s = pl.program_id(0)
    ns = pl.num_programs(0)
    jb = ubuf.shape[1]
    slot = s % 2

    def in_copy(step, sl, jj):
        return pltpu.make_async_copy(u_hbm.at[:, step * jb + jj], ubuf.at[sl, jj], isem.at[sl])

    def out_copies(step, sl, jj):
        j = step * jb + jj
        return (pltpu.make_async_copy(obuf.at[sl, 0, jj], br_hbm.at[:, j], osem.at[sl]),
                pltpu.make_async_copy(obuf.at[sl, 1, jj], bi_hbm.at[:, j], osem.at[sl]))

    @pl.when(s == 0)
    def _():
        for jj in range(jb):
            in_copy(0, 0, jj).start()

    @pl.when(s + 1 < ns)
    def _():
        for jj in range(jb):
            in_copy(s + 1, 1 - slot, jj).start()

    for jj in range(jb):
        in_copy(s, slot, jj).wait()

    @pl.when(s >= 2)
    def _():
        for jj in range(jb):
            for cp in out_copies(s - 2, slot, jj):
                cp.wait()

    c = ubuf.shape[3]
    u = ubuf[slot].reshape(jb * n1, c).astype(BF16)
    for g in range(c // dc):
        z = jnp.dot(u[:, g * dc:(g + 1) * dc], wc_ref[...], preferred_element_type=F32)
        z = z.astype(BF16)
        for jj in range(jb):
            zz_ref[jj, :n1, g * dc:(g + 1) * dc] = z[jj * n1:(jj + 1) * n1, :dc]
            zz_ref[jj, n1:, g * dc:(g + 1) * dc] = z[jj * n1:(jj + 1) * n1, dc:]
    for jj in range(jb):
        a = jnp.dot(m1_ref[...], zz_ref[jj], preferred_element_type=F32)
        ar, ai = a[:n1], a[n1:]
        tc = jnp.concatenate([tc_ref[jj]] * (c // LANES), axis=1)
        ts = jnp.concatenate([ts_ref[jj]] * (c // LANES), axis=1)
        obuf[slot, 0, jj] = ar * tc + ai * ts
        obuf[slot, 1, jj] = ai * tc - ar * ts
    for jj in range(jb):
        for cp in out_copies(s, slot, jj):
            cp.start()

    @pl.when(s == ns - 1)
    def _():
        for jj in range(jb):
            for cp in out_copies(s, slot, jj):
                cp.wait()

        @pl.when(s >= 1)
        def _():
            for jj in range(jb):
                for cp in out_copies(s - 1, 1 - slot, jj):
                    cp.wait()


def _fft2_kernel(br_ref, bi_ref, m2_ref, y_hbm, obuf, osem, *, n2, scale):
    s = pl.program_id(0)
    ns = pl.num_programs(0)
    kb = obuf.shape[1]
    slot = s % 2

    def out_copy(step, sl, kk):
        return pltpu.make_async_copy(obuf.at[sl, kk], y_hbm.at[:, step * kb + kk], osem.at[sl])

    @pl.when(s >= 2)
    def _():
        for kk in range(kb):
            out_copy(s - 2, slot, kk).wait()

    for kk in range(kb):
        rows = slice(kk * n2, (kk + 1) * n2)
        bb = jnp.concatenate([br_ref[rows, :], bi_ref[rows, :]], axis=0).astype(BF16)
        obuf[slot, kk] = jnp.dot(m2_ref[...], bb, preferred_element_type=F32) * scale
    for kk in range(kb):
        out_copy(s, slot, kk).start()

    @pl.when(s == ns - 1)
    def _():
        for kk in range(kb):
            out_copy(s, slot, kk).wait()

        @pl.when(s >= 1)
        def _():
            for kk in range(kb):
                out_copy(s - 1, 1 - slot, kk).wait()


def _fourier(u):
    t, fw = u.shape
    n2 = GRID_W
    n1 = t // n2
    dc = F_GROUP_DIM
    w_c, m1, tw_c, tw_s, m2 = _dft_tables(n1, n2, dc)
    jb = min(FFT_JB, n2)
    kb = min(FFT_KB, n1)
    any_spec = pl.BlockSpec(memory_space=pl.ANY)
    tw_spec = pl.BlockSpec((jb, n1, LANES), lambda s: (s, 0, 0))
    b_re, b_im = pl.pallas_call(
        functools.partial(_fft1_kernel, n1=n1, dc=dc),
        grid=(n2 // jb,),
        in_specs=[
            any_spec,
            pl.BlockSpec((dc, 2 * dc), lambda s: (0, 0)),
            pl.BlockSpec((2 * n1, 2 * n1), lambda s: (0, 0)),
            tw_spec, tw_spec,
        ],
        out_specs=[any_spec, any_spec],
        out_shape=[jax.ShapeDtypeStruct((n1, n2, fw), F32)] * 2,
        scratch_shapes=[
            pltpu.VMEM((2, jb, n1, fw), F32),
            pltpu.VMEM((jb, 2 * n1, fw), BF16),
            pltpu.VMEM((2, 2, jb, n1, fw), F32),
            pltpu.SemaphoreType.DMA((2,)),
            pltpu.SemaphoreType.DMA((2,)),
        ],
        compiler_params=_cparams("arbitrary"), name="fourier_stage1",
    )(u.reshape(n1, n2, fw), w_c, m1, tw_c, tw_s)
    scale = 1.0 / math.sqrt(t * dc)
    in_blk = pl.BlockSpec((kb * n2, fw), lambda s: (s, 0))
    y = pl.pallas_call(
        functools.partial(_fft2_kernel, n2=n2, scale=scale),
        grid=(n1 // kb,),
        in_specs=[in_blk, in_blk, pl.BlockSpec((n2, 2 * n2), lambda s: (0, 0))],
        out_specs=any_spec,
        out_shape=jax.ShapeDtypeStruct((n2, n1, fw), F32),
        scratch_shapes=[pltpu.VMEM((2, kb, n2, fw), F32), pltpu.SemaphoreType.DMA((2,))],
        compiler_params=_cparams("arbitrary"), name="fourier_stage2",
    )(b_re.reshape(t, fw), b_im.reshape(t, fw), m2)
    return y.reshape(t, fw)


def _dispatch_kernel(dest_ref, pad0_ref, padn_ref, na_ref, h_ref, xb_hbm, zbuf, sem, zsem):
    i = pl.program_id(0)
    tm = h_ref.shape[0]
    tb = zbuf.shape[0]
    nb = xb_hbm.shape[0] // tb

    def tail_copies(fn):
        def body(b, carry):
            fn(pltpu.make_async_copy(zbuf, xb_hbm.at[pl.ds(pl.multiple_of(b * tb, tb), tb)], zsem))
            return carry
        lax.fori_loop(na_ref[0], nb, body, 0)

    def pad_copies(e, fn):
        pos = pad0_ref[e]
        n = padn_ref[e]
        head = jnp.minimum((-pos) & (SUBLANES - 1), n)
        for r in range(SUBLANES - 1):
            @pl.when(r < head)
            def _(r=r):
                fn(pltpu.make_async_copy(zbuf.at[pl.ds(0, 1)], xb_hbm.at[pl.ds(pos + r, 1)], zsem))
        pos = pos + head
        n = n - head
        bit = tb // 2
        while bit >= SUBLANES:
            @pl.when((n & bit) != 0)
            def _(pos=pos, bit=bit):
                start = pl.multiple_of(pos, SUBLANES)
                fn(pltpu.make_async_copy(zbuf.at[pl.ds(0, bit)], xb_hbm.at[pl.ds(start, bit)], zsem))
            pos = pos + (n & bit)
            bit //= 2

    @pl.when(i == 0)
    def _():
        zbuf[...] = jnp.zeros_like(zbuf)
        lax.fori_loop(0, N_EXPERTS, lambda e, c: (pad_copies(e, lambda cp: cp.start()), c)[1], 0)
        tail_copies(lambda cp: cp.start())

    def row_copy(r, k):
        dst = dest_ref[TOP_K * (i * tm + r) + k]
        return pltpu.make_async_copy(h_ref.at[pl.ds(r, 1)], xb_hbm.at[pl.ds(dst, 1)], sem)

    def for_rows(fn):
        def body(r, carry):
            for k in range(TOP_K):
                fn(r, k)
            return carry
        lax.fori_loop(0, tm, body, 0)

    for_rows(lambda r, k: row_copy(r, k).start())
    for_rows(lambda r, k: row_copy(r, k).wait())

    @pl.when(i == 0)
    def _():
        lax.fori_loop(0, N_EXPERTS, lambda e, c: (pad_copies(e, lambda cp: cp.wait()), c)[1], 0)
        tail_copies(lambda cp: cp.wait())


def _expert_kernel(be_ref, bi_ref, na_ref, x_ref, wg_ref, wu_ref, wd_ref, y_ref):
    active = pl.program_id(0) < na_ref[0]

    @pl.when(active)
    def _():
        x = x_ref[...].astype(BF16)
        gate = jnp.dot(x, wg_ref[...].astype(BF16), preferred_element_type=F32)
        up = jnp.dot(x, wu_ref[...].astype(BF16), preferred_element_type=F32)
        act = (gate * _sigmoid(gate) * up).astype(BF16)
        y_ref[...] = jnp.dot(act, wd_ref[...].astype(BF16), preferred_element_type=F32)

    @pl.when(jnp.logical_not(active))
    def _():
        y_ref[...] = jnp.zeros_like(y_ref)


def _moe_plan(experts, t, tb):
    a = t * TOP_K
    e_flat = experts.reshape(a)
    lanes = jnp.arange(N_EXPERTS, dtype=jnp.int32)[None, :]
    onehot = (e_flat[:, None] == lanes).astype(jnp.int32)
    csum = jnp.cumsum(onehot, axis=0)
    counts = csum[-1]
    pcounts = (counts + tb - 1) // tb * tb
    pends = jnp.cumsum(pcounts)
    pstarts = pends - pcounts
    dest = jnp.sum(onehot * (pstarts[None, :] + csum - 1), axis=1).astype(jnp.int32)
    nb = (a + N_EXPERTS * tb) // tb
    n_act = (pends[-1] // tb).astype(jnp.int32)
    blk = jnp.arange(nb, dtype=jnp.int32)
    blk_idx = jnp.minimum(blk, n_act - 1)
    be = jnp.sum((pends[None, :] <= (blk_idx * tb)[:, None]).astype(jnp.int32), axis=1)
    be = jnp.minimum(be, N_EXPERTS - 1).astype(jnp.int32)
    return dest, be, blk_idx, n_act.reshape(1), (pstarts + counts).astype(jnp.int32), (pcounts - counts).astype(jnp.int32)


def _experts(h, experts, layer, w_gate, w_up, w_down):
    t, d = h.shape
    de = w_gate.shape[3]
    tb = EXPERT_TB
    tm = min(PRE_TM, t)
    dest, be, blk_idx, n_act, pad0, padn = _moe_plan(experts, t, tb)
    nb = be.shape[0]
    p = nb * tb
    xb = pl.pallas_call(
        _dispatch_kernel,
        grid_spec=pltpu.PrefetchScalarGridSpec(
            num_scalar_prefetch=4, grid=(t // tm,),
            in_specs=[pl.BlockSpec((tm, d), lambda i, *_: (i, 0))],
            out_specs=pl.BlockSpec(memory_space=pl.ANY),
            scratch_shapes=[pltpu.VMEM((tb, d), F32), pltpu.SemaphoreType.DMA(()), pltpu.SemaphoreType.DMA(())]),
        out_shape=jax.ShapeDtypeStruct((p, d), F32),
        compiler_params=_cparams("arbitrary"), name="moe_dispatch",
    )(dest, pad0, padn, n_act, h)
    slot_blk = pl.BlockSpec((tb, d), lambda b, be, bi, na: (bi[b], 0))
    yb = pl.pallas_call(
        _expert_kernel,
        grid_spec=pltpu.PrefetchScalarGridSpec(
            num_scalar_prefetch=3, grid=(nb,),
            in_specs=[
                slot_blk,
                pl.BlockSpec((None, None, d, de), lambda b, be, bi, na: (layer, be[b], 0, 0)),
                pl.BlockSpec((None, None, d, de), lambda b, be, bi, na: (layer, be[b], 0, 0)),
                pl.BlockSpec((None, None, de, d), lambda b, be, bi, na: (layer, be[b], 0, 0)),
            ],
            out_specs=pl.BlockSpec((tb, d), lambda b, be, bi, na: (b, 0))),
        out_shape=jax.ShapeDtypeStruct((p, d), F32),
        compiler_params=_cparams("arbitrary"), name="moe_experts",
    )(be, blk_idx, n_act, xb, w_gate, w_up, w_down)
    return yb, dest


def kernel(x, c, norm1_g, norm2_g, ada_w, ada_b, mix_in_w, na_rpb, na_out_w, fourier_out_w, branch_gate_w,
           branch_gate_b, mix_out_w, router_group_w, router_group_b, router_expert_w, router_expert_b,
           expert_w_gate, expert_w_up, expert_w_down, final_g):
    bsz, s, d = x.shape
    assert bsz == 1 and s % GRID_W == 0
    depth = ada_w.shape[0]
    na_w = NA_HEADS * NA_HEAD_DIM
    f_w = F_GROUPS * F_GROUP_DIM
    xs = x.reshape(s, d)
    mod = _ada_all(c, ada_w, ada_b)
    bias_tab = _na_bias_table(na_rpb.reshape((-1,) + na_rpb.shape[2:]))
    bias_tab = bias_tab.reshape((depth, NA_HEADS) + bias_tab.shape[1:])
    pad = ROUTER_LANES - N_GROUPS - N_EXPERTS
    wr_all = jnp.concatenate([router_group_w, router_expert_w, jnp.zeros((depth, d, pad), F32)], axis=2)
    br_all = jnp.concatenate([router_group_b, router_expert_b, jnp.zeros((depth, pad), F32)], axis=1)
    moe = g2 = None
    for l in range(depth):
        sh1, sc1, g1, sh2, sc2, g2_l = [mod[l, :, i * d:(i + 1) * d] for i in range(N_MOD)]
        if moe is None:
            (h,) = _pre(xs, norm1_g[l], shift=sh1, scale=sc1)
        else:
            xs, h = _pre(xs, norm1_g[l], moe=moe, g2=g2, shift=sh1, scale=sc1)
        qkv = _mm([(h, mix_in_w, l, 0)], [], lambda acc: acc, 3 * na_w, BF16, "mix_in_qkv")
        u = _mm([(h, mix_in_w, l, 3 * na_w)], [], lambda acc: acc, f_w, F32, "mix_in_fourier")
        gates = _mm([(h, branch_gate_w, l, 0)],
                    [(branch_gate_b[l].reshape(1, -1), _rowvec, _vec_cols(0))],
                    lambda acc, bias: _sigmoid(acc + bias), branch_gate_w.shape[2], F32, "branch_gates")
        y_att = _attention(qkv, bias_tab, l)
        y_fft = _fourier(u)
        mixed = _mm([(y_att, na_out_w, l, 0), (y_fft, fourier_out_w, l, 0)],
                    [(gates, _tile, _at_cols(0)), (gates, _tile, _at_cols(d))],
                    lambda acc, ga, gf: ga * acc[0] + gf * acc[1], d, BF16, "branch_mix")
        xs = _mm([(mixed, mix_out_w, l, 0)],
                 [(xs, _tile, _at_cols(0)), (g1, _rowvec, _vec_cols(0))],
                 lambda acc, xr, g: xr + g * acc, d, F32, "mix_out")
        h2, wts, ids = _pre(xs, norm2_g[l], shift=sh2, scale=sc2, wr=wr_all[l], br=br_all[l].reshape(1, -1),
                            h_dtype=F32)
        yb, dest = _experts(h2, ids[:, :TOP_K], l, expert_w_gate, expert_w_up, expert_w_down)
        moe, g2 = (yb, dest, wts), g2_l
    _, out = _pre(xs, final_g, moe=moe, g2=g2, h_dtype=F32)
    return out.reshape(bsz, s, d)
```

```python
import functools
import math

import numpy as np
import jax
import jax.numpy as jnp
from jax import lax
from jax.experimental import pallas as pl
from jax.experimental.pallas import tpu as pltpu

F32 = jnp.float32
BF16 = jnp.bfloat16

GRID_W = 64
NA_HEADS = 16
NA_HEAD_DIM = 64
WIN_ROWS = 8
WIN_COLS = 16
F_GROUPS = 4
F_GROUP_DIM = 256
N_GROUPS = 4
EXPERTS_PER_GROUP = 8
N_EXPERTS = N_GROUPS * EXPERTS_PER_GROUP
TOP_K = 2
N_MOD = 6
EPS = 1e-6
NEG_INF = -1e30

LANES = 128
SUBLANES = 8

ADA_TN = 1536
ADA_ROWS = 256
PRE_TM = 256
MM_TM = 1024
MM_TN = 512
ATTN_ROWS = 8
EXPERT_TB = 256
ROW_DMA_UNROLL = 4
FFT_JB = 4
FFT_KB = 8
ROUTER_LANES = LANES

VMEM_LIMIT = 56 * 1024 * 1024


def _cparams(*sem):
    return pltpu.CompilerParams(dimension_semantics=sem, vmem_limit_bytes=VMEM_LIMIT)


def _sigmoid(x):
    return 1.0 / (1.0 + jnp.exp(-x))


def _ada_kernel(c_ref, w_ref, b_ref, o_ref, cs_ref):
    c = c_ref[...]
    cs_ref[...] = c * _sigmoid(c)
    d, tn = w_ref.shape

    def body(i, acc):
        r0 = pl.multiple_of(i * ADA_ROWS, ADA_ROWS)
        prod = w_ref[pl.ds(r0, ADA_ROWS), :] * cs_ref[pl.ds(r0, ADA_ROWS), :]
        return acc + jnp.sum(prod.reshape(ADA_ROWS // SUBLANES, SUBLANES, tn), axis=0)

    acc = lax.fori_loop(0, d // ADA_ROWS, body, jnp.zeros((SUBLANES, tn), F32))
    o_ref[...] = jnp.sum(acc, axis=0, keepdims=True) + b_ref[...]


def _ada_all(c, ada_w, ada_b):
    nl, d, n = ada_w.shape
    tn = min(ADA_TN, n)
    return pl.pallas_call(
        _ada_kernel,
        grid=(nl, n // tn),
        in_specs=[
            pl.BlockSpec((d, 1), lambda l, j: (0, 0)),
            pl.BlockSpec((None, d, tn), lambda l, j: (l, 0, j)),
            pl.BlockSpec((None, 1, tn), lambda l, j: (l, 0, j)),
        ],
        out_specs=pl.BlockSpec((None, 1, tn), lambda l, j: (l, 0, j)),
        out_shape=jax.ShapeDtypeStruct((nl, 1, n), F32),
        scratch_shapes=[pltpu.VMEM((d, 1), F32)],
        compiler_params=_cparams("parallel", "parallel"),
        name="ada_proj",
    )(c.reshape(d, 1), ada_w, ada_b.reshape(nl, 1, n))


def _rms(x, g):
    return x * lax.rsqrt(jnp.mean(x * x, axis=-1, keepdims=True) + EPS) * g


def _router(h, wr_ref, br_ref):
    w = wr_ref[...]
    h_hi = h.astype(BF16)
    h_lo = (h - h_hi.astype(F32)).astype(BF16)
    w_hi = w.astype(BF16)
    w_lo = (w - w_hi.astype(F32)).astype(BF16)
    dot = functools.partial(jnp.dot, preferred_element_type=F32)
    logits = dot(h_hi, w_hi) + (dot(h_hi, w_lo) + dot(h_lo, w_hi)) + br_ref[...]
    lane = lax.broadcasted_iota(jnp.int32, logits.shape, 1)
    big = jnp.int32(ROUTER_LANES)
    is_g = lane < N_GROUPS
    gl = jnp.where(is_g, logits, NEG_INF)
    gmax = jnp.max(gl, axis=-1, keepdims=True)
    gsum = jnp.sum(jnp.where(is_g, jnp.exp(gl - gmax), 0.0), axis=-1, keepdims=True)
    g_top_p = 1.0 / gsum
    g_top = jnp.min(jnp.where(is_g & (gl == gmax), lane, big), axis=-1, keepdims=True)
    e_lane = lane - N_GROUPS
    sel = (e_lane >= 0) & (e_lane < N_EXPERTS) & ((e_lane >> 3) == g_top)
    l1 = jnp.where(sel, logits, NEG_INF)
    m1 = jnp.max(l1, axis=-1, keepdims=True)
    i1 = jnp.min(jnp.where(sel & (l1 == m1), lane, big), axis=-1, keepdims=True)
    sel2 = sel & (lane != i1)
    l2 = jnp.where(sel2, logits, NEG_INF)
    m2 = jnp.max(l2, axis=-1, keepdims=True)
    i2 = jnp.min(jnp.where(sel2 & (l2 == m2), lane, big), axis=-1, keepdims=True)
    e21 = jnp.exp(m2 - m1)
    p1 = 1.0 / (1.0 + e21)
    p2 = e21 / (1.0 + e21)
    wts = jnp.where(lane == 0, g_top_p * p1, jnp.where(lane == 1, g_top_p * p2, 0.0))
    ids = jnp.where(lane == 0, i1 - N_GROUPS, jnp.where(lane == 1, i2 - N_GROUPS, 0))
    return wts, ids


def _load_token_major(ref, n_tok, chunks):
    return jnp.concatenate([ref[pl.ds(c, n_tok, stride=chunks), :] for c in range(chunks)], axis=1)


def _store_token_major(ref, val):
    n_tok, d = val.shape
    chunks = d // LANES
    for c in range(chunks):
        ref[pl.ds(c, n_tok, stride=chunks), :] = val[:, c * LANES:(c + 1) * LANES]


def _token_rows(tok, chunks):
    return pl.ds(pl.multiple_of(tok * chunks, chunks), chunks)


def _pre_kernel(*refs, combine, modulate, router, emit_h, h_dtype):
    it = iter(refs)
    if combine:
        dest_ref = next(it)
    x_ref = next(it)
    if combine:
        y_hbm, cw_ref, g2_ref = next(it), next(it), next(it)
    g_ref = next(it)
    if modulate:
        sh_ref, sc_ref = next(it), next(it)
    if router:
        wr_ref, br_ref = next(it), next(it)
    outs = [next(it) for _ in range(int(combine) + int(emit_h) + 2 * int(router))]
    x = x_ref[...]
    if combine:
        gbuf, gsem = next(it), next(it)
        i = pl.program_id(0)
        slot = i % 2
        tm, d = x_ref.shape
        chunks = d // LANES

        def row_copy(step, sl, r, k):
            src = dest_ref[TOP_K * (step * tm + r) + k]
            return pltpu.make_async_copy(y_hbm.at[_token_rows(src, chunks)],
                                         gbuf.at[sl, k, _token_rows(r, chunks)], gsem.at[sl])

        def for_rows(fn):
            def body(r, carry):
                for k in range(TOP_K):
                    fn(r, k)
                return carry
            lax.fori_loop(0, tm, body, 0, unroll=ROW_DMA_UNROLL)

        @pl.when(i == 0)
        def _():
            for_rows(lambda r, k: row_copy(0, 0, r, k).start())

        @pl.when(i + 1 < pl.num_programs(0))
        def _():
            for_rows(lambda r, k: row_copy(i + 1, 1 - slot, r, k).start())

        for_rows(lambda r, k: row_copy(i, slot, r, k).wait())
        cw = cw_ref[...]
        y0 = _load_token_major(gbuf.at[slot, 0], tm, chunks)
        y1 = _load_token_major(gbuf.at[slot, 1], tm, chunks)
        x = x + g2_ref[...] * (cw[:, 0:1] * y0 + cw[:, 1:2] * y1)
        outs.pop(0)[...] = x
    h = _rms(x, g_ref[...])
    if modulate:
        h = h * (1.0 + sc_ref[...]) + sh_ref[...]
    if emit_h and router:
        _store_token_major(outs.pop(0), h)
    elif emit_h:
        outs.pop(0)[...] = h.astype(h_dtype)
    if router:
        wts, ids = _router(h, wr_ref, br_ref)
        outs.pop(0)[...] = wts
        outs.pop(0)[...] = ids


def _pre(x, g, *, moe=None, g2=None, shift=None, scale=None, wr=None, br=None, h_dtype=BF16):
    t, d = x.shape
    tm = min(PRE_TM, t)
    chunks = d // LANES
    combine, modulate, router = moe is not None, shift is not None, wr is not None
    row = pl.BlockSpec((tm, d), lambda i, *_: (i, 0))
    vec = pl.BlockSpec((1, d), lambda i, *_: (0, 0))
    lane_row = pl.BlockSpec((tm, ROUTER_LANES), lambda i, *_: (i, 0))
    prefetch, args, specs, scratch = [], [x], [row], []
    if combine:
        yb, dest, cw = moe
        prefetch.append(dest)
        args += [yb, cw, g2]
        specs += [pl.BlockSpec(memory_space=pl.ANY), lane_row, vec]
        scratch = [pltpu.VMEM((2, TOP_K, tm * chunks, LANES), F32), pltpu.SemaphoreType.DMA((2,))]
    args.append(g.reshape(1, d))
    specs.append(vec)
    if modulate:
        args += [shift, scale]
        specs += [vec, vec]
    if router:
        args += [wr, br]
        specs += [pl.BlockSpec(wr.shape, lambda i, *_: (0, 0)), pl.BlockSpec(br.shape, lambda i, *_: (0, 0))]
    out_shape, out_specs = [], []
    if combine:
        out_shape.append(jax.ShapeDtypeStruct((t, d), F32))
        out_specs.append(row)
    if router:
        out_shape.append(jax.ShapeDtypeStruct((t * chunks, LANES), F32))
        out_specs.append(pl.BlockSpec((tm * chunks, LANES), lambda i, *_: (i, 0)))
    else:
        out_shape.append(jax.ShapeDtypeStruct((t, d), h_dtype))
        out_specs.append(row)
    if router:
        out_shape += [jax.ShapeDtypeStruct((t, ROUTER_LANES), F32), jax.ShapeDtypeStruct((t, ROUTER_LANES), jnp.int32)]
        out_specs += [lane_row, lane_row]
    kern = functools.partial(_pre_kernel, combine=combine, modulate=modulate, router=router,
                             emit_h=True, h_dtype=h_dtype)
    grid_spec = pltpu.PrefetchScalarGridSpec(
        num_scalar_prefetch=len(prefetch), grid=(t // tm,), in_specs=specs, out_specs=out_specs,
        scratch_shapes=scratch)
    return pl.pallas_call(
        kern, grid_spec=grid_spec, out_shape=out_shape,
        compiler_params=_cparams("arbitrary" if combine else "parallel"), name="pre_norm",
    )(*prefetch, *args)


def _mm_kernel(*refs, n_pairs, epilogue):
    o_ref = refs[-1]
    acc = None
    for p in range(n_pairs):
        part = jnp.dot(refs[2 * p][...].astype(BF16), refs[2 * p + 1][...].astype(BF16),
                       preferred_element_type=F32)
        acc = part if acc is None else (acc, part)
    extras = [r[...] for r in refs[2 * n_pairs:-1]]
    o_ref[...] = epilogue(acc, *extras).astype(o_ref.dtype)


def _mm(pairs, extras, epilogue, n, out_dtype, name):
    m = pairs[0][0].shape[0]
    tm, tn = min(MM_TM, m), min(MM_TN, n)
    args, specs = [], []
    for a, w, layer, off in pairs:
        k = a.shape[1]
        args += [a, w]
        specs += [pl.BlockSpec((tm, k), lambda i, j: (i, 0)),
                  pl.BlockSpec((None, k, tn), lambda i, j, layer=layer, off=off: (layer, 0, j + off // tn))]
    for arr, bshape, imap in extras:
        args.append(arr)
        specs.append(pl.BlockSpec(bshape(tm, tn), imap(tn)))
    return pl.pallas_call(
        functools.partial(_mm_kernel, n_pairs=len(pairs), epilogue=epilogue),
        grid=(m // tm, n // tn), in_specs=specs,
        out_specs=pl.BlockSpec((tm, tn), lambda i, j: (i, j)),
        out_shape=jax.ShapeDtypeStruct((m, n), out_dtype),
        compiler_params=_cparams("parallel", "parallel"), name=name,
    )(*args)


def _tile(tm, tn):
    return (tm, tn)


def _rowvec(tm, tn):
    return (1, tn)


def _at_cols(off):
    return lambda tn: (lambda i, j: (i, j + off // tn))


def _vec_cols(off):
    return lambda tn: (lambda i, j: (0, j + off // tn))


def _attn_bias_fill(rpb_ref, b_ref, head0):
    n_dr, n_dc = 2 * WIN_ROWS - 1, 2 * WIN_COLS - 1
    qc = lax.broadcasted_iota(jnp.int32, (GRID_W, 2 * GRID_W), 0)
    kc = lax.broadcasted_iota(jnp.int32, (GRID_W, 2 * GRID_W), 1) & (GRID_W - 1)
    cs = jnp.clip(qc - WIN_COLS // 2, 0, GRID_W - WIN_COLS)
    inside = (kc >= cs) & (kc < cs + WIN_COLS)
    dc = jnp.clip(kc - qc, -(WIN_COLS - 1), WIN_COLS - 1) + (WIN_COLS - 1)
    for h in range(2):
        base = (head0 + h) * (n_dr * n_dc)
        for dr in range(n_dr):
            t = jnp.zeros((GRID_W, 2 * GRID_W), F32)
            for d in range(n_dc):
                t = jnp.where(dc == d, rpb_ref[base + dr * n_dc + d], t)
            t = jnp.where(inside, t, NEG_INF)
            for var in range(WIN_ROWS):
                j = dr - var
                if 0 <= j < WIN_ROWS:
                    c0 = j * GRID_W
                    lo = c0 % (2 * GRID_W)
                    b_ref[var, h * GRID_W:(h + 1) * GRID_W, c0:c0 + GRID_W] = t[:, lo:lo + GRID_W]


def _attn_kernel(rpb_ref, q_ref, k_ref, v_ref, o_ref, b_ref, s_ref, *, n_rows):
    rb = pl.program_id(1)
    n_q = q_ref.shape[0]
    wkeys = WIN_ROWS * GRID_W
    dh = NA_HEAD_DIM
    lane = lax.broadcasted_iota(jnp.int32, (GRID_W, 2 * dh), 1)
    first = lane < dh

    @pl.when(rb == 0)
    def _():
        _attn_bias_fill(rpb_ref, b_ref, 2 * pl.program_id(0))

    def window(i):
        r = rb * n_q + i
        rs = jnp.clip(r - WIN_ROWS // 2, 0, n_rows - WIN_ROWS)
        return rs, rs - r + (WIN_ROWS - 1)

    for i in range(n_q):
        rs, var = window(i)
        kw = k_ref[pl.ds(rs, WIN_ROWS)].reshape(wkeys, 2 * dh)
        q = q_ref[i]
        zero = jnp.zeros_like(q)
        q2 = jnp.concatenate([jnp.where(first, q, zero), jnp.where(first, zero, q)], axis=0)
        s = lax.dot_general(q2, kw, (((1,), (1,)), ((), ())), preferred_element_type=F32) * (dh ** -0.5)
        b = b_ref[var]
        s_ref[i] = jnp.where(b > 0.5 * NEG_INF, s + b, NEG_INF)
    for i in range(n_q):
        rs, _ = window(i)
        vw = v_ref[pl.ds(rs, WIN_ROWS)].reshape(wkeys, 2 * dh)
        s = s_ref[i]
        m = jnp.max(s, axis=-1, keepdims=True)
        p = jnp.exp(s - m)
        l = jnp.sum(p, axis=-1, keepdims=True)
        o = jnp.dot(p.astype(BF16), vw, preferred_element_type=F32) / l
        o_ref[i] = jnp.where(first, o[:GRID_W], o[GRID_W:]).astype(o_ref.dtype)


def _attention(proj, rpb):
    t, width = proj.shape
    n_rows = t // GRID_W
    pairs = NA_HEADS // 2
    p3 = proj.reshape(n_rows, GRID_W, width)
    blk = 2 * NA_HEAD_DIM
    rows = min(ATTN_ROWS, n_rows)
    wkeys = WIN_ROWS * GRID_W
    out = pl.pallas_call(
        functools.partial(_attn_kernel, n_rows=n_rows),
        grid_spec=pltpu.PrefetchScalarGridSpec(
            num_scalar_prefetch=1, grid=(pairs, n_rows // rows),
            in_specs=[
                pl.BlockSpec((rows, GRID_W, blk), lambda hp, rb, _: (rb, 0, hp)),
                pl.BlockSpec((n_rows, GRID_W, blk), lambda hp, rb, _: (0, 0, pairs + hp)),
                pl.BlockSpec((n_rows, GRID_W, blk), lambda hp, rb, _: (0, 0, 2 * pairs + hp)),
            ],
            out_specs=pl.BlockSpec((rows, GRID_W, blk), lambda hp, rb, _: (rb, 0, hp)),
            scratch_shapes=[pltpu.VMEM((WIN_ROWS, 2 * GRID_W, wkeys), F32),
                            pltpu.VMEM((rows, 2 * GRID_W, wkeys), F32)]),
        out_shape=jax.ShapeDtypeStruct((n_rows, GRID_W, NA_HEADS * NA_HEAD_DIM), BF16),
        compiler_params=_cparams("arbitrary", "arbitrary"), name="na_attention",
    )(rpb.reshape(-1), p3, p3, p3)
    return out.reshape(t, NA_HEADS * NA_HEAD_DIM)


def _dft_tables(n1, n2, dc):
    s = n1 * n2
    a = np.arange(dc, dtype=np.float64)
    ang_c = 2 * np.pi * np.outer(a, a) / dc
    w_c = np.concatenate([np.cos(ang_c), -np.sin(ang_c)], axis=1)
    a1 = np.arange(n1, dtype=np.float64)
    ang1 = 2 * np.pi * np.outer(a1, a1) / n1
    c1, s1 = np.cos(ang1), np.sin(ang1)
    m1 = np.block([[c1, s1], [-s1, c1]])
    a2 = np.arange(n2, dtype=np.float64)
    ang_t = 2 * np.pi * np.outer(a2, a1) / s
    tw_c = np.repeat(np.cos(ang_t)[:, :, None], LANES, axis=2)
    tw_s = np.repeat(np.sin(ang_t)[:, :, None], LANES, axis=2)
    ang2 = 2 * np.pi * np.outer(a2, a2) / n2
    m2 = np.concatenate([np.cos(ang2), np.sin(ang2)], axis=1)
    return (jnp.asarray(w_c, BF16), jnp.asarray(m1, BF16), jnp.asarray(tw_c, F32), jnp.asarray(tw_s, F32),
            jnp.asarray(m2, BF16))


def _fft1_kernel(u_hbm, wc_ref, m1_ref, tc_ref, ts_ref, br_hbm, bi_hbm,
                 ubuf, zz_ref, obuf, isem, osem, *, n1, dc):
    s = pl.program_id(0)
    ns = pl.num_programs(0)
    jb = ubuf.shape[1]
    slot = s % 2

    def in_copy(step, sl, jj):
        return pltpu.make_async_copy(u_hbm.at[:, step * jb + jj], ubuf.at[sl, jj], isem.at[sl])

    def out_copies(step, sl, jj):
        j = step * jb + jj
        return (pltpu.make_async_copy(obuf.at[sl, 0, jj], br_hbm.at[:, j], osem.at[sl]),
                pltpu.make_async_copy(obuf.at[sl, 1, jj], bi_hbm.at[:, j], osem.at[sl]))

    @pl.when(s == 0)
    def _():
        for jj in range(jb):
            in_copy(0, 0, jj).start()

    @pl.when(s + 1 < ns)
    def _():
        for jj in range(jb):
            in_copy(s + 1, 1 - slot, jj).start()

    for jj in range(jb):
        in_copy(s, slot, jj).wait()

    @pl.when(s >= 2)
    def _():
        for jj in range(jb):
            for cp in out_copies(s - 2, slot, jj):
                cp.wait()

    c = ubuf.shape[3]
    u = ubuf[slot].reshape(jb * n1, c).astype(BF16)
    for g in range(c // dc):
        z = jnp.dot(u[:, g * dc:(g + 1) * dc], wc_ref[...], preferred_element_type=F32)
        z = z.astype(BF16)
        for jj in range(jb):
            zz_ref[jj, :n1, g * dc:(g + 1) * dc] = z[jj * n1:(jj + 1) * n1, :dc]
            zz_ref[jj, n1:, g * dc:(g + 1) * dc] = z[jj * n1:(jj + 1) * n1, dc:]
    for jj in range(jb):
        a = jnp.dot(m1_ref[...], zz_ref[jj], preferred_element_type=F32)
        ar, ai = a[:n1], a[n1:]
        tc = jnp.concatenate([tc_ref[jj]] * (c // LANES), axis=1)
        ts = jnp.concatenate([ts_ref[jj]] * (c // LANES), axis=1)
        obuf[slot, 0, jj] = ar * tc + ai * ts
        obuf[slot, 1, jj] = ai * tc - ar * ts
    for jj in range(jb):
        for cp in out_copies(s, slot, jj):
            cp.start()

    @pl.when(s == ns - 1)
    def _():
        for jj in range(jb):
            for cp in out_copies(s, slot, jj):
                cp.wait()

        @pl.when(s >= 1)
        def _():
            for jj in range(jb):
                for cp in out_copies(s - 1, 1 - slot, jj):
                    cp.wait()


def _fft2_kernel(br_ref, bi_ref, m2_ref, y_hbm, obuf, osem, *, n2, scale):
    s = pl.program_id(0)
    ns = pl.num_programs(0)
    kb = obuf.shape[1]
    slot = s % 2

    def out_copy(step, sl, kk):
        return pltpu.make_async_copy(obuf.at[sl, kk], y_hbm.at[:, step * kb + kk], osem.at[sl])

    @pl.when(s >= 2)
    def _():
        for kk in range(kb):
            out_copy(s - 2, slot, kk).wait()

    for kk in range(kb):
        rows = slice(kk * n2, (kk + 1) * n2)
        bb = jnp.concatenate([br_ref[rows, :], bi_ref[rows, :]], axis=0).astype(BF16)
        obuf[slot, kk] = jnp.dot(m2_ref[...], bb, preferred_element_type=F32) * scale
    for kk in range(kb):
        out_copy(s, slot, kk).start()

    @pl.when(s == ns - 1)
    def _():
        for kk in range(kb):
            out_copy(s, slot, kk).wait()

        @pl.when(s >= 1)
        def _():
            for kk in range(kb):
                out_copy(s - 1, 1 - slot, kk).wait()


def _fourier(u):
    t, fw = u.shape
    n2 = GRID_W
    n1 = t // n2
    dc = F_GROUP_DIM
    w_c, m1, tw_c, tw_s, m2 = _dft_tables(n1, n2, dc)
    jb = min(FFT_JB, n2)
    kb = min(FFT_KB, n1)
    any_spec = pl.BlockSpec(memory_space=pl.ANY)
    tw_spec = pl.BlockSpec((jb, n1, LANES), lambda s: (s, 0, 0))
    b_re, b_im = pl.pallas_call(
        functools.partial(_fft1_kernel, n1=n1, dc=dc),
        grid=(n2 // jb,),
        in_specs=[
            any_spec,
            pl.BlockSpec((dc, 2 * dc), lambda s: (0, 0)),
            pl.BlockSpec((2 * n1, 2 * n1), lambda s: (0, 0)),
            tw_spec, tw_spec,
        ],
        out_specs=[any_spec, any_spec],
        out_shape=[jax.ShapeDtypeStruct((n1, n2, fw), F32)] * 2,
        scratch_shapes=[
            pltpu.VMEM((2, jb, n1, fw), F32),
            pltpu.VMEM((jb, 2 * n1, fw), BF16),
            pltpu.VMEM((2, 2, jb, n1, fw), F32),
            pltpu.SemaphoreType.DMA((2,)),
            pltpu.SemaphoreType.DMA((2,)),
        ],
        compiler_params=_cparams("arbitrary"), name="fourier_stage1",
    )(u.reshape(n1, n2, fw), w_c, m1, tw_c, tw_s)
    scale = 1.0 / math.sqrt(t * dc)
    in_blk = pl.BlockSpec((kb * n2, fw), lambda s: (s, 0))
    y = pl.pallas_call(
        functools.partial(_fft2_kernel, n2=n2, scale=scale),
        grid=(n1 // kb,),
        in_specs=[in_blk, in_blk, pl.BlockSpec((n2, 2 * n2), lambda s: (0, 0))],
        out_specs=any_spec,
        out_shape=jax.ShapeDtypeStruct((n2, n1, fw), F32),
        scratch_shapes=[pltpu.VMEM((2, kb, n2, fw), F32), pltpu.SemaphoreType.DMA((2,))],
        compiler_params=_cparams("arbitrary"), name="fourier_stage2",
    )(b_re.reshape(t, fw), b_im.reshape(t, fw), m2)
    return y.reshape(t, fw)


def _dispatch_kernel(dest_ref, pad0_ref, padn_ref, na_ref, h_ref, xb_hbm, zbuf, sem, zsem, *, chunks):
    i = pl.program_id(0)
    tm = h_ref.shape[0] // chunks
    tb = zbuf.shape[0] // chunks
    nb = xb_hbm.shape[0] // (tb * chunks)

    def tail_copies(fn):
        def body(b, carry):
            fn(pltpu.make_async_copy(zbuf, xb_hbm.at[_token_rows(b, tb * chunks)], zsem))
            return carry
        lax.fori_loop(na_ref[0], nb, body, 0)

    def pad_copies(e, fn):
        pos = pad0_ref[e]
        n = padn_ref[e]
        bit = tb // 2
        while bit >= 1:
            @pl.when((n & bit) != 0)
            def _(pos=pos, bit=bit):
                rows = pl.ds(pl.multiple_of(pos * chunks, chunks), bit * chunks)
                fn(pltpu.make_async_copy(zbuf.at[pl.ds(0, bit * chunks)], xb_hbm.at[rows], zsem))
            pos = pos + (n & bit)
            bit //= 2

    @pl.when(i == 0)
    def _():
        zbuf[...] = jnp.zeros_like(zbuf)
        lax.fori_loop(0, N_EXPERTS, lambda e, c: (pad_copies(e, lambda cp: cp.start()), c)[1], 0)
        tail_copies(lambda cp: cp.start())

    def row_copy(r, k):
        dst = dest_ref[TOP_K * (i * tm + r) + k]
        return pltpu.make_async_copy(h_ref.at[_token_rows(r, chunks)], xb_hbm.at[_token_rows(dst, chunks)], sem)

    def for_rows(fn):
        def body(r, carry):
            for k in range(TOP_K):
                fn(r, k)
            return carry
        lax.fori_loop(0, tm, body, 0, unroll=ROW_DMA_UNROLL)

    for_rows(lambda r, k: row_copy(r, k).start())
    for_rows(lambda r, k: row_copy(r, k).wait())

    @pl.when(i == 0)
    def _():
        lax.fori_loop(0, N_EXPERTS, lambda e, c: (pad_copies(e, lambda cp: cp.wait()), c)[1], 0)
        tail_copies(lambda cp: cp.wait())


def _expert_kernel(be_ref, bi_ref, na_ref, x_ref, wg_ref, wu_ref, wd_ref, y_ref):
    active = pl.program_id(0) < na_ref[0]
    chunks = wg_ref.shape[0] // LANES
    tb = x_ref.shape[0] // chunks

    @pl.when(active)
    def _():
        x = _load_token_major(x_ref, tb, chunks).astype(BF16)
        gate = jnp.dot(x, wg_ref[...].astype(BF16), preferred_element_type=F32)
        up = jnp.dot(x, wu_ref[...].astype(BF16), preferred_element_type=F32)
        act = (gate * _sigmoid(gate) * up).astype(BF16)
        _store_token_major(y_ref, jnp.dot(act, wd_ref[...].astype(BF16), preferred_element_type=F32))

    @pl.when(jnp.logical_not(active))
    def _():
        y_ref[...] = jnp.zeros_like(y_ref)


def _moe_plan(experts, t, tb):
    a = t * TOP_K
    e_flat = experts.reshape(a)
    lanes = jnp.arange(N_EXPERTS, dtype=jnp.int32)[None, :]
    onehot = (e_flat[:, None] == lanes).astype(jnp.int32)
    csum = jnp.cumsum(onehot, axis=0)
    counts = csum[-1]
    pcounts = (counts + tb - 1) // tb * tb
    pends = jnp.cumsum(pcounts)
    pstarts = pends - pcounts
    dest = jnp.sum(onehot * (pstarts[None, :] + csum - 1), axis=1).astype(jnp.int32)
    nb = (a + N_EXPERTS * tb) // tb
    n_act = (pends[-1] // tb).astype(jnp.int32)
    blk = jnp.arange(nb, dtype=jnp.int32)
    blk_idx = jnp.minimum(blk, n_act - 1)
    be = jnp.sum((pends[None, :] <= (blk_idx * tb)[:, None]).astype(jnp.int32), axis=1)
    be = jnp.minimum(be, N_EXPERTS - 1).astype(jnp.int32)
    return dest, be, blk_idx, n_act.reshape(1), (pstarts + counts).astype(jnp.int32), (pcounts - counts).astype(jnp.int32)


def _experts(h, experts, layer, w_gate, w_up, w_down):
    d, de = w_gate.shape[2:]
    chunks = d // LANES
    t = h.shape[0] // chunks
    tb = EXPERT_TB
    tm = min(PRE_TM, t)
    dest, be, blk_idx, n_act, pad0, padn = _moe_plan(experts, t, tb)
    nb = be.shape[0]
    p = nb * tb
    xb = pl.pallas_call(
        functools.partial(_dispatch_kernel, chunks=chunks),
        grid_spec=pltpu.PrefetchScalarGridSpec(
            num_scalar_prefetch=4, grid=(t // tm,),
            in_specs=[pl.BlockSpec((tm * chunks, LANES), lambda i, *_: (i, 0))],
            out_specs=pl.BlockSpec(memory_space=pl.ANY),
            scratch_shapes=[pltpu.VMEM((tb * chunks, LANES), F32),
                            pltpu.SemaphoreType.DMA(()), pltpu.SemaphoreType.DMA(())]),
        out_shape=jax.ShapeDtypeStruct((p * chunks, LANES), F32),
        compiler_params=_cparams("arbitrary"), name="moe_dispatch",
    )(dest, pad0, padn, n_act, h)
    slot_blk = pl.BlockSpec((tb * chunks, LANES), lambda b, be, bi, na: (bi[b], 0))
    yb = pl.pallas_call(
        _expert_kernel,
        grid_spec=pltpu.PrefetchScalarGridSpec(
            num_scalar_prefetch=3, grid=(nb,),
            in_specs=[
                slot_blk,
                pl.BlockSpec((None, None, d, de), lambda b, be, bi, na: (layer, be[b], 0, 0)),
                pl.BlockSpec((None, None, d, de), lambda b, be, bi, na: (layer, be[b], 0, 0)),
                pl.BlockSpec((None, None, de, d), lambda b, be, bi, na: (layer, be[b], 0, 0)),
            ],
            out_specs=pl.BlockSpec((tb * chunks, LANES), lambda b, be, bi, na: (b, 0))),
        out_shape=jax.ShapeDtypeStruct((p * chunks, LANES), F32),
        compiler_params=_cparams("arbitrary"), name="moe_experts",
    )(be, blk_idx, n_act, xb, w_gate, w_up, w_down)
    return yb, dest


def kernel(x, c, norm1_g, norm2_g, ada_w, ada_b, mix_in_w, na_rpb, na_out_w, fourier_out_w, branch_gate_w,
           branch_gate_b, mix_out_w, router_group_w, router_group_b, router_expert_w, router_expert_b,
           expert_w_gate, expert_w_up, expert_w_down, final_g):
    bsz, s, d = x.shape
    assert bsz == 1 and s % GRID_W == 0
    depth = ada_w.shape[0]
    na_w = NA_HEADS * NA_HEAD_DIM
    f_w = F_GROUPS * F_GROUP_DIM
    xs = x.reshape(s, d)
    mod = _ada_all(c, ada_w, ada_b)
    pad = ROUTER_LANES - N_GROUPS - N_EXPERTS
    wr_all = jnp.concatenate([router_group_w, router_expert_w, jnp.zeros((depth, d, pad), F32)], axis=2)
    br_all = jnp.concatenate([router_group_b, router_expert_b, jnp.zeros((depth, pad), F32)], axis=1)
    moe = g2 = None
    for l in range(depth):
        sh1, sc1, g1, sh2, sc2, g2_l = [mod[l, :, i * d:(i + 1) * d] for i in range(N_MOD)]
        if moe is None:
            (h,) = _pre(xs, norm1_g[l], shift=sh1, scale=sc1)
        else:
            xs, h = _pre(xs, norm1_g[l], moe=moe, g2=g2, shift=sh1, scale=sc1)
        qkv = _mm([(h, mix_in_w, l, 0)], [], lambda acc: acc, 3 * na_w, BF16, "mix_in_qkv")
        u = _mm([(h, mix_in_w, l, 3 * na_w)], [], lambda acc: acc, f_w, F32, "mix_in_fourier")
        gates = _mm([(h, branch_gate_w, l, 0)],
                    [(branch_gate_b[l].reshape(1, -1), _rowvec, _vec_cols(0))],
                    lambda acc, bias: _sigmoid(acc + bias), branch_gate_w.shape[2], F32, "branch_gates")
        y_att = _attention(qkv, na_rpb[l])
        y_fft = _fourier(u)
        mixed = _mm([(y_att, na_out_w, l, 0), (y_fft, fourier_out_w, l, 0)],
                    [(gates, _tile, _at_cols(0)), (gates, _tile, _at_cols(d))],
                    lambda acc, ga, gf: ga * acc[0] + gf * acc[1], d, BF16, "branch_mix")
        xs = _mm([(mixed, mix_out_w, l, 0)],
                 [(xs, _tile, _at_cols(0)), (g1, _rowvec, _vec_cols(0))],
                 lambda acc, xr, g: xr + g * acc, d, F32, "mix_out")
        h2, wts, ids = _pre(xs, norm2_g[l], shift=sh2, scale=sc2, wr=wr_all[l], br=br_all[l].reshape(1, -1),
                            h_dtype=F32)
        yb, dest = _experts(h2, ids[:, :TOP_K], l, expert_w_gate, expert_w_up, expert_w_down)
        moe, g2 = (yb, dest, wts), g2_l
    _, out = _pre(xs, final_g, moe=moe, g2=g2, h_dtype=F32)
    return out.reshape(bsz, s, d)
```

```python
import functools
import math

import numpy as np
import jax
import jax.numpy as jnp
from jax import lax
from jax.experimental import pallas as pl
from jax.experimental.pallas import tpu as pltpu

F32 = jnp.float32
BF16 = jnp.bfloat16

GRID_W = 64
NA_HEADS = 16
NA_HEAD_DIM = 64
WIN_ROWS = 8
WIN_COLS = 16
F_GROUPS = 4
F_GROUP_DIM = 256
N_GROUPS = 4
EXPERTS_PER_GROUP = 8
N_EXPERTS = N_GROUPS * EXPERTS_PER_GROUP
TOP_K = 2
N_MOD = 6
EPS = 1e-6
NEG_INF = -1e30

LANES = 128
SUBLANES = 8

ADA_TN = 1536
ADA_ROWS = 256
PRE_TM = 256
MM_TM = 1024
MM_TN = 512
MM_TN_WIDE = 1024
ATTN_ROWS = 8
EXPERT_TB = 256
ROW_DMA_UNROLL = 4
FFT_JB = 4
FFT_KB = 8
ROUTER_LANES = LANES

VMEM_LIMIT = 56 * 1024 * 1024


def _cparams(*sem):
    return pltpu.CompilerParams(dimension_semantics=sem, vmem_limit_bytes=VMEM_LIMIT)


def _sigmoid(x):
    return 1.0 / (1.0 + jnp.exp(-x))


def _ada_kernel(c_ref, w_ref, b_ref, o_ref, cs_ref):
    c = c_ref[...]
    cs_ref[...] = c * _sigmoid(c)
    d, tn = w_ref.shape

    def body(i, acc):
        r0 = pl.multiple_of(i * ADA_ROWS, ADA_ROWS)
        prod = w_ref[pl.ds(r0, ADA_ROWS), :] * cs_ref[pl.ds(r0, ADA_ROWS), :]
        return acc + jnp.sum(prod.reshape(ADA_ROWS // SUBLANES, SUBLANES, tn), axis=0)

    acc = lax.fori_loop(0, d // ADA_ROWS, body, jnp.zeros((SUBLANES, tn), F32))
    o_ref[...] = jnp.sum(acc, axis=0, keepdims=True) + b_ref[...]


def _ada_all(c, ada_w, ada_b):
    nl, d, n = ada_w.shape
    tn = min(ADA_TN, n)
    return pl.pallas_call(
        _ada_kernel,
        grid=(nl, n // tn),
        in_specs=[
            pl.BlockSpec((d, 1), lambda l, j: (0, 0)),
            pl.BlockSpec((None, d, tn), lambda l, j: (l, 0, j)),
            pl.BlockSpec((None, 1, tn), lambda l, j: (l, 0, j)),
        ],
        out_specs=pl.BlockSpec((None, 1, tn), lambda l, j: (l, 0, j)),
        out_shape=jax.ShapeDtypeStruct((nl, 1, n), F32),
        scratch_shapes=[pltpu.VMEM((d, 1), F32)],
        compiler_params=_cparams("parallel", "parallel"),
        name="ada_proj",
    )(c.reshape(d, 1), ada_w, ada_b.reshape(nl, 1, n))


def _rms(x, g):
    return x * lax.rsqrt(jnp.mean(x * x, axis=-1, keepdims=True) + EPS) * g


def _router(h, wr_ref, br_ref):
    w = wr_ref[...]
    h_hi = h.astype(BF16)
    h_lo = (h - h_hi.astype(F32)).astype(BF16)
    w_hi = w.astype(BF16)
    w_lo = (w - w_hi.astype(F32)).astype(BF16)
    dot = functools.partial(jnp.dot, preferred_element_type=F32)
    logits = dot(h_hi, w_hi) + (dot(h_hi, w_lo) + dot(h_lo, w_hi)) + br_ref[...]
    lane = lax.broadcasted_iota(jnp.int32, logits.shape, 1)
    big = jnp.int32(ROUTER_LANES)
    is_g = lane < N_GROUPS
    gl = jnp.where(is_g, logits, NEG_INF)
    gmax = jnp.max(gl, axis=-1, keepdims=True)
    gsum = jnp.sum(jnp.where(is_g, jnp.exp(gl - gmax), 0.0), axis=-1, keepdims=True)
    g_top_p = 1.0 / gsum
    g_top = jnp.min(jnp.where(is_g & (gl == gmax), lane, big), axis=-1, keepdims=True)
    e_lane = lane - N_GROUPS
    sel = (e_lane >= 0) & (e_lane < N_EXPERTS) & ((e_lane >> 3) == g_top)
    l1 = jnp.where(sel, logits, NEG_INF)
    m1 = jnp.max(l1, axis=-1, keepdims=True)
    i1 = jnp.min(jnp.where(sel & (l1 == m1), lane, big), axis=-1, keepdims=True)
    sel2 = sel & (lane != i1)
    l2 = jnp.where(sel2, logits, NEG_INF)
    m2 = jnp.max(l2, axis=-1, keepdims=True)
    i2 = jnp.min(jnp.where(sel2 & (l2 == m2), lane, big), axis=-1, keepdims=True)
    e21 = jnp.exp(m2 - m1)
    p1 = 1.0 / (1.0 + e21)
    p2 = e21 / (1.0 + e21)
    wts = jnp.where(lane == 0, g_top_p * p1, jnp.where(lane == 1, g_top_p * p2, 0.0))
    ids = jnp.where(lane == 0, i1 - N_GROUPS, jnp.where(lane == 1, i2 - N_GROUPS, 0))
    return wts, ids


def _load_token_major(ref, n_tok, chunks):
    return jnp.concatenate([ref[pl.ds(c, n_tok, stride=chunks), :] for c in range(chunks)], axis=1)


def _store_token_major(ref, val):
    n_tok, d = val.shape
    chunks = d // LANES
    for c in range(chunks):
        ref[pl.ds(c, n_tok, stride=chunks), :] = val[:, c * LANES:(c + 1) * LANES]


def _token_rows(tok, chunks):
    return pl.ds(pl.multiple_of(tok * chunks, chunks), chunks)


def _pre_kernel(*refs, combine, modulate, router, emit_h, h_dtype):
    it = iter(refs)
    if combine:
        dest_ref = next(it)
    x_ref = next(it)
    if combine:
        y_hbm, cw_ref, g2_ref = next(it), next(it), next(it)
    g_ref = next(it)
    if modulate:
        sh_ref, sc_ref = next(it), next(it)
    if router:
        wr_ref, br_ref = next(it), next(it)
    outs = [next(it) for _ in range(int(combine) + int(emit_h) + 2 * int(router))]
    x = x_ref[...]
    if combine:
        gbuf, gsem = next(it), next(it)
        i = pl.program_id(0)
        slot = i % 2
        tm, d = x_ref.shape
        chunks = d // LANES

        def row_copy(step, sl, r, k):
            src = dest_ref[TOP_K * (step * tm + r) + k]
            return pltpu.make_async_copy(y_hbm.at[_token_rows(src, chunks)],
                                         gbuf.at[sl, k, _token_rows(r, chunks)], gsem.at[sl])

        def for_rows(fn):
            def body(r, carry):
                for k in range(TOP_K):
                    fn(r, k)
                return carry
            lax.fori_loop(0, tm, body, 0, unroll=ROW_DMA_UNROLL)

        @pl.when(i == 0)
        def _():
            for_rows(lambda r, k: row_copy(0, 0, r, k).start())

        @pl.when(i + 1 < pl.num_programs(0))
        def _():
            for_rows(lambda r, k: row_copy(i + 1, 1 - slot, r, k).start())

        for_rows(lambda r, k: row_copy(i, slot, r, k).wait())
        cw = cw_ref[...]
        y0 = _load_token_major(gbuf.at[slot, 0], tm, chunks)
        y1 = _load_token_major(gbuf.at[slot, 1], tm, chunks)
        x = x + g2_ref[...] * (cw[:, 0:1] * y0 + cw[:, 1:2] * y1)
        outs.pop(0)[...] = x
    h = _rms(x, g_ref[...])
    if modulate:
        h = h * (1.0 + sc_ref[...]) + sh_ref[...]
    if emit_h and router:
        _store_token_major(outs.pop(0), _pack_bf16_pairs(h))
    elif emit_h:
        outs.pop(0)[...] = h.astype(h_dtype)
    if router:
        wts, ids = _router(h, wr_ref, br_ref)
        outs.pop(0)[...] = wts
        outs.pop(0)[...] = ids


def _pre(x, g, *, moe=None, g2=None, shift=None, scale=None, wr=None, br=None, h_dtype=BF16):
    t, d = x.shape
    tm = min(PRE_TM, t)
    chunks = d // LANES
    combine, modulate, router = moe is not None, shift is not None, wr is not None
    row = pl.BlockSpec((tm, d), lambda i, *_: (i, 0))
    vec = pl.BlockSpec((1, d), lambda i, *_: (0, 0))
    lane_row = pl.BlockSpec((tm, ROUTER_LANES), lambda i, *_: (i, 0))
    prefetch, args, specs, scratch = [], [x], [row], []
    if combine:
        yb, dest, cw = moe
        prefetch.append(dest)
        args += [yb, cw, g2]
        specs += [pl.BlockSpec(memory_space=pl.ANY), lane_row, vec]
        scratch = [pltpu.VMEM((2, TOP_K, tm * chunks, LANES), F32), pltpu.SemaphoreType.DMA((2,))]
    args.append(g.reshape(1, d))
    specs.append(vec)
    if modulate:
        args += [shift, scale]
        specs += [vec, vec]
    if router:
        args += [wr, br]
        specs += [pl.BlockSpec(wr.shape, lambda i, *_: (0, 0)), pl.BlockSpec(br.shape, lambda i, *_: (0, 0))]
    out_shape, out_specs = [], []
    if combine:
        out_shape.append(jax.ShapeDtypeStruct((t, d), F32))
        out_specs.append(row)
    if router:
        out_shape.append(jax.ShapeDtypeStruct((t * chunks // 2, LANES), jnp.uint32))
        out_specs.append(pl.BlockSpec((tm * chunks // 2, LANES), lambda i, *_: (i, 0)))
    else:
        out_shape.append(jax.ShapeDtypeStruct((t, d), h_dtype))
        out_specs.append(row)
    if router:
        out_shape += [jax.ShapeDtypeStruct((t, ROUTER_LANES), F32), jax.ShapeDtypeStruct((t, ROUTER_LANES), jnp.int32)]
        out_specs += [lane_row, lane_row]
    kern = functools.partial(_pre_kernel, combine=combine, modulate=modulate, router=router,
                             emit_h=True, h_dtype=h_dtype)
    grid_spec = pltpu.PrefetchScalarGridSpec(
        num_scalar_prefetch=len(prefetch), grid=(t // tm,), in_specs=specs, out_specs=out_specs,
        scratch_shapes=scratch)
    return pl.pallas_call(
        kern, grid_spec=grid_spec, out_shape=out_shape,
        compiler_params=_cparams("arbitrary" if combine else "parallel"), name="pre_norm",
    )(*prefetch, *args)


def _mm_kernel(*refs, n_pairs, n_extras, epilogue):
    o_ref = refs[2 * n_pairs + n_extras]
    caches = refs[2 * n_pairs + n_extras + 1:]

    @pl.when(pl.program_id(1) == 0)
    def _():
        for p in range(n_pairs):
            caches[p][...] = refs[2 * p + 1][...].astype(BF16)

    acc = None
    for p in range(n_pairs):
        part = jnp.dot(refs[2 * p][...].astype(BF16), caches[p][...], preferred_element_type=F32)
        acc = part if acc is None else (acc, part)
    extras = [r[...] for r in refs[2 * n_pairs:2 * n_pairs + n_extras]]
    o_ref[...] = epilogue(acc, *extras).astype(o_ref.dtype)


def _mm(pairs, extras, epilogue, n, out_dtype, name, tn=MM_TN):
    m = pairs[0][0].shape[0]
    tm, tn = min(MM_TM, m), min(tn, n)
    args, specs, scratch = [], [], []
    for a, w, layer, off in pairs:
        k = a.shape[1]
        args += [a, w]
        specs += [pl.BlockSpec((tm, k), lambda j, i: (i, 0)),
                  pl.BlockSpec((None, k, tn), lambda j, i, layer=layer, off=off: (layer, 0, j + off // tn))]
        scratch.append(pltpu.VMEM((k, tn), BF16))
    for arr, bshape, imap in extras:
        args.append(arr)
        specs.append(pl.BlockSpec(bshape(tm, tn), imap(tn)))
    return pl.pallas_call(
        functools.partial(_mm_kernel, n_pairs=len(pairs), n_extras=len(extras), epilogue=epilogue),
        grid=(n // tn, m // tm), in_specs=specs,
        out_specs=pl.BlockSpec((tm, tn), lambda j, i: (i, j)),
        out_shape=jax.ShapeDtypeStruct((m, n), out_dtype),
        scratch_shapes=scratch,
        compiler_params=_cparams("arbitrary", "arbitrary"), name=name,
    )(*args)


def _tile(tm, tn):
    return (tm, tn)


def _rowvec(tm, tn):
    return (1, tn)


def _at_cols(off):
    return lambda tn: (lambda j, i: (i, j + off // tn))


def _vec_cols(off):
    return lambda tn: (lambda j, i: (0, j + off // tn))


def _attn_bias_fill(rpb_ref, b_ref, head0):
    n_dr, n_dc = 2 * WIN_ROWS - 1, 2 * WIN_COLS - 1
    qc = lax.broadcasted_iota(jnp.int32, (GRID_W, 2 * GRID_W), 0)
    kc = lax.broadcasted_iota(jnp.int32, (GRID_W, 2 * GRID_W), 1) & (GRID_W - 1)
    cs = jnp.clip(qc - WIN_COLS // 2, 0, GRID_W - WIN_COLS)
    inside = (kc >= cs) & (kc < cs + WIN_COLS)
    dc = jnp.clip(kc - qc, -(WIN_COLS - 1), WIN_COLS - 1) + (WIN_COLS - 1)
    for h in range(2):
        base = (head0 + h) * (n_dr * n_dc)
        for dr in range(n_dr):
            t = jnp.zeros((GRID_W, 2 * GRID_W), F32)
            for d in range(n_dc):
                t = jnp.where(dc == d, rpb_ref[base + dr * n_dc + d], t)
            t = jnp.where(inside, t, NEG_INF)
            for var in range(WIN_ROWS):
                j = dr - var
                if 0 <= j < WIN_ROWS:
                    c0 = j * GRID_W
                    lo = c0 % (2 * GRID_W)
                    b_ref[var, h * GRID_W:(h + 1) * GRID_W, c0:c0 + GRID_W] = t[:, lo:lo + GRID_W]


def _attn_kernel(rpb_ref, q_ref, k_ref, v_ref, o_ref, b_ref, s_ref, *, n_rows):
    rb = pl.program_id(1)
    n_q = q_ref.shape[0]
    wkeys = WIN_ROWS * GRID_W
    dh = NA_HEAD_DIM
    lane = lax.broadcasted_iota(jnp.int32, (GRID_W, 2 * dh), 1)
    first = lane < dh

    @pl.when(rb == 0)
    def _():
        _attn_bias_fill(rpb_ref, b_ref, 2 * pl.program_id(0))

    def window(i):
        r = rb * n_q + i
        rs = jnp.clip(r - WIN_ROWS // 2, 0, n_rows - WIN_ROWS)
        return rs, rs - r + (WIN_ROWS - 1)

    for i in range(n_q):
        rs, var = window(i)
        kw = k_ref[pl.ds(rs, WIN_ROWS)].reshape(wkeys, 2 * dh)
        q = q_ref[i] * jnp.asarray(dh ** -0.5, BF16)
        zero = jnp.zeros_like(q)
        q2 = jnp.concatenate([jnp.where(first, q, zero), jnp.where(first, zero, q)], axis=0)
        s = lax.dot_general(q2, kw, (((1,), (1,)), ((), ())), preferred_element_type=F32)
        b = b_ref[var]
        s_ref[i] = jnp.where(b > 0.5 * NEG_INF, s + b, NEG_INF)
    for i in range(n_q):
        rs, _ = window(i)
        vw = v_ref[pl.ds(rs, WIN_ROWS)].reshape(wkeys, 2 * dh)
        s = s_ref[i]
        m = jnp.max(s, axis=-1, keepdims=True)
        p = jnp.exp(s - m)
        l = jnp.sum(p, axis=-1, keepdims=True)
        o = jnp.dot(p.astype(BF16), vw, preferred_element_type=F32) / l
        o_ref[i] = jnp.where(first, o[:GRID_W], o[GRID_W:]).astype(o_ref.dtype)


def _attention(proj, rpb):
    t, width = proj.shape
    n_rows = t // GRID_W
    pairs = NA_HEADS // 2
    p3 = proj.reshape(n_rows, GRID_W, width)
    blk = 2 * NA_HEAD_DIM
    rows = min(ATTN_ROWS, n_rows)
    wkeys = WIN_ROWS * GRID_W
    out = pl.pallas_call(
        functools.partial(_attn_kernel, n_rows=n_rows),
        grid_spec=pltpu.PrefetchScalarGridSpec(
            num_scalar_prefetch=1, grid=(pairs, n_rows // rows),
            in_specs=[
                pl.BlockSpec((rows, GRID_W, blk), lambda hp, rb, _: (rb, 0, hp)),
                pl.BlockSpec((n_rows, GRID_W, blk), lambda hp, rb, _: (0, 0, pairs + hp)),
                pl.BlockSpec((n_rows, GRID_W, blk), lambda hp, rb, _: (0, 0, 2 * pairs + hp)),
            ],
            out_specs=pl.BlockSpec((rows, GRID_W, blk), lambda hp, rb, _: (rb, 0, hp)),
            scratch_shapes=[pltpu.VMEM((WIN_ROWS, 2 * GRID_W, wkeys), F32),
                            pltpu.VMEM((rows, 2 * GRID_W, wkeys), F32)]),
        out_shape=jax.ShapeDtypeStruct((n_rows, GRID_W, NA_HEADS * NA_HEAD_DIM), BF16),
        compiler_params=_cparams("arbitrary", "arbitrary"), name="na_attention",
    )(rpb.reshape(-1), p3, p3, p3)
    return out.reshape(t, NA_HEADS * NA_HEAD_DIM)


def _dft_tables(n1, n2, dc):
    s = n1 * n2
    a = np.arange(dc, dtype=np.float64)
    ang_c = 2 * np.pi * np.outer(a, a) / dc
    w_c = np.concatenate([np.cos(ang_c), -np.sin(ang_c)], axis=1)
    a1 = np.arange(n1, dtype=np.float64)
    ang1 = 2 * np.pi * np.outer(a1, a1) / n1
    c1, s1 = np.cos(ang1), np.sin(ang1)
    m1 = np.block([[c1, s1], [-s1, c1]])
    a2 = np.arange(n2, dtype=np.float64)
    ang_t = 2 * np.pi * np.outer(a2, a1) / s
    tw_c = np.repeat(np.cos(ang_t)[:, :, None], LANES, axis=2)
    tw_s = np.repeat(np.sin(ang_t)[:, :, None], LANES, axis=2)
    ang2 = 2 * np.pi * np.outer(a2, a2) / n2
    m2 = np.concatenate([np.cos(ang2), np.sin(ang2)], axis=1)
    return (jnp.asarray(w_c, BF16), jnp.asarray(m1, BF16), jnp.asarray(tw_c, F32), jnp.asarray(tw_s, F32),
            jnp.asarray(m2, BF16))


def _fft1_kernel(u_hbm, wc_ref, m1_ref, tc_ref, ts_ref, br_hbm, bi_hbm,
                 ubuf, zz_ref, obuf, isem, osem, *, n1, dc):
    s = pl.program_id(0)
    ns = pl.num_programs(0)
    jb = ubuf.shape[1]
    slot = s % 2

    def in_copy(step, sl, jj):
        return pltpu.make_async_copy(u_hbm.at[:, step * jb + jj], ubuf.at[sl, jj], isem.at[sl])

    def out_copies(step, sl, jj):
        j = step * jb + jj
        return (pltpu.make_async_copy(obuf.at[sl, 0, jj], br_hbm.at[:, j], osem.at[sl]),
                pltpu.make_async_copy(obuf.at[sl, 1, jj], bi_hbm.at[:, j], osem.at[sl]))

    @pl.when(s == 0)
    def _():
        for jj in range(jb):
            in_copy(0, 0, jj).start()

    @pl.when(s + 1 < ns)
    def _():
        for jj in range(jb):
            in_copy(s + 1, 1 - slot, jj).start()

    for jj in range(jb):
        in_copy(s, slot, jj).wait()

    @pl.when(s >= 2)
    def _():
        for jj in range(jb):
            for cp in out_copies(s - 2, slot, jj):
                cp.wait()

    c = ubuf.shape[3]
    u = ubuf[slot].reshape(jb * n1, c).astype(BF16)
    for g in range(c // dc):
        z = jnp.dot(u[:, g * dc:(g + 1) * dc], wc_ref[...], preferred_element_type=F32)
        z = z.astype(BF16)
        for jj in range(jb):
            zz_ref[jj, :n1, g * dc:(g + 1) * dc] = z[jj * n1:(jj + 1) * n1, :dc]
            zz_ref[jj, n1:, g * dc:(g + 1) * dc] = z[jj * n1:(jj + 1) * n1, dc:]
    for jj in range(jb):
        a = jnp.dot(m1_ref[...], zz_ref[jj], preferred_element_type=F32)
        ar, ai = a[:n1], a[n1:]
        tc = jnp.concatenate([tc_ref[jj]] * (c // LANES), axis=1)
        ts = jnp.concatenate([ts_ref[jj]] * (c // LANES), axis=1)
        obuf[slot, 0, jj] = ar * tc + ai * ts
        obuf[slot, 1, jj] = ai * tc - ar * ts
    for jj in range(jb):
        for cp in out_copies(s, slot, jj):
            cp.start()

    @pl.when(s == ns - 1)
    def _():
        for jj in range(jb):
            for cp in out_copies(s, slot, jj):
                cp.wait()

        @pl.when(s >= 1)
        def _():
            for jj in range(jb):
                for cp in out_copies(s - 1, 1 - slot, jj):
                    cp.wait()


def _fft2_kernel(br_ref, bi_ref, m2_ref, y_hbm, obuf, osem, *, n2, scale):
    s = pl.program_id(0)
    ns = pl.num_programs(0)
    kb = obuf.shape[1]
    slot = s % 2

    def out_copy(step, sl, kk):
        return pltpu.make_async_copy(obuf.at[sl, kk], y_hbm.at[:, step * kb + kk], osem.at[sl])

    @pl.when(s >= 2)
    def _():
        for kk in range(kb):
            out_copy(s - 2, slot, kk).wait()

    for kk in range(kb):
        rows = slice(kk * n2, (kk + 1) * n2)
        bb = jnp.concatenate([br_ref[rows, :], bi_ref[rows, :]], axis=0).astype(BF16)
        obuf[slot, kk] = jnp.dot(m2_ref[...], bb, preferred_element_type=F32) * scale
    for kk in range(kb):
        out_copy(s, slot, kk).start()

    @pl.when(s == ns - 1)
    def _():
        for kk in range(kb):
            out_copy(s, slot, kk).wait()

        @pl.when(s >= 1)
        def _():
            for kk in range(kb):
                out_copy(s - 1, 1 - slot, kk).wait()


def _fourier(u):
    t, fw = u.shape
    n2 = GRID_W
    n1 = t // n2
    dc = F_GROUP_DIM
    w_c, m1, tw_c, tw_s, m2 = _dft_tables(n1, n2, dc)
    jb = min(FFT_JB, n2)
    kb = min(FFT_KB, n1)
    any_spec = pl.BlockSpec(memory_space=pl.ANY)
    tw_spec = pl.BlockSpec((jb, n1, LANES), lambda s: (s, 0, 0))
    b_re, b_im = pl.pallas_call(
        functools.partial(_fft1_kernel, n1=n1, dc=dc),
        grid=(n2 // jb,),
        in_specs=[
            any_spec,
            pl.BlockSpec((dc, 2 * dc), lambda s: (0, 0)),
            pl.BlockSpec((2 * n1, 2 * n1), lambda s: (0, 0)),
            tw_spec, tw_spec,
        ],
        out_specs=[any_spec, any_spec],
        out_shape=[jax.ShapeDtypeStruct((n1, n2, fw), F32)] * 2,
        scratch_shapes=[
            pltpu.VMEM((2, jb, n1, fw), F32),
            pltpu.VMEM((jb, 2 * n1, fw), BF16),
            pltpu.VMEM((2, 2, jb, n1, fw), F32),
            pltpu.SemaphoreType.DMA((2,)),
            pltpu.SemaphoreType.DMA((2,)),
        ],
        compiler_params=_cparams("arbitrary"), name="fourier_stage1",
    )(u.reshape(n1, n2, fw), w_c, m1, tw_c, tw_s)
    scale = 1.0 / math.sqrt(t * dc)
    in_blk = pl.BlockSpec((kb * n2, fw), lambda s: (s, 0))
    y = pl.pallas_call(
        functools.partial(_fft2_kernel, n2=n2, scale=scale),
        grid=(n1 // kb,),
        in_specs=[in_blk, in_blk, pl.BlockSpec((n2, 2 * n2), lambda s: (0, 0))],
        out_specs=any_spec,
        out_shape=jax.ShapeDtypeStruct((n2, n1, fw), F32),
        scratch_shapes=[pltpu.VMEM((2, kb, n2, fw), F32), pltpu.SemaphoreType.DMA((2,))],
        compiler_params=_cparams("arbitrary"), name="fourier_stage2",
    )(b_re.reshape(t, fw), b_im.reshape(t, fw), m2)
    return y.reshape(t, fw)


def _dispatch_kernel(dest_ref, pad0_ref, padn_ref, na_ref, h_ref, xb_hbm, zbuf, sem, zsem, *, chunks):
    i = pl.program_id(0)
    tm = h_ref.shape[0] // chunks
    tb = zbuf.shape[0] // chunks
    nb = xb_hbm.shape[0] // (tb * chunks)

    def tail_copies(fn):
        def body(b, carry):
            fn(pltpu.make_async_copy(zbuf, xb_hbm.at[_token_rows(b, tb * chunks)], zsem))
            return carry
        lax.fori_loop(na_ref[0], nb, body, 0)

    def pad_copies(e, fn):
        pos = pad0_ref[e]
        n = padn_ref[e]
        bit = tb // 2
        while bit >= 1:
            @pl.when((n & bit) != 0)
            def _(pos=pos, bit=bit):
                rows = pl.ds(pl.multiple_of(pos * chunks, chunks), bit * chunks)
                fn(pltpu.make_async_copy(zbuf.at[pl.ds(0, bit * chunks)], xb_hbm.at[rows], zsem))
            pos = pos + (n & bit)
            bit //= 2

    @pl.when(i == 0)
    def _():
        zbuf[...] = jnp.zeros_like(zbuf)
        lax.fori_loop(0, N_EXPERTS, lambda e, c: (pad_copies(e, lambda cp: cp.start()), c)[1], 0)
        tail_copies(lambda cp: cp.start())

    def row_copy(r, k):
        dst = dest_ref[TOP_K * (i * tm + r) + k]
        return pltpu.make_async_copy(h_ref.at[_token_rows(r, chunks)], xb_hbm.at[_token_rows(dst, chunks)], sem)

    def for_rows(fn):
        def body(r, carry):
            for k in range(TOP_K):
                fn(r, k)
            return carry
        lax.fori_loop(0, tm, body, 0, unroll=ROW_DMA_UNROLL)

    for_rows(lambda r, k: row_copy(r, k).start())
    for_rows(lambda r, k: row_copy(r, k).wait())

    @pl.when(i == 0)
    def _():
        lax.fori_loop(0, N_EXPERTS, lambda e, c: (pad_copies(e, lambda cp: cp.wait()), c)[1], 0)
        tail_copies(lambda cp: cp.wait())


def _pack_bf16_pairs(h):
    half = h.shape[1] // 2
    bits = lax.bitcast_convert_type(h.astype(BF16).astype(F32), jnp.uint32)
    return bits[:, half:] | (bits[:, :half] >> 16)


def _unpack_bf16_pairs(p):
    lo = lax.bitcast_convert_type(p << 16, F32)
    hi = lax.bitcast_convert_type(p & jnp.uint32(0xFFFF0000), F32)
    return jnp.concatenate([lo, hi], axis=1).astype(BF16)


def _expert_kernel(be_ref, bi_ref, first_ref, nxt_ref, na_ref, x_ref, wg_hbm, wu_hbm, wd_hbm, y_ref,
                   wg_st, wu_st, wd_st, wg_c, wu_c, wd_c, wsem, *, layer):
    b = pl.program_id(0)
    active = b < na_ref[0]
    d = wg_st.shape[0]
    tb = y_ref.shape[0] // (d // LANES)

    def weight_copies(e):
        return [pltpu.make_async_copy(src.at[layer, e], dst, wsem.at[n])
                for n, (src, dst) in enumerate(((wg_hbm, wg_st), (wu_hbm, wu_st), (wd_hbm, wd_st)))]

    @pl.when(b == 0)
    def _():
        for cp in weight_copies(be_ref[0]):
            cp.start()

    @pl.when(active & (first_ref[b] == 1))
    def _():
        for cp in weight_copies(be_ref[b]):
            cp.wait()
        wg_c[...] = wg_st[...].astype(BF16)
        wu_c[...] = wu_st[...].astype(BF16)
        wd_c[...] = wd_st[...].astype(BF16)

        @pl.when(nxt_ref[b] >= 0)
        def _():
            for cp in weight_copies(nxt_ref[b]):
                cp.start()

    @pl.when(active)
    def _():
        x = _unpack_bf16_pairs(_load_token_major(x_ref, tb, x_ref.shape[0] // tb))
        gate = jnp.dot(x, wg_c[...], preferred_element_type=F32)
        up = jnp.dot(x, wu_c[...], preferred_element_type=F32)
        act = (gate * _sigmoid(gate) * up).astype(BF16)
        _store_token_major(y_ref, jnp.dot(act, wd_c[...], preferred_element_type=F32))

    @pl.when(jnp.logical_not(active))
    def _():
        y_ref[...] = jnp.zeros_like(y_ref)


def _moe_plan(experts, t, tb):
    a = t * TOP_K
    e_flat = experts.reshape(a)
    lanes = jnp.arange(N_EXPERTS, dtype=jnp.int32)[None, :]
    onehot = (e_flat[:, None] == lanes).astype(jnp.int32)
    csum = jnp.cumsum(onehot, axis=0)
    counts = csum[-1]
    pcounts = (counts + tb - 1) // tb * tb
    pends = jnp.cumsum(pcounts)
    pstarts = pends - pcounts
    dest = jnp.sum(onehot * (pstarts[None, :] + csum - 1), axis=1).astype(jnp.int32)
    nb = (a + N_EXPERTS * tb) // tb
    n_act = (pends[-1] // tb).astype(jnp.int32)
    blk = jnp.arange(nb, dtype=jnp.int32)
    blk_idx = jnp.minimum(blk, n_act - 1)
    be = jnp.sum((pends[None, :] <= (blk_idx * tb)[:, None]).astype(jnp.int32), axis=1)
    be = jnp.minimum(be, N_EXPERTS - 1).astype(jnp.int32)
    first = ((blk * tb == pstarts[be]) & (blk < n_act)).astype(jnp.int32)
    after = pends[be] // tb
    nxt = jnp.where(after < n_act, be[jnp.minimum(after, nb - 1)], -1).astype(jnp.int32)
    plan = dict(dest=dest, be=be, blk_idx=blk_idx, first=first, nxt=nxt, n_act=n_act.reshape(1),
                pad0=(pstarts + counts).astype(jnp.int32), padn=(pcounts - counts).astype(jnp.int32))
    return plan


def _experts(h, experts, layer, w_gate, w_up, w_down):
    d, de = w_gate.shape[2:]
    chunks = d // LANES
    xchunks = chunks // 2
    t = h.shape[0] // xchunks
    tb = EXPERT_TB
    tm = min(PRE_TM, t)
    plan = _moe_plan(experts, t, tb)
    nb = plan["be"].shape[0]
    p = nb * tb
    xb = pl.pallas_call(
        functools.partial(_dispatch_kernel, chunks=xchunks),
        grid_spec=pltpu.PrefetchScalarGridSpec(
            num_scalar_prefetch=4, grid=(t // tm,),
            in_specs=[pl.BlockSpec((tm * xchunks, LANES), lambda i, *_: (i, 0))],
            out_specs=pl.BlockSpec(memory_space=pl.ANY),
            scratch_shapes=[pltpu.VMEM((tb * xchunks, LANES), h.dtype),
                            pltpu.SemaphoreType.DMA(()), pltpu.SemaphoreType.DMA(())]),
        out_shape=jax.ShapeDtypeStruct((p * xchunks, LANES), h.dtype),
        compiler_params=_cparams("arbitrary"), name="moe_dispatch",
    )(plan["dest"], plan["pad0"], plan["padn"], plan["n_act"], h)
    hbm = pl.BlockSpec(memory_space=pl.ANY)
    yb = pl.pallas_call(
        functools.partial(_expert_kernel, layer=layer),
        grid_spec=pltpu.PrefetchScalarGridSpec(
            num_scalar_prefetch=5, grid=(nb,),
            in_specs=[pl.BlockSpec((tb * xchunks, LANES), lambda b, be, bi, *_: (bi[b], 0)), hbm, hbm, hbm],
            out_specs=pl.BlockSpec((tb * chunks, LANES), lambda b, *_: (b, 0)),
            scratch_shapes=[pltpu.VMEM((d, de), F32), pltpu.VMEM((d, de), F32), pltpu.VMEM((de, d), F32),
                            pltpu.VMEM((d, de), BF16), pltpu.VMEM((d, de), BF16), pltpu.VMEM((de, d), BF16),
                            pltpu.SemaphoreType.DMA((3,))]),
        out_shape=jax.ShapeDtypeStruct((p * chunks, LANES), F32),
        compiler_params=_cparams("arbitrary"), name="moe_experts",
    )(plan["be"], plan["blk_idx"], plan["first"], plan["nxt"], plan["n_act"], xb, w_gate, w_up, w_down)
    return yb, plan["dest"]


def kernel(x, c, norm1_g, norm2_g, ada_w, ada_b, mix_in_w, na_rpb, na_out_w, fourier_out_w, branch_gate_w,
           branch_gate_b, mix_out_w, router_group_w, router_group_b, router_expert_w, router_expert_b,
           expert_w_gate, expert_w_up, expert_w_down, final_g):
    bsz, s, d = x.shape
    assert bsz == 1 and s % GRID_W == 0
    depth = ada_w.shape[0]
    na_w = NA_HEADS * NA_HEAD_DIM
    f_w = F_GROUPS * F_GROUP_DIM
    xs = x.reshape(s, d)
    mod = _ada_all(c, ada_w, ada_b)
    pad = ROUTER_LANES - N_GROUPS - N_EXPERTS
    wr_all = jnp.concatenate([router_group_w, router_expert_w, jnp.zeros((depth, d, pad), F32)], axis=2)
    br_all = jnp.concatenate([router_group_b, router_expert_b, jnp.zeros((depth, pad), F32)], axis=1)
    moe = g2 = None
    for l in range(depth):
        sh1, sc1, g1, sh2, sc2, g2_l = [mod[l, :, i * d:(i + 1) * d] for i in range(N_MOD)]
        if moe is None:
            (h,) = _pre(xs, norm1_g[l], shift=sh1, scale=sc1)
        else:
            xs, h = _pre(xs, norm1_g[l], moe=moe, g2=g2, shift=sh1, scale=sc1)
        qkv = _mm([(h, mix_in_w, l, 0)], [], lambda acc: acc, 3 * na_w, BF16, "mix_in_qkv", tn=MM_TN_WIDE)
        u = _mm([(h, mix_in_w, l, 3 * na_w)], [], lambda acc: acc, f_w, F32, "mix_in_fourier", tn=MM_TN_WIDE)
        gates = _mm([(h, branch_gate_w, l, 0)],
                    [(branch_gate_b[l].reshape(1, -1), _rowvec, _vec_cols(0))],
                    lambda acc, bias: _sigmoid(acc + bias), branch_gate_w.shape[2], F32, "branch_gates",
                    tn=MM_TN_WIDE)
        y_att = _attention(qkv, na_rpb[l])
        y_fft = _fourier(u)
        mixed = _mm([(y_att, na_out_w, l, 0), (y_fft, fourier_out_w, l, 0)],
                    [(gates, _tile, _at_cols(0)), (gates, _tile, _at_cols(d))],
                    lambda acc, ga, gf: ga * acc[0] + gf * acc[1], d, BF16, "branch_mix")
        xs = _mm([(mixed, mix_out_w, l, 0)],
                 [(xs, _tile, _at_cols(0)), (g1, _rowvec, _vec_cols(0))],
                 lambda acc, xr, g: xr + g * acc, d, F32, "mix_out")
        h2, wts, ids = _pre(xs, norm2_g[l], shift=sh2, scale=sc2, wr=wr_all[l], br=br_all[l].reshape(1, -1),
                            h_dtype=F32)
        yb, dest = _experts(h2, ids[:, :TOP_K], l, expert_w_gate, expert_w_up, expert_w_down)
        moe, g2 = (yb, dest, wts), g2_l
    _, out = _pre(xs, final_g, moe=moe, g2=g2, h_dtype=F32)
    return out.reshape(bsz, s, d)
```

```python
import functools
import math

import numpy as np
import jax
import jax.numpy as jnp
from jax import lax
from jax.experimental import pallas as pl
from jax.experimental.pallas import tpu as pltpu

F32 = jnp.float32
BF16 = jnp.bfloat16

GRID_W = 64
NA_HEADS = 16
NA_HEAD_DIM = 64
WIN_ROWS = 8
WIN_COLS = 16
F_GROUPS = 4
F_GROUP_DIM = 256
N_GROUPS = 4
EXPERTS_PER_GROUP = 8
N_EXPERTS = N_GROUPS * EXPERTS_PER_GROUP
TOP_K = 2
N_MOD = 6
EPS = 1e-6
NEG_INF = -1e30

LANES = 128
SUBLANES = 8

ADA_TN = 1536
ADA_ROWS = 256
PRE_TM = 256
MM_TM = 1024
MM_TN = 512
MM_TN_WIDE = 1024
ATTN_ROWS = 8
EXPERT_TB = 256
ROW_DMA_UNROLL = 4
FFT_JB = 4
FFT_KB = 8
ROUTER_LANES = LANES

VMEM_LIMIT = 56 * 1024 * 1024


def _cparams(*sem):
    return pltpu.CompilerParams(dimension_semantics=sem, vmem_limit_bytes=VMEM_LIMIT)


def _sigmoid(x):
    return 1.0 / (1.0 + jnp.exp(-x))


def _ada_kernel(c_ref, w_ref, b_ref, o_ref, cs_ref):
    c = c_ref[...]
    cs_ref[...] = c * _sigmoid(c)
    d, tn = w_ref.shape

    def body(i, acc):
        r0 = pl.multiple_of(i * ADA_ROWS, ADA_ROWS)
        prod = w_ref[pl.ds(r0, ADA_ROWS), :] * cs_ref[pl.ds(r0, ADA_ROWS), :]
        return acc + jnp.sum(prod.reshape(ADA_ROWS // SUBLANES, SUBLANES, tn), axis=0)

    acc = lax.fori_loop(0, d // ADA_ROWS, body, jnp.zeros((SUBLANES, tn), F32))
    o_ref[...] = jnp.sum(acc, axis=0, keepdims=True) + b_ref[...]


def _ada_all(c, ada_w, ada_b):
    nl, d, n = ada_w.shape
    tn = min(ADA_TN, n)
    return pl.pallas_call(
        _ada_kernel,
        grid=(nl, n // tn),
        in_specs=[
            pl.BlockSpec((d, 1), lambda l, j: (0, 0)),
            pl.BlockSpec((None, d, tn), lambda l, j: (l, 0, j)),
            pl.BlockSpec((None, 1, tn), lambda l, j: (l, 0, j)),
        ],
        out_specs=pl.BlockSpec((None, 1, tn), lambda l, j: (l, 0, j)),
        out_shape=jax.ShapeDtypeStruct((nl, 1, n), F32),
        scratch_shapes=[pltpu.VMEM((d, 1), F32)],
        compiler_params=_cparams("parallel", "parallel"),
        name="ada_proj",
    )(c.reshape(d, 1), ada_w, ada_b.reshape(nl, 1, n))


def _rms(x, g):
    return x * lax.rsqrt(jnp.mean(x * x, axis=-1, keepdims=True) + EPS) * g


def _router(h, wr_ref, br_ref):
    w = wr_ref[...]
    h_hi = h.astype(BF16)
    h_lo = (h - h_hi.astype(F32)).astype(BF16)
    w_hi = w.astype(BF16)
    w_lo = (w - w_hi.astype(F32)).astype(BF16)
    dot = functools.partial(jnp.dot, preferred_element_type=F32)
    logits = dot(h_hi, w_hi) + (dot(h_hi, w_lo) + dot(h_lo, w_hi)) + br_ref[...]
    lane = lax.broadcasted_iota(jnp.int32, logits.shape, 1)
    big = jnp.int32(ROUTER_LANES)
    is_g = lane < N_GROUPS
    gl = jnp.where(is_g, logits, NEG_INF)
    gmax = jnp.max(gl, axis=-1, keepdims=True)
    gsum = jnp.sum(jnp.where(is_g, jnp.exp(gl - gmax), 0.0), axis=-1, keepdims=True)
    g_top_p = 1.0 / gsum
    g_top = jnp.min(jnp.where(is_g & (gl == gmax), lane, big), axis=-1, keepdims=True)
    e_lane = lane - N_GROUPS
    sel = (e_lane >= 0) & (e_lane < N_EXPERTS) & ((e_lane >> 3) == g_top)
    l1 = jnp.where(sel, logits, NEG_INF)
    m1 = jnp.max(l1, axis=-1, keepdims=True)
    i1 = jnp.min(jnp.where(sel & (l1 == m1), lane, big), axis=-1, keepdims=True)
    sel2 = sel & (lane != i1)
    l2 = jnp.where(sel2, logits, NEG_INF)
    m2 = jnp.max(l2, axis=-1, keepdims=True)
    i2 = jnp.min(jnp.where(sel2 & (l2 == m2), lane, big), axis=-1, keepdims=True)
    e21 = jnp.exp(m2 - m1)
    p1 = 1.0 / (1.0 + e21)
    p2 = e21 / (1.0 + e21)
    wts = jnp.where(lane == 0, g_top_p * p1, jnp.where(lane == 1, g_top_p * p2, 0.0))
    ids = jnp.where(lane == 0, i1 - N_GROUPS, jnp.where(lane == 1, i2 - N_GROUPS, 0))
    return wts, ids


def _load_token_major(ref, n_tok, chunks):
    return jnp.concatenate([ref[pl.ds(c, n_tok, stride=chunks), :] for c in range(chunks)], axis=1)


def _store_token_major(ref, val):
    n_tok, d = val.shape
    chunks = d // LANES
    for c in range(chunks):
        ref[pl.ds(c, n_tok, stride=chunks), :] = val[:, c * LANES:(c + 1) * LANES]


def _token_rows(tok, chunks):
    return pl.ds(pl.multiple_of(tok * chunks, chunks), chunks)


def _pre_kernel(*refs, combine, modulate, router, emit_h, h_dtype):
    it = iter(refs)
    if combine:
        dest_ref = next(it)
    x_ref = next(it)
    if combine:
        y_hbm, cw_ref, g2_ref = next(it), next(it), next(it)
    g_ref = next(it)
    if modulate:
        sh_ref, sc_ref = next(it), next(it)
    if router:
        wr_ref, br_ref = next(it), next(it)
    outs = [next(it) for _ in range(int(combine) + int(emit_h) + 2 * int(router))]
    x = x_ref[...]
    if combine:
        gbuf, gsem = next(it), next(it)
        i = pl.program_id(0)
        slot = i % 2
        tm, d = x_ref.shape
        chunks = d // LANES // 2

        def row_copy(step, sl, r, k):
            src = dest_ref[TOP_K * (step * tm + r) + k]
            return pltpu.make_async_copy(y_hbm.at[_token_rows(src, chunks)],
                                         gbuf.at[sl, k, _token_rows(r, chunks)], gsem.at[sl])

        def for_rows(fn):
            def body(r, carry):
                for k in range(TOP_K):
                    fn(r, k)
                return carry
            lax.fori_loop(0, tm, body, 0, unroll=ROW_DMA_UNROLL)

        @pl.when(i == 0)
        def _():
            for_rows(lambda r, k: row_copy(0, 0, r, k).start(priority=k))

        @pl.when(i + 1 < pl.num_programs(0))
        def _():
            for_rows(lambda r, k: row_copy(i + 1, 1 - slot, r, k).start(priority=k))

        for_rows(lambda r, k: row_copy(i, slot, r, k).wait())
        cw = cw_ref[...]
        y0 = _unpack_bf16_pairs(_load_token_major(gbuf.at[slot, 0], tm, chunks))
        y1 = _unpack_bf16_pairs(_load_token_major(gbuf.at[slot, 1], tm, chunks))
        x = x + g2_ref[...] * (cw[:, 0:1] * y0 + cw[:, 1:2] * y1)
        outs.pop(0)[...] = x
    h = _rms(x, g_ref[...])
    if modulate:
        h = h * (1.0 + sc_ref[...]) + sh_ref[...]
    if emit_h and router:
        _store_token_major(outs.pop(0), _pack_bf16_pairs(h))
    elif emit_h:
        outs.pop(0)[...] = h.astype(h_dtype)
    if router:
        wts, ids = _router(h, wr_ref, br_ref)
        outs.pop(0)[...] = wts
        outs.pop(0)[...] = ids


def _pre(x, g, *, moe=None, g2=None, shift=None, scale=None, wr=None, br=None, h_dtype=BF16):
    t, d = x.shape
    tm = min(PRE_TM, t)
    chunks = d // LANES
    combine, modulate, router = moe is not None, shift is not None, wr is not None
    row = pl.BlockSpec((tm, d), lambda i, *_: (i, 0))
    vec = pl.BlockSpec((1, d), lambda i, *_: (0, 0))
    lane_row = pl.BlockSpec((tm, ROUTER_LANES), lambda i, *_: (i, 0))
    prefetch, args, specs, scratch = [], [x], [row], []
    if combine:
        yb, dest, cw = moe
        prefetch.append(dest)
        args += [yb, cw, g2]
        specs += [pl.BlockSpec(memory_space=pl.ANY), lane_row, vec]
        scratch = [pltpu.VMEM((2, TOP_K, tm * chunks // 2, LANES), jnp.uint32), pltpu.SemaphoreType.DMA((2,))]
    args.append(g.reshape(1, d))
    specs.append(vec)
    if modulate:
        args += [shift, scale]
        specs += [vec, vec]
    if router:
        args += [wr, br]
        specs += [pl.BlockSpec(wr.shape, lambda i, *_: (0, 0)), pl.BlockSpec(br.shape, lambda i, *_: (0, 0))]
    out_shape, out_specs = [], []
    if combine:
        out_shape.append(jax.ShapeDtypeStruct((t, d), F32))
        out_specs.append(row)
    if router:
        out_shape.append(jax.ShapeDtypeStruct((t * chunks // 2, LANES), jnp.uint32))
        out_specs.append(pl.BlockSpec((tm * chunks // 2, LANES), lambda i, *_: (i, 0)))
    else:
        out_shape.append(jax.ShapeDtypeStruct((t, d), h_dtype))
        out_specs.append(row)
    if router:
        out_shape += [jax.ShapeDtypeStruct((t, ROUTER_LANES), F32), jax.ShapeDtypeStruct((t, ROUTER_LANES), jnp.int32)]
        out_specs += [lane_row, lane_row]
    kern = functools.partial(_pre_kernel, combine=combine, modulate=modulate, router=router,
                             emit_h=True, h_dtype=h_dtype)
    grid_spec = pltpu.PrefetchScalarGridSpec(
        num_scalar_prefetch=len(prefetch), grid=(t // tm,), in_specs=specs, out_specs=out_specs,
        scratch_shapes=scratch)
    return pl.pallas_call(
        kern, grid_spec=grid_spec, out_shape=out_shape,
        compiler_params=_cparams("arbitrary" if combine else "parallel"), name="pre_norm",
    )(*prefetch, *args)


def _mm_kernel(*refs, n_pairs, n_extras, epilogue):
    o_ref = refs[2 * n_pairs + n_extras]
    caches = refs[2 * n_pairs + n_extras + 1:]

    @pl.when(pl.program_id(1) == 0)
    def _():
        for p in range(n_pairs):
            caches[p][...] = refs[2 * p + 1][...].astype(BF16)

    acc = None
    for p in range(n_pairs):
        part = jnp.dot(refs[2 * p][...].astype(BF16), caches[p][...], preferred_element_type=F32)
        acc = part if acc is None else (acc, part)
    extras = [r[...] for r in refs[2 * n_pairs:2 * n_pairs + n_extras]]
    o_ref[...] = epilogue(acc, *extras).astype(o_ref.dtype)


def _mm(pairs, extras, epilogue, n, out_dtype, name, tn=MM_TN):
    m = pairs[0][0].shape[0]
    tm, tn = min(MM_TM, m), min(tn, n)
    args, specs, scratch = [], [], []
    for a, w, layer, off in pairs:
        k = a.shape[1]
        args += [a, w]
        specs += [pl.BlockSpec((tm, k), lambda j, i: (i, 0)),
                  pl.BlockSpec((None, k, tn), lambda j, i, layer=layer, off=off: (layer, 0, j + off // tn))]
        scratch.append(pltpu.VMEM((k, tn), BF16))
    for arr, bshape, imap in extras:
        args.append(arr)
        specs.append(pl.BlockSpec(bshape(tm, tn), imap(tn)))
    return pl.pallas_call(
        functools.partial(_mm_kernel, n_pairs=len(pairs), n_extras=len(extras), epilogue=epilogue),
        grid=(n // tn, m // tm), in_specs=specs,
        out_specs=pl.BlockSpec((tm, tn), lambda j, i: (i, j)),
        out_shape=jax.ShapeDtypeStruct((m, n), out_dtype),
        scratch_shapes=scratch,
        compiler_params=_cparams("arbitrary", "arbitrary"), name=name,
    )(*args)


def _tile(tm, tn):
    return (tm, tn)


def _rowvec(tm, tn):
    return (1, tn)


def _at_cols(off):
    return lambda tn: (lambda j, i: (i, j + off // tn))


def _vec_cols(off):
    return lambda tn: (lambda j, i: (0, j + off // tn))


def _attn_bias_fill(rpb_ref, b_ref, head0):
    n_dr, n_dc = 2 * WIN_ROWS - 1, 2 * WIN_COLS - 1
    qc = lax.broadcasted_iota(jnp.int32, (GRID_W, 2 * GRID_W), 0)
    kc = lax.broadcasted_iota(jnp.int32, (GRID_W, 2 * GRID_W), 1) & (GRID_W - 1)
    cs = jnp.clip(qc - WIN_COLS // 2, 0, GRID_W - WIN_COLS)
    inside = (kc >= cs) & (kc < cs + WIN_COLS)
    dc = jnp.clip(kc - qc, -(WIN_COLS - 1), WIN_COLS - 1) + (WIN_COLS - 1)
    for h in range(2):
        base = (head0 + h) * (n_dr * n_dc)
        for dr in range(n_dr):
            t = jnp.zeros((GRID_W, 2 * GRID_W), F32)
            for d in range(n_dc):
                t = jnp.where(dc == d, rpb_ref[base + dr * n_dc + d], t)
            t = jnp.where(inside, t, NEG_INF)
            for var in range(WIN_ROWS):
                j = dr - var
                if 0 <= j < WIN_ROWS:
                    c0 = j * GRID_W
                    lo = c0 % (2 * GRID_W)
                    b_ref[var, h * GRID_W:(h + 1) * GRID_W, c0:c0 + GRID_W] = t[:, lo:lo + GRID_W]


def _attn_kernel(rpb_ref, q_ref, k_ref, v_ref, o_ref, b_ref, s_ref, *, n_rows):
    rb = pl.program_id(1)
    n_q = q_ref.shape[0]
    wkeys = WIN_ROWS * GRID_W
    dh = NA_HEAD_DIM
    lane = lax.broadcasted_iota(jnp.int32, (GRID_W, 2 * dh), 1)
    first = lane < dh

    @pl.when(rb == 0)
    def _():
        _attn_bias_fill(rpb_ref, b_ref, 2 * pl.program_id(0))

    def window(i):
        r = rb * n_q + i
        rs = jnp.clip(r - WIN_ROWS // 2, 0, n_rows - WIN_ROWS)
        return rs, rs - r + (WIN_ROWS - 1)

    for i in range(n_q):
        rs, var = window(i)
        kw = k_ref[pl.ds(rs, WIN_ROWS)].reshape(wkeys, 2 * dh)
        q = q_ref[i] * jnp.asarray(dh ** -0.5, BF16)
        zero = jnp.zeros_like(q)
        q2 = jnp.concatenate([jnp.where(first, q, zero), jnp.where(first, zero, q)], axis=0)
        s = lax.dot_general(q2, kw, (((1,), (1,)), ((), ())), preferred_element_type=F32)
        b = b_ref[var]
        s_ref[i] = s + b
    for i in range(n_q):
        rs, _ = window(i)
        vw = v_ref[pl.ds(rs, WIN_ROWS)].reshape(wkeys, 2 * dh)
        s = s_ref[i]
        m = jnp.max(s, axis=-1, keepdims=True)
        p = jnp.exp(s - m)
        l = jnp.sum(p, axis=-1, keepdims=True)
        o = jnp.dot(p.astype(BF16), vw, preferred_element_type=F32) / l
        o_ref[i] = jnp.where(first, o[:GRID_W], o[GRID_W:]).astype(o_ref.dtype)


def _attention(proj, rpb):
    t, width = proj.shape
    n_rows = t // GRID_W
    pairs = NA_HEADS // 2
    p3 = proj.reshape(n_rows, GRID_W, width)
    blk = 2 * NA_HEAD_DIM
    rows = min(ATTN_ROWS, n_rows)
    wkeys = WIN_ROWS * GRID_W
    out = pl.pallas_call(
        functools.partial(_attn_kernel, n_rows=n_rows),
        grid_spec=pltpu.PrefetchScalarGridSpec(
            num_scalar_prefetch=1, grid=(pairs, n_rows // rows),
            in_specs=[
                pl.BlockSpec((rows, GRID_W, blk), lambda hp, rb, _: (rb, 0, hp)),
                pl.BlockSpec((n_rows, GRID_W, blk), lambda hp, rb, _: (0, 0, pairs + hp)),
                pl.BlockSpec((n_rows, GRID_W, blk), lambda hp, rb, _: (0, 0, 2 * pairs + hp)),
            ],
            out_specs=pl.BlockSpec((rows, GRID_W, blk), lambda hp, rb, _: (rb, 0, hp)),
            scratch_shapes=[pltpu.VMEM((WIN_ROWS, 2 * GRID_W, wkeys), F32),
                            pltpu.VMEM((rows, 2 * GRID_W, wkeys), F32)]),
        out_shape=jax.ShapeDtypeStruct((n_rows, GRID_W, NA_HEADS * NA_HEAD_DIM), BF16),
        compiler_params=_cparams("arbitrary", "arbitrary"), name="na_attention",
    )(rpb.reshape(-1), p3, p3, p3)
    return out.reshape(t, NA_HEADS * NA_HEAD_DIM)


def _dft_tables(n1, n2, dc):
    s = n1 * n2
    a = np.arange(dc, dtype=np.float64)
    ang_c = 2 * np.pi * np.outer(a, a) / dc
    w_c = np.concatenate([np.cos(ang_c), -np.sin(ang_c)], axis=1)
    a1 = np.arange(n1, dtype=np.float64)
    ang1 = 2 * np.pi * np.outer(a1, a1) / n1
    c1, s1 = np.cos(ang1), np.sin(ang1)
    m1 = np.block([[c1, s1], [-s1, c1]])
    a2 = np.arange(n2, dtype=np.float64)
    ang_t = 2 * np.pi * np.outer(a2, a1) / s
    tw_c = np.repeat(np.cos(ang_t)[:, :, None], LANES, axis=2)
    tw_s = np.repeat(np.sin(ang_t)[:, :, None], LANES, axis=2)
    ang2 = 2 * np.pi * np.outer(a2, a2) / n2
    m2 = np.concatenate([np.cos(ang2), np.sin(ang2)], axis=1)
    return (jnp.asarray(w_c, BF16), jnp.asarray(m1, BF16), jnp.asarray(tw_c, F32), jnp.asarray(tw_s, F32),
            jnp.asarray(m2, BF16))


def _fft1_kernel(u_hbm, wc_ref, m1_ref, tc_ref, ts_ref, br_hbm, bi_hbm,
                 ubuf, zz_ref, obuf, isem, osem, *, n1, dc):
    s = pl.program_id(0)
    ns = pl.num_programs(0)
    jb = ubuf.shape[1]
    slot = s % 2

    def in_copy(step, sl, jj):
        return pltpu.make_async_copy(u_hbm.at[:, step * jb + jj], ubuf.at[sl, jj], isem.at[sl])

    def out_copies(step, sl, jj):
        j = step * jb + jj
        return (pltpu.make_async_copy(obuf.at[sl, 0, jj], br_hbm.at[:, j], osem.at[sl]),
                pltpu.make_async_copy(obuf.at[sl, 1, jj], bi_hbm.at[:, j], osem.at[sl]))

    @pl.when(s == 0)
    def _():
        for jj in range(jb):
            in_copy(0, 0, jj).start()

    @pl.when(s + 1 < ns)
    def _():
        for jj in range(jb):
            in_copy(s + 1, 1 - slot, jj).start()

    for jj in range(jb):
        in_copy(s, slot, jj).wait()

    @pl.when(s >= 2)
    def _():
        for jj in range(jb):
            for cp in out_copies(s - 2, slot, jj):
                cp.wait()

    c = ubuf.shape[3]
    u = ubuf[slot].reshape(jb * n1, c).astype(BF16)
    for g in range(c // dc):
        z = jnp.dot(u[:, g * dc:(g + 1) * dc], wc_ref[...], preferred_element_type=F32)
        z = z.astype(BF16)
        for jj in range(jb):
            zz_ref[jj, :n1, g * dc:(g + 1) * dc] = z[jj * n1:(jj + 1) * n1, :dc]
            zz_ref[jj, n1:, g * dc:(g + 1) * dc] = z[jj * n1:(jj + 1) * n1, dc:]
    for jj in range(jb):
        a = jnp.dot(m1_ref[...], zz_ref[jj], preferred_element_type=F32)
        ar, ai = a[:n1], a[n1:]
        tc = jnp.concatenate([tc_ref[jj]] * (c // LANES), axis=1)
        ts = jnp.concatenate([ts_ref[jj]] * (c // LANES), axis=1)
        obuf[slot, 0, jj] = ar * tc + ai * ts
        obuf[slot, 1, jj] = ai * tc - ar * ts
    for jj in range(jb):
        for cp in out_copies(s, slot, jj):
            cp.start()

    @pl.when(s == ns - 1)
    def _():
        for jj in range(jb):
            for cp in out_copies(s, slot, jj):
                cp.wait()

        @pl.when(s >= 1)
        def _():
            for jj in range(jb):
                for cp in out_copies(s - 1, 1 - slot, jj):
                    cp.wait()


def _fft2_kernel(br_ref, bi_ref, m2_ref, y_hbm, obuf, osem, *, n2, scale):
    s = pl.program_id(0)
    ns = pl.num_programs(0)
    kb = obuf.shape[1]
    slot = s % 2

    def out_copy(step, sl, kk):
        return pltpu.make_async_copy(obuf.at[sl, kk], y_hbm.at[:, step * kb + kk], osem.at[sl])

    @pl.when(s >= 2)
    def _():
        for kk in range(kb):
            out_copy(s - 2, slot, kk).wait()

    for kk in range(kb):
        rows = slice(kk * n2, (kk + 1) * n2)
        bb = jnp.concatenate([br_ref[rows, :], bi_ref[rows, :]], axis=0).astype(BF16)
        obuf[slot, kk] = jnp.dot(m2_ref[...], bb, preferred_element_type=F32) * scale
    for kk in range(kb):
        out_copy(s, slot, kk).start()

    @pl.when(s == ns - 1)
    def _():
        for kk in range(kb):
            out_copy(s, slot, kk).wait()

        @pl.when(s >= 1)
        def _():
            for kk in range(kb):
                out_copy(s - 1, 1 - slot, kk).wait()


def _fourier(u):
    t, fw = u.shape
    n2 = GRID_W
    n1 = t // n2
    dc = F_GROUP_DIM
    w_c, m1, tw_c, tw_s, m2 = _dft_tables(n1, n2, dc)
    jb = min(FFT_JB, n2)
    kb = min(FFT_KB, n1)
    any_spec = pl.BlockSpec(memory_space=pl.ANY)
    tw_spec = pl.BlockSpec((jb, n1, LANES), lambda s: (s, 0, 0))
    b_re, b_im = pl.pallas_call(
        functools.partial(_fft1_kernel, n1=n1, dc=dc),
        grid=(n2 // jb,),
        in_specs=[
            any_spec,
            pl.BlockSpec((dc, 2 * dc), lambda s: (0, 0)),
            pl.BlockSpec((2 * n1, 2 * n1), lambda s: (0, 0)),
            tw_spec, tw_spec,
        ],
        out_specs=[any_spec, any_spec],
        out_shape=[jax.ShapeDtypeStruct((n1, n2, fw), F32)] * 2,
        scratch_shapes=[
            pltpu.VMEM((2, jb, n1, fw), F32),
            pltpu.VMEM((jb, 2 * n1, fw), BF16),
            pltpu.VMEM((2, 2, jb, n1, fw), F32),
            pltpu.SemaphoreType.DMA((2,)),
            pltpu.SemaphoreType.DMA((2,)),
        ],
        compiler_params=_cparams("arbitrary"), name="fourier_stage1",
    )(u.reshape(n1, n2, fw), w_c, m1, tw_c, tw_s)
    scale = 1.0 / math.sqrt(t * dc)
    in_blk = pl.BlockSpec((kb * n2, fw), lambda s: (s, 0))
    y = pl.pallas_call(
        functools.partial(_fft2_kernel, n2=n2, scale=scale),
        grid=(n1 // kb,),
        in_specs=[in_blk, in_blk, pl.BlockSpec((n2, 2 * n2), lambda s: (0, 0))],
        out_specs=any_spec,
        out_shape=jax.ShapeDtypeStruct((n2, n1, fw), F32),
        scratch_shapes=[pltpu.VMEM((2, kb, n2, fw), F32), pltpu.SemaphoreType.DMA((2,))],
        compiler_params=_cparams("arbitrary"), name="fourier_stage2",
    )(b_re.reshape(t, fw), b_im.reshape(t, fw), m2)
    return y.reshape(t, fw)


def _dispatch_kernel(dest_ref, pad0_ref, padn_ref, na_ref, h_ref, xb_hbm, zbuf, sem, zsem, *, chunks):
    i = pl.program_id(0)
    tm = h_ref.shape[0] // chunks
    tb = zbuf.shape[0] // chunks
    nb = xb_hbm.shape[0] // (tb * chunks)

    def tail_copies(fn):
        def body(b, carry):
            fn(pltpu.make_async_copy(zbuf, xb_hbm.at[_token_rows(b, tb * chunks)], zsem))
            return carry
        lax.fori_loop(na_ref[0], nb, body, 0)

    def pad_copies(e, fn):
        pos = pad0_ref[e]
        n = padn_ref[e]
        bit = tb // 2
        while bit >= 1:
            @pl.when((n & bit) != 0)
            def _(pos=pos, bit=bit):
                rows = pl.ds(pl.multiple_of(pos * chunks, chunks), bit * chunks)
                fn(pltpu.make_async_copy(zbuf.at[pl.ds(0, bit * chunks)], xb_hbm.at[rows], zsem))
            pos = pos + (n & bit)
            bit //= 2

    @pl.when(i == 0)
    def _():
        zbuf[...] = jnp.zeros_like(zbuf)
        lax.fori_loop(0, N_EXPERTS, lambda e, c: (pad_copies(e, lambda cp: cp.start()), c)[1], 0)
        tail_copies(lambda cp: cp.start())

    def row_copy(r, k):
        dst = dest_ref[TOP_K * (i * tm + r) + k]
        return pltpu.make_async_copy(h_ref.at[_token_rows(r, chunks)], xb_hbm.at[_token_rows(dst, chunks)], sem)

    def for_rows(fn):
        def body(r, carry):
            for k in range(TOP_K):
                fn(r, k)
            return carry
        lax.fori_loop(0, tm, body, 0, unroll=ROW_DMA_UNROLL)

    for_rows(lambda r, k: row_copy(r, k).start(priority=k))
    for_rows(lambda r, k: row_copy(r, k).wait())

    @pl.when(i == 0)
    def _():
        lax.fori_loop(0, N_EXPERTS, lambda e, c: (pad_copies(e, lambda cp: cp.wait()), c)[1], 0)
        tail_copies(lambda cp: cp.wait())


def _pack_bf16_pairs(h):
    half = h.shape[1] // 2
    bits = lax.bitcast_convert_type(h.astype(BF16).astype(F32), jnp.uint32)
    return bits[:, half:] | (bits[:, :half] >> 16)


def _unpack_bf16_pairs(p):
    lo = lax.bitcast_convert_type(p << 16, F32)
    hi = lax.bitcast_convert_type(p & jnp.uint32(0xFFFF0000), F32)
    return jnp.concatenate([lo, hi], axis=1)


def _expert_kernel(be_ref, bi_ref, first_ref, nxt_ref, na_ref, x_ref, wg_hbm, wu_hbm, wd_hbm, y_ref,
                   wg_st, wu_st, wd_st, wg_c, wu_c, wd_c, wsem, *, layer):
    b = pl.program_id(0)
    active = b < na_ref[0]
    d = wg_st.shape[0]
    xchunks = d // LANES // 2
    tb = x_ref.shape[0] // xchunks

    def weight_copies(e):
        return [pltpu.make_async_copy(src.at[layer, e], dst, wsem.at[n])
                for n, (src, dst) in enumerate(((wg_hbm, wg_st), (wu_hbm, wu_st), (wd_hbm, wd_st)))]

    @pl.when(b == 0)
    def _():
        for cp in weight_copies(be_ref[0]):
            cp.start()

    @pl.when(active & (first_ref[b] == 1))
    def _():
        for cp in weight_copies(be_ref[b]):
            cp.wait()
        wg_c[...] = wg_st[...].astype(BF16)
        wu_c[...] = wu_st[...].astype(BF16)
        wd_c[...] = wd_st[...].astype(BF16)

        @pl.when(nxt_ref[b] >= 0)
        def _():
            for cp in weight_copies(nxt_ref[b]):
                cp.start()

    @pl.when(active)
    def _():
        x = _unpack_bf16_pairs(_load_token_major(x_ref, tb, xchunks)).astype(BF16)
        gate = jnp.dot(x, wg_c[...], preferred_element_type=F32)
        up = jnp.dot(x, wu_c[...], preferred_element_type=F32)
        act = (gate * _sigmoid(gate) * up).astype(BF16)
        _store_token_major(y_ref, _pack_bf16_pairs(jnp.dot(act, wd_c[...], preferred_element_type=F32)))

    @pl.when(jnp.logical_not(active))
    def _():
        y_ref[...] = jnp.zeros_like(y_ref)


def _moe_plan(experts, t, tb):
    a = t * TOP_K
    e_flat = experts.reshape(a)
    lanes = jnp.arange(N_EXPERTS, dtype=jnp.int32)[None, :]
    onehot = (e_flat[:, None] == lanes).astype(jnp.int32)
    csum = jnp.cumsum(onehot, axis=0)
    counts = csum[-1]
    pcounts = (counts + tb - 1) // tb * tb
    pends = jnp.cumsum(pcounts)
    pstarts = pends - pcounts
    dest = jnp.sum(onehot * (pstarts[None, :] + csum - 1), axis=1).astype(jnp.int32)
    nb = (a + N_EXPERTS * tb) // tb
    n_act = (pends[-1] // tb).astype(jnp.int32)
    blk = jnp.arange(nb, dtype=jnp.int32)
    blk_idx = jnp.minimum(blk, n_act - 1)
    be = jnp.sum((pends[None, :] <= (blk_idx * tb)[:, None]).astype(jnp.int32), axis=1)
    be = jnp.minimum(be, N_EXPERTS - 1).astype(jnp.int32)
    first = ((blk * tb == pstarts[be]) & (blk < n_act)).astype(jnp.int32)
    after = pends[be] // tb
    nxt = jnp.where(after < n_act, be[jnp.minimum(after, nb - 1)], -1).astype(jnp.int32)
    plan = dict(dest=dest, be=be, blk_idx=blk_idx, first=first, nxt=nxt, n_act=n_act.reshape(1),
                pad0=(pstarts + counts).astype(jnp.int32), padn=(pcounts - counts).astype(jnp.int32))
    return plan


def _experts(h, experts, layer, w_gate, w_up, w_down):
    d, de = w_gate.shape[2:]
    chunks = d // LANES
    xchunks = chunks // 2
    t = h.shape[0] // xchunks
    tb = EXPERT_TB
    tm = min(PRE_TM, t)
    plan = _moe_plan(experts, t, tb)
    nb = plan["be"].shape[0]
    p = nb * tb
    xb = pl.pallas_call(
        functools.partial(_dispatch_kernel, chunks=xchunks),
        grid_spec=pltpu.PrefetchScalarGridSpec(
            num_scalar_prefetch=4, grid=(t // tm,),
            in_specs=[pl.BlockSpec((tm * xchunks, LANES), lambda i, *_: (i, 0))],
            out_specs=pl.BlockSpec(memory_space=pl.ANY),
            scratch_shapes=[pltpu.VMEM((tb * xchunks, LANES), h.dtype),
                            pltpu.SemaphoreType.DMA(()), pltpu.SemaphoreType.DMA(())]),
        out_shape=jax.ShapeDtypeStruct((p * xchunks, LANES), h.dtype),
        compiler_params=_cparams("arbitrary"), name="moe_dispatch",
    )(plan["dest"], plan["pad0"], plan["padn"], plan["n_act"], h)
    hbm = pl.BlockSpec(memory_space=pl.ANY)
    yb = pl.pallas_call(
        functools.partial(_expert_kernel, layer=layer),
        grid_spec=pltpu.PrefetchScalarGridSpec(
            num_scalar_prefetch=5, grid=(nb,),
            in_specs=[pl.BlockSpec((tb * xchunks, LANES), lambda b, be, bi, *_: (bi[b], 0)), hbm, hbm, hbm],
            out_specs=pl.BlockSpec((tb * xchunks, LANES), lambda b, *_: (b, 0)),
            scratch_shapes=[pltpu.VMEM((d, de), F32), pltpu.VMEM((d, de), F32), pltpu.VMEM((de, d), F32),
                            pltpu.VMEM((d, de), BF16), pltpu.VMEM((d, de), BF16), pltpu.VMEM((de, d), BF16),
                            pltpu.SemaphoreType.DMA((3,))]),
        out_shape=jax.ShapeDtypeStruct((p * xchunks, LANES), jnp.uint32),
        compiler_params=_cparams("arbitrary"), name="moe_experts",
    )(plan["be"], plan["blk_idx"], plan["first"], plan["nxt"], plan["n_act"], xb, w_gate, w_up, w_down)
    return yb, plan["dest"]


def kernel(x, c, norm1_g, norm2_g, ada_w, ada_b, mix_in_w, na_rpb, na_out_w, fourier_out_w, branch_gate_w,
           branch_gate_b, mix_out_w, router_group_w, router_group_b, router_expert_w, router_expert_b,
           expert_w_gate, expert_w_up, expert_w_down, final_g):
    bsz, s, d = x.shape
    assert bsz == 1 and s % GRID_W == 0
    depth = ada_w.shape[0]
    na_w = NA_HEADS * NA_HEAD_DIM
    f_w = F_GROUPS * F_GROUP_DIM
    xs = x.reshape(s, d)
    mod = _ada_all(c, ada_w, ada_b)
    pad = ROUTER_LANES - N_GROUPS - N_EXPERTS
    wr_all = jnp.concatenate([router_group_w, router_expert_w, jnp.zeros((depth, d, pad), F32)], axis=2)
    br_all = jnp.concatenate([router_group_b, router_expert_b, jnp.zeros((depth, pad), F32)], axis=1)
    moe = g2 = None
    for l in range(depth):
        sh1, sc1, g1, sh2, sc2, g2_l = [mod[l, :, i * d:(i + 1) * d] for i in range(N_MOD)]
        if moe is None:
            (h,) = _pre(xs, norm1_g[l], shift=sh1, scale=sc1)
        else:
            xs, h = _pre(xs, norm1_g[l], moe=moe, g2=g2, shift=sh1, scale=sc1)
        qkv = _mm([(h, mix_in_w, l, 0)], [], lambda acc: acc, 3 * na_w, BF16, "mix_in_qkv", tn=MM_TN_WIDE)
        u = _mm([(h, mix_in_w, l, 3 * na_w)], [], lambda acc: acc, f_w, F32, "mix_in_fourier", tn=MM_TN_WIDE)
        gates = _mm([(h, branch_gate_w, l, 0)],
                    [(branch_gate_b[l].reshape(1, -1), _rowvec, _vec_cols(0))],
                    lambda acc, bias: _sigmoid(acc + bias), branch_gate_w.shape[2], BF16, "branch_gates",
                    tn=MM_TN_WIDE)
        y_att = _attention(qkv, na_rpb[l])
        y_fft = _fourier(u)
        mixed = _mm([(y_att, na_out_w, l, 0), (y_fft, fourier_out_w, l, 0)],
                    [(gates, _tile, _at_cols(0)), (gates, _tile, _at_cols(d))],
                    lambda acc, ga, gf: ga * acc[0] + gf * acc[1], d, BF16, "branch_mix")
        xs = _mm([(mixed, mix_out_w, l, 0)],
                 [(xs, _tile, _at_cols(0)), (g1, _rowvec, _vec_cols(0))],
                 lambda acc, xr, g: xr + g * acc, d, F32, "mix_out", tn=MM_TN_WIDE)
        h2, wts, ids = _pre(xs, norm2_g[l], shift=sh2, scale=sc2, wr=wr_all[l], br=br_all[l].reshape(1, -1),
                            h_dtype=F32)
        yb, dest = _experts(h2, ids[:, :TOP_K], l, expert_w_gate, expert_w_up, expert_w_down)
        moe, g2 = (yb, dest, wts), g2_l
    _, out = _pre(xs, final_g, moe=moe, g2=g2, h_dtype=F32)
    return out.reshape(bsz, s, d)
```

```python
import functools
import math

import numpy as np
import jax
import jax.numpy as jnp
from jax import lax
from jax.experimental import pallas as pl
from jax.experimental.pallas import tpu as pltpu

F32 = jnp.float32
BF16 = jnp.bfloat16

GRID_W = 64
NA_HEADS = 16
NA_HEAD_DIM = 64
WIN_ROWS = 8
WIN_COLS = 16
F_GROUPS = 4
F_GROUP_DIM = 256
N_GROUPS = 4
EXPERTS_PER_GROUP = 8
N_EXPERTS = N_GROUPS * EXPERTS_PER_GROUP
TOP_K = 2
N_MOD = 6
EPS = 1e-6
NEG_INF = -1e30

LANES = 128
SUBLANES = 8

ADA_TN = 1536
ADA_ROWS = 256
PRE_TM = 256
MM_TM = 1024
MM_TM_FUSED = 512
MM_TN = 512
MM_TN_WIDE = 1024
ATTN_ROWS = 8
EXPERT_TB = 256
ROW_DMA_UNROLL = 8
FFT_JB = 4
FFT_KB = 8
ROUTER_LANES = LANES

VMEM_LIMIT = 56 * 1024 * 1024


def _cparams(*sem):
    return pltpu.CompilerParams(dimension_semantics=sem, vmem_limit_bytes=VMEM_LIMIT)


def _sigmoid(x):
    return 1.0 / (1.0 + jnp.exp(-x))


def _ada_kernel(c_ref, w_ref, b_ref, o_ref, cs_ref):
    c = c_ref[...]
    cs_ref[...] = c * _sigmoid(c)
    d, tn = w_ref.shape

    def body(i, acc):
        r0 = pl.multiple_of(i * ADA_ROWS, ADA_ROWS)
        prod = w_ref[pl.ds(r0, ADA_ROWS), :] * cs_ref[pl.ds(r0, ADA_ROWS), :]
        return acc + jnp.sum(prod.reshape(ADA_ROWS // SUBLANES, SUBLANES, tn), axis=0)

    acc = lax.fori_loop(0, d // ADA_ROWS, body, jnp.zeros((SUBLANES, tn), F32))
    o_ref[...] = jnp.sum(acc, axis=0, keepdims=True) + b_ref[...]


def _ada_all(c, ada_w, ada_b):
    nl, d, n = ada_w.shape
    tn = min(ADA_TN, n)
    return pl.pallas_call(
        _ada_kernel,
        grid=(nl, n // tn),
        in_specs=[
            pl.BlockSpec((d, 1), lambda l, j: (0, 0)),
            pl.BlockSpec((None, d, tn), lambda l, j: (l, 0, j)),
            pl.BlockSpec((None, 1, tn), lambda l, j: (l, 0, j)),
        ],
        out_specs=pl.BlockSpec((None, 1, tn), lambda l, j: (l, 0, j)),
        out_shape=jax.ShapeDtypeStruct((nl, 1, n), F32),
        scratch_shapes=[pltpu.VMEM((d, 1), F32)],
        compiler_params=_cparams("parallel", "parallel"),
        name="ada_proj",
    )(c.reshape(d, 1), ada_w, ada_b.reshape(nl, 1, n))


def _rms(x, g):
    return x * lax.rsqrt(jnp.mean(x * x, axis=-1, keepdims=True) + EPS) * g


def _router(h, wr_ref, br_ref):
    w = wr_ref[...]
    h_hi = h.astype(BF16)
    h_lo = (h - h_hi.astype(F32)).astype(BF16)
    w_hi = w.astype(BF16)
    w_lo = (w - w_hi.astype(F32)).astype(BF16)
    dot = functools.partial(jnp.dot, preferred_element_type=F32)
    logits = dot(h_hi, w_hi) + (dot(h_hi, w_lo) + dot(h_lo, w_hi)) + br_ref[...]
    lane = lax.broadcasted_iota(jnp.int32, logits.shape, 1)
    big = jnp.int32(ROUTER_LANES)
    is_g = lane < N_GROUPS
    gl = jnp.where(is_g, logits, NEG_INF)
    gmax = jnp.max(gl, axis=-1, keepdims=True)
    gsum = jnp.sum(jnp.where(is_g, jnp.exp(gl - gmax), 0.0), axis=-1, keepdims=True)
    g_top_p = 1.0 / gsum
    g_top = jnp.min(jnp.where(is_g & (gl == gmax), lane, big), axis=-1, keepdims=True)
    e_lane = lane - N_GROUPS
    sel = (e_lane >= 0) & (e_lane < N_EXPERTS) & ((e_lane >> 3) == g_top)
    l1 = jnp.where(sel, logits, NEG_INF)
    m1 = jnp.max(l1, axis=-1, keepdims=True)
    i1 = jnp.min(jnp.where(sel & (l1 == m1), lane, big), axis=-1, keepdims=True)
    sel2 = sel & (lane != i1)
    l2 = jnp.where(sel2, logits, NEG_INF)
    m2 = jnp.max(l2, axis=-1, keepdims=True)
    i2 = jnp.min(jnp.where(sel2 & (l2 == m2), lane, big), axis=-1, keepdims=True)
    e21 = jnp.exp(m2 - m1)
    p1 = 1.0 / (1.0 + e21)
    p2 = e21 / (1.0 + e21)
    wts = jnp.where(lane == 0, g_top_p * p1, jnp.where(lane == 1, g_top_p * p2, 0.0))
    ids = jnp.where(lane == 0, i1 - N_GROUPS, jnp.where(lane == 1, i2 - N_GROUPS, 0))
    return wts, ids


def _load_token_major(ref, n_tok, chunks):
    return jnp.concatenate([ref[pl.ds(c, n_tok, stride=chunks), :] for c in range(chunks)], axis=1)


def _store_token_major(ref, val):
    n_tok, d = val.shape
    chunks = d // LANES
    for c in range(chunks):
        ref[pl.ds(c, n_tok, stride=chunks), :] = val[:, c * LANES:(c + 1) * LANES]


def _token_rows(tok, chunks):
    return pl.ds(pl.multiple_of(tok * chunks, chunks), chunks)


def _pre_kernel(*refs, combine, modulate, router, emit_h, h_dtype):
    it = iter(refs)
    if combine:
        dest_ref = next(it)
    x_ref = next(it)
    if combine:
        y_hbm, cw_ref, g2_ref = next(it), next(it), next(it)
    g_ref = next(it)
    if modulate:
        sh_ref, sc_ref = next(it), next(it)
    if router:
        wr_ref, br_ref = next(it), next(it)
    outs = [next(it) for _ in range(int(combine) + int(emit_h) + 2 * int(router))]
    x = x_ref[...]
    if combine:
        gbuf, gsem = next(it), next(it)
        i = pl.program_id(0)
        slot = i % 2
        tm, d = x_ref.shape
        chunks = d // LANES // 2

        def row_copy(step, sl, r, k):
            src = dest_ref[TOP_K * (step * tm + r) + k]
            return pltpu.make_async_copy(y_hbm.at[_token_rows(src, chunks)],
                                         gbuf.at[sl, k, _token_rows(r, chunks)], gsem.at[sl])

        def for_rows(fn):
            def body(r, carry):
                for k in range(TOP_K):
                    fn(r, k)
                return carry
            lax.fori_loop(0, tm, body, 0, unroll=ROW_DMA_UNROLL)

        @pl.when(i == 0)
        def _():
            for_rows(lambda r, k: row_copy(0, 0, r, k).start(priority=k))

        @pl.when(i + 1 < pl.num_programs(0))
        def _():
            for_rows(lambda r, k: row_copy(i + 1, 1 - slot, r, k).start(priority=k))

        for_rows(lambda r, k: row_copy(i, slot, r, k).wait())
        cw = cw_ref[...]
        y0 = _unpack_bf16_pairs(_load_token_major(gbuf.at[slot, 0], tm, chunks))
        y1 = _unpack_bf16_pairs(_load_token_major(gbuf.at[slot, 1], tm, chunks))
        x = x + g2_ref[...] * (cw[:, 0:1] * y0 + cw[:, 1:2] * y1)
        outs.pop(0)[...] = x
    h = _rms(x, g_ref[...])
    if modulate:
        h = h * (1.0 + sc_ref[...]) + sh_ref[...]
    if emit_h and router:
        _store_token_major(outs.pop(0), _pack_bf16_pairs(h))
    elif emit_h:
        outs.pop(0)[...] = h.astype(h_dtype)
    if router:
        wts, ids = _router(h, wr_ref, br_ref)
        outs.pop(0)[...] = wts
        outs.pop(0)[...] = ids


def _pre(x, g, *, moe=None, g2=None, shift=None, scale=None, wr=None, br=None, h_dtype=BF16):
    t, d = x.shape
    tm = min(PRE_TM, t)
    chunks = d // LANES
    combine, modulate, router = moe is not None, shift is not None, wr is not None
    row = pl.BlockSpec((tm, d), lambda i, *_: (i, 0))
    vec = pl.BlockSpec((1, d), lambda i, *_: (0, 0))
    lane_row = pl.BlockSpec((tm, ROUTER_LANES), lambda i, *_: (i, 0))
    prefetch, args, specs, scratch = [], [x], [row], []
    if combine:
        yb, dest, cw = moe
        prefetch.append(dest)
        args += [yb, cw, g2]
        specs += [pl.BlockSpec(memory_space=pl.ANY), lane_row, vec]
        scratch = [pltpu.VMEM((2, TOP_K, tm * chunks // 2, LANES), jnp.uint32), pltpu.SemaphoreType.DMA((2,))]
    args.append(g.reshape(1, d))
    specs.append(vec)
    if modulate:
        args += [shift, scale]
        specs += [vec, vec]
    if router:
        args += [wr, br]
        specs += [pl.BlockSpec(wr.shape, lambda i, *_: (0, 0)), pl.BlockSpec(br.shape, lambda i, *_: (0, 0))]
    out_shape, out_specs = [], []
    if combine:
        out_shape.append(jax.ShapeDtypeStruct((t, d), F32))
        out_specs.append(row)
    if router:
        out_shape.append(jax.ShapeDtypeStruct((t * chunks // 2, LANES), jnp.uint32))
        out_specs.append(pl.BlockSpec((tm * chunks // 2, LANES), lambda i, *_: (i, 0)))
    else:
        out_shape.append(jax.ShapeDtypeStruct((t, d), h_dtype))
        out_specs.append(row)
    if router:
        out_shape += [jax.ShapeDtypeStruct((t, ROUTER_LANES), F32), jax.ShapeDtypeStruct((t, ROUTER_LANES), jnp.int32)]
        out_specs += [lane_row, lane_row]
    kern = functools.partial(_pre_kernel, combine=combine, modulate=modulate, router=router,
                             emit_h=True, h_dtype=h_dtype)
    grid_spec = pltpu.PrefetchScalarGridSpec(
        num_scalar_prefetch=len(prefetch), grid=(t // tm,), in_specs=specs, out_specs=out_specs,
        scratch_shapes=scratch)
    return pl.pallas_call(
        kern, grid_spec=grid_spec, out_shape=out_shape,
        compiler_params=_cparams("arbitrary" if combine else "parallel"), name="pre_norm",
    )(*prefetch, *args)


def _mm_kernel(*refs, lhs_of, n_lhs, n_extras, epilogue):
    n_pairs = len(lhs_of)
    lhs = refs[:n_lhs]
    weights = refs[n_lhs:n_lhs + n_pairs]
    extras = refs[n_lhs + n_pairs:n_lhs + n_pairs + n_extras]
    o_ref = refs[n_lhs + n_pairs + n_extras]
    caches = refs[n_lhs + n_pairs + n_extras + 1:]

    @pl.when(pl.program_id(1) == 0)
    def _():
        for p in range(n_pairs):
            caches[p][...] = weights[p][...].astype(BF16)

    def product(p):
        return jnp.dot(lhs[lhs_of[p]][...].astype(BF16), caches[p][...], preferred_element_type=F32)

    o_ref[...] = epilogue(product, *[r[...] for r in extras]).astype(o_ref.dtype)


def _mm(pairs, extras, epilogue, n, out_dtype, name, tn=MM_TN, tm=MM_TM):
    m = pairs[0][0].shape[0]
    tm, tn = min(tm, m), min(tn, n)
    lhs, lhs_of, w_args, w_specs, scratch = [], [], [], [], []
    for a, w, layer, off in pairs:
        k = a.shape[1]
        if not any(a is seen for seen in lhs):
            lhs.append(a)
        lhs_of.append([a is seen for seen in lhs].index(True))
        w_args.append(w)
        w_specs.append(pl.BlockSpec((None, k, tn), lambda j, i, layer=layer, off=off: (layer, 0, j + off // tn)))
        scratch.append(pltpu.VMEM((k, tn), BF16))
    args = lhs + w_args
    specs = [pl.BlockSpec((tm, a.shape[1]), lambda j, i: (i, 0)) for a in lhs] + w_specs
    for arr, bshape, imap in extras:
        args.append(arr)
        specs.append(pl.BlockSpec(bshape(tm, tn), imap(tn)))
    return pl.pallas_call(
        functools.partial(_mm_kernel, lhs_of=tuple(lhs_of), n_lhs=len(lhs), n_extras=len(extras),
                          epilogue=epilogue),
        grid=(n // tn, m // tm), in_specs=specs,
        out_specs=pl.BlockSpec((tm, tn), lambda j, i: (i, j)),
        out_shape=jax.ShapeDtypeStruct((m, n), out_dtype),
        scratch_shapes=scratch,
        compiler_params=_cparams("arbitrary", "arbitrary"), name=name,
    )(*args)


def _tile(tm, tn):
    return (tm, tn)


def _rowvec(tm, tn):
    return (1, tn)


def _at_cols(off):
    return lambda tn: (lambda j, i: (i, j + off // tn))


def _vec_cols(off):
    return lambda tn: (lambda j, i: (0, j + off // tn))


def _attn_bias_fill(rpb_ref, b_ref, head0):
    n_dr, n_dc = 2 * WIN_ROWS - 1, 2 * WIN_COLS - 1
    qc = lax.broadcasted_iota(jnp.int32, (GRID_W, 2 * GRID_W), 0)
    kc = lax.broadcasted_iota(jnp.int32, (GRID_W, 2 * GRID_W), 1) & (GRID_W - 1)
    cs = jnp.clip(qc - WIN_COLS // 2, 0, GRID_W - WIN_COLS)
    inside = (kc >= cs) & (kc < cs + WIN_COLS)
    dc = jnp.clip(kc - qc, -(WIN_COLS - 1), WIN_COLS - 1) + (WIN_COLS - 1)
    for h in range(2):
        base = (head0 + h) * (n_dr * n_dc)
        for dr in range(n_dr):
            t = jnp.zeros((GRID_W, 2 * GRID_W), F32)
            for d in range(n_dc):
                t = jnp.where(dc == d, rpb_ref[base + dr * n_dc + d], t)
            t = jnp.where(inside, t, NEG_INF)
            for var in range(WIN_ROWS):
                j = dr - var
                if 0 <= j < WIN_ROWS:
                    c0 = j * GRID_W
                    lo = c0 % (2 * GRID_W)
                    b_ref[var, h * GRID_W:(h + 1) * GRID_W, c0:c0 + GRID_W] = t[:, lo:lo + GRID_W]


def _attn_kernel(rpb_ref, q_ref, k_ref, v_ref, o_ref, b_ref, s_ref, *, n_rows):
    rb = pl.program_id(1)
    n_q = q_ref.shape[0]
    wkeys = WIN_ROWS * GRID_W
    dh = NA_HEAD_DIM
    lane = lax.broadcasted_iota(jnp.int32, (GRID_W, 2 * dh), 1)
    first = lane < dh

    @pl.when(rb == 0)
    def _():
        _attn_bias_fill(rpb_ref, b_ref, 2 * pl.program_id(0))

    def window(i):
        r = rb * n_q + i
        rs = jnp.clip(r - WIN_ROWS // 2, 0, n_rows - WIN_ROWS)
        return rs, rs - r + (WIN_ROWS - 1)

    for i in range(n_q):
        rs, var = window(i)
        kw = k_ref[pl.ds(rs, WIN_ROWS)].reshape(wkeys, 2 * dh)
        q = q_ref[i] * jnp.asarray(dh ** -0.5, BF16)
        zero = jnp.zeros_like(q)
        q2 = jnp.concatenate([jnp.where(first, q, zero), jnp.where(first, zero, q)], axis=0)
        s = lax.dot_general(q2, kw, (((1,), (1,)), ((), ())), preferred_element_type=F32)
        b = b_ref[var]
        s_ref[i] = s + b
    for i in range(n_q):
        rs, _ = window(i)
        vw = v_ref[pl.ds(rs, WIN_ROWS)].reshape(wkeys, 2 * dh)
        s = s_ref[i]
        m = jnp.max(s, axis=-1, keepdims=True)
        p = jnp.exp(s - m)
        l = jnp.sum(p, axis=-1, keepdims=True)
        o = jnp.dot(p.astype(BF16), vw, preferred_element_type=F32) / l
        o_ref[i] = jnp.where(first, o[:GRID_W], o[GRID_W:]).astype(o_ref.dtype)


def _attention(proj, rpb):
    t, width = proj.shape
    n_rows = t // GRID_W
    pairs = NA_HEADS // 2
    p3 = proj.reshape(n_rows, GRID_W, width)
    blk = 2 * NA_HEAD_DIM
    rows = min(ATTN_ROWS, n_rows)
    wkeys = WIN_ROWS * GRID_W
    out = pl.pallas_call(
        functools.partial(_attn_kernel, n_rows=n_rows),
        grid_spec=pltpu.PrefetchScalarGridSpec(
            num_scalar_prefetch=1, grid=(pairs, n_rows // rows),
            in_specs=[
                pl.BlockSpec((rows, GRID_W, blk), lambda hp, rb, _: (rb, 0, hp)),
                pl.BlockSpec((n_rows, GRID_W, blk), lambda hp, rb, _: (0, 0, pairs + hp)),
                pl.BlockSpec((n_rows, GRID_W, blk), lambda hp, rb, _: (0, 0, 2 * pairs + hp)),
            ],
            out_specs=pl.BlockSpec((rows, GRID_W, blk), lambda hp, rb, _: (rb, 0, hp)),
            scratch_shapes=[pltpu.VMEM((WIN_ROWS, 2 * GRID_W, wkeys), F32),
                            pltpu.VMEM((rows, 2 * GRID_W, wkeys), F32)]),
        out_shape=jax.ShapeDtypeStruct((n_rows, GRID_W, NA_HEADS * NA_HEAD_DIM), BF16),
        compiler_params=_cparams("arbitrary", "arbitrary"), name="na_attention",
    )(rpb.reshape(-1), p3, p3, p3)
    return out.reshape(t, NA_HEADS * NA_HEAD_DIM)


def _dft_tables(n1, n2, dc):
    s = n1 * n2
    a = np.arange(dc, dtype=np.float64)
    ang_c = 2 * np.pi * np.outer(a, a) / dc
    w_c = np.concatenate([np.cos(ang_c), -np.sin(ang_c)], axis=1)
    a1 = np.arange(n1, dtype=np.float64)
    ang1 = 2 * np.pi * np.outer(a1, a1) / n1
    c1, s1 = np.cos(ang1), np.sin(ang1)
    m1 = np.block([[c1, s1], [-s1, c1]])
    a2 = np.arange(n2, dtype=np.float64)
    ang_t = 2 * np.pi * np.outer(a2, a1) / s
    tw_c = np.repeat(np.cos(ang_t)[:, :, None], LANES, axis=2)
    tw_s = np.repeat(np.sin(ang_t)[:, :, None], LANES, axis=2)
    ang2 = 2 * np.pi * np.outer(a2, a2) / n2
    m2 = np.concatenate([np.cos(ang2), np.sin(ang2)], axis=1)
    return (jnp.asarray(w_c, BF16), jnp.asarray(m1, BF16), jnp.asarray(tw_c, F32), jnp.asarray(tw_s, F32),
            jnp.asarray(m2, BF16))


def _fft1_kernel(u_hbm, wc_ref, m1_ref, tc_ref, ts_ref, br_hbm, bi_hbm,
                 ubuf, zz_ref, obuf, isem, osem, *, n1, dc):
    s = pl.program_id(0)
    ns = pl.num_programs(0)
    jb = ubuf.shape[1]
    slot = s % 2

    def in_copy(step, sl, jj):
        return pltpu.make_async_copy(u_hbm.at[:, step * jb + jj], ubuf.at[sl, jj], isem.at[sl])

    def out_copies(step, sl, jj):
        j = step * jb + jj
        return (pltpu.make_async_copy(obuf.at[sl, 0, jj], br_hbm.at[:, j], osem.at[sl]),
                pltpu.make_async_copy(obuf.at[sl, 1, jj], bi_hbm.at[:, j], osem.at[sl]))

    @pl.when(s == 0)
    def _():
        for jj in range(jb):
            in_copy(0, 0, jj).start()

    @pl.when(s + 1 < ns)
    def _():
        for jj in range(jb):
            in_copy(s + 1, 1 - slot, jj).start()

    for jj in range(jb):
        in_copy(s, slot, jj).wait()

    @pl.when(s >= 2)
    def _():
        for jj in range(jb):
            for cp in out_copies(s - 2, slot, jj):
                cp.wait()

    c = ubuf.shape[3]
    u = ubuf[slot].reshape(jb * n1, c).astype(BF16)
    for g in range(c // dc):
        z = jnp.dot(u[:, g * dc:(g + 1) * dc], wc_ref[...], preferred_element_type=F32)
        z = z.astype(BF16)
        for jj in range(jb):
            zz_ref[jj, :n1, g * dc:(g + 1) * dc] = z[jj * n1:(jj + 1) * n1, :dc]
            zz_ref[jj, n1:, g * dc:(g + 1) * dc] = z[jj * n1:(jj + 1) * n1, dc:]
    for jj in range(jb):
        a = jnp.dot(m1_ref[...], zz_ref[jj], preferred_element_type=F32)
        ar, ai = a[:n1], a[n1:]
        tc = jnp.concatenate([tc_ref[jj]] * (c // LANES), axis=1)
        ts = jnp.concatenate([ts_ref[jj]] * (c // LANES), axis=1)
        obuf[slot, 0, jj] = ar * tc + ai * ts
        obuf[slot, 1, jj] = ai * tc - ar * ts
    for jj in range(jb):
        for cp in out_copies(s, slot, jj):
            cp.start()

    @pl.when(s == ns - 1)
    def _():
        for jj in range(jb):
            for cp in out_copies(s, slot, jj):
                cp.wait()

        @pl.when(s >= 1)
        def _():
            for jj in range(jb):
                for cp in out_copies(s - 1, 1 - slot, jj):
                    cp.wait()


def _fft2_kernel(br_ref, bi_ref, m2_ref, y_hbm, obuf, osem, *, n2, scale):
    s = pl.program_id(0)
    ns = pl.num_programs(0)
    kb = obuf.shape[1]
    slot = s % 2

    def out_copy(step, sl, kk):
        return pltpu.make_async_copy(obuf.at[sl, kk], y_hbm.at[:, step * kb + kk], osem.at[sl])

    @pl.when(s >= 2)
    def _():
        for kk in range(kb):
            out_copy(s - 2, slot, kk).wait()

    for kk in range(kb):
        rows = slice(kk * n2, (kk + 1) * n2)
        bb = jnp.concatenate([br_ref[rows, :], bi_ref[rows, :]], axis=0).astype(BF16)
        obuf[slot, kk] = jnp.dot(m2_ref[...], bb, preferred_element_type=F32) * scale
    for kk in range(kb):
        out_copy(s, slot, kk).start()

    @pl.when(s == ns - 1)
    def _():
        for kk in range(kb):
            out_copy(s, slot, kk).wait()

        @pl.when(s >= 1)
        def _():
            for kk in range(kb):
                out_copy(s - 1, 1 - slot, kk).wait()


def _fourier(u):
    t, fw = u.shape
    n2 = GRID_W
    n1 = t // n2
    dc = F_GROUP_DIM
    w_c, m1, tw_c, tw_s, m2 = _dft_tables(n1, n2, dc)
    jb = min(FFT_JB, n2)
    kb = min(FFT_KB, n1)
    any_spec = pl.BlockSpec(memory_space=pl.ANY)
    tw_spec = pl.BlockSpec((jb, n1, LANES), lambda s: (s, 0, 0))
    b_re, b_im = pl.pallas_call(
        functools.partial(_fft1_kernel, n1=n1, dc=dc),
        grid=(n2 // jb,),
        in_specs=[
            any_spec,
            pl.BlockSpec((dc, 2 * dc), lambda s: (0, 0)),
            pl.BlockSpec((2 * n1, 2 * n1), lambda s: (0, 0)),
            tw_spec, tw_spec,
        ],
        out_specs=[any_spec, any_spec],
        out_shape=[jax.ShapeDtypeStruct((n1, n2, fw), F32)] * 2,
        scratch_shapes=[
            pltpu.VMEM((2, jb, n1, fw), F32),
            pltpu.VMEM((jb, 2 * n1, fw), BF16),
            pltpu.VMEM((2, 2, jb, n1, fw), F32),
            pltpu.SemaphoreType.DMA((2,)),
            pltpu.SemaphoreType.DMA((2,)),
        ],
        compiler_params=_cparams("arbitrary"), name="fourier_stage1",
    )(u.reshape(n1, n2, fw), w_c, m1, tw_c, tw_s)
    scale = 1.0 / math.sqrt(t * dc)
    in_blk = pl.BlockSpec((kb * n2, fw), lambda s: (s, 0))
    y = pl.pallas_call(
        functools.partial(_fft2_kernel, n2=n2, scale=scale),
        grid=(n1 // kb,),
        in_specs=[in_blk, in_blk, pl.BlockSpec((n2, 2 * n2), lambda s: (0, 0))],
        out_specs=any_spec,
        out_shape=jax.ShapeDtypeStruct((n2, n1, fw), F32),
        scratch_shapes=[pltpu.VMEM((2, kb, n2, fw), F32), pltpu.SemaphoreType.DMA((2,))],
        compiler_params=_cparams("arbitrary"), name="fourier_stage2",
    )(b_re.reshape(t, fw), b_im.reshape(t, fw), m2)
    return y.reshape(t, fw)


def _dispatch_kernel(dest_ref, pad0_ref, padn_ref, na_ref, h_ref, xb_hbm, zbuf, sem, zsem, *, chunks):
    i = pl.program_id(0)
    tm = h_ref.shape[0] // chunks
    tb = zbuf.shape[0] // chunks
    nb = xb_hbm.shape[0] // (tb * chunks)

    def tail_copies(fn):
        def body(b, carry):
            fn(pltpu.make_async_copy(zbuf, xb_hbm.at[_token_rows(b, tb * chunks)], zsem))
            return carry
        lax.fori_loop(na_ref[0], nb, body, 0)

    def pad_copies(e, fn):
        pos = pad0_ref[e]
        n = padn_ref[e]
        bit = tb // 2
        while bit >= 1:
            @pl.when((n & bit) != 0)
            def _(pos=pos, bit=bit):
                rows = pl.ds(pl.multiple_of(pos * chunks, chunks), bit * chunks)
                fn(pltpu.make_async_copy(zbuf.at[pl.ds(0, bit * chunks)], xb_hbm.at[rows], zsem))
            pos = pos + (n & bit)
            bit //= 2

    @pl.when(i == 0)
    def _():
        zbuf[...] = jnp.zeros_like(zbuf)
        lax.fori_loop(0, N_EXPERTS, lambda e, c: (pad_copies(e, lambda cp: cp.start()), c)[1], 0)
        tail_copies(lambda cp: cp.start())

    def row_copy(r, k):
        dst = dest_ref[TOP_K * (i * tm + r) + k]
        return pltpu.make_async_copy(h_ref.at[_token_rows(r, chunks)], xb_hbm.at[_token_rows(dst, chunks)], sem)

    def for_rows(fn):
        def body(r, carry):
            for k in range(TOP_K):
                fn(r, k)
            return carry
        lax.fori_loop(0, tm, body, 0, unroll=ROW_DMA_UNROLL)

    for_rows(lambda r, k: row_copy(r, k).start(priority=k))
    for_rows(lambda r, k: row_copy(r, k).wait())

    @pl.when(i == 0)
    def _():
        lax.fori_loop(0, N_EXPERTS, lambda e, c: (pad_copies(e, lambda cp: cp.wait()), c)[1], 0)
        tail_copies(lambda cp: cp.wait())


def _pack_bf16_pairs(h):
    half = h.shape[1] // 2
    bits = lax.bitcast_convert_type(h.astype(BF16).astype(F32), jnp.uint32)
    return bits[:, half:] | (bits[:, :half] >> 16)


def _unpack_bf16_pairs(p):
    lo = lax.bitcast_convert_type(p << 16, F32)
    hi = lax.bitcast_convert_type(p & jnp.uint32(0xFFFF0000), F32)
    return jnp.concatenate([lo, hi], axis=1)


def _expert_kernel(be_ref, bi_ref, first_ref, par_ref, nxt1_ref, nxt2_ref, na_ref,
                   x_ref, wg_hbm, wu_hbm, wd_hbm, y_ref,
                   wg_st, wu_st, wd_st, wg_c, wu_c, wd_c, wsem, *, layer):
    b = pl.program_id(0)
    active = b < na_ref[0]
    d = wg_c.shape[0]
    xchunks = d // LANES // 2
    tb = x_ref.shape[0] // xchunks

    def weight_copies(e, slot):
        return [pltpu.make_async_copy(src.at[layer, e], dst.at[slot], wsem.at[slot, n])
                for n, (src, dst) in enumerate(((wg_hbm, wg_st), (wu_hbm, wu_st), (wd_hbm, wd_st)))]

    @pl.when(b == 0)
    def _():
        for cp in weight_copies(be_ref[0], 0):
            cp.start()

        @pl.when(nxt1_ref[0] >= 0)
        def _():
            for cp in weight_copies(nxt1_ref[0], 1):
                cp.start()

    @pl.when(active & (first_ref[b] == 1))
    def _():
        slot = par_ref[b]
        for cp in weight_copies(be_ref[b], slot):
            cp.wait()
        wg_c[...] = wg_st[slot].astype(BF16)
        wu_c[...] = wu_st[slot].astype(BF16)
        wd_c[...] = wd_st[slot].astype(BF16)

        @pl.when(nxt2_ref[b] >= 0)
        def _():
            for cp in weight_copies(nxt2_ref[b], slot):
                cp.start()

    @pl.when(active)
    def _():
        x = _unpack_bf16_pairs(_load_token_major(x_ref, tb, xchunks)).astype(BF16)
        gate = jnp.dot(x, wg_c[...], preferred_element_type=F32)
        up = jnp.dot(x, wu_c[...], preferred_element_type=F32)
        act = (gate * _sigmoid(gate) * up).astype(BF16)
        _store_token_major(y_ref, _pack_bf16_pairs(jnp.dot(act, wd_c[...], preferred_element_type=F32)))

    @pl.when(jnp.logical_not(active))
    def _():
        y_ref[...] = jnp.zeros_like(y_ref)


def _moe_plan(experts, t, tb):
    a = t * TOP_K
    e_flat = experts.reshape(a)
    lanes = jnp.arange(N_EXPERTS, dtype=jnp.int32)[None, :]
    onehot = (e_flat[:, None] == lanes).astype(jnp.int32)
    csum = jnp.cumsum(onehot, axis=0)
    counts = csum[-1]
    pcounts = (counts + tb - 1) // tb * tb
    pends = jnp.cumsum(pcounts)
    pstarts = pends - pcounts
    dest = jnp.sum(onehot * (pstarts[None, :] + csum - 1), axis=1).astype(jnp.int32)
    nb = (a + N_EXPERTS * tb) // tb
    n_act = (pends[-1] // tb).astype(jnp.int32)
    blk = jnp.arange(nb, dtype=jnp.int32)
    blk_idx = jnp.minimum(blk, n_act - 1)
    be = jnp.sum((pends[None, :] <= (blk_idx * tb)[:, None]).astype(jnp.int32), axis=1)
    be = jnp.minimum(be, N_EXPERTS - 1).astype(jnp.int32)
    first = ((blk * tb == pstarts[be]) & (blk < n_act)).astype(jnp.int32)
    parity = ((jnp.cumsum(first) - 1) & 1).astype(jnp.int32)

    def next_expert(e):
        after = pends[jnp.maximum(e, 0)] // tb
        return jnp.where((e >= 0) & (after < n_act), be[jnp.minimum(after, nb - 1)], -1).astype(jnp.int32)

    nxt1 = next_expert(be)
    nxt2 = next_expert(nxt1)
    plan = dict(dest=dest, be=be, blk_idx=blk_idx, first=first, parity=parity, nxt1=nxt1, nxt2=nxt2,
                n_act=n_act.reshape(1),
                pad0=(pstarts + counts).astype(jnp.int32), padn=(pcounts - counts).astype(jnp.int32))
    return plan


def _experts(h, experts, layer, w_gate, w_up, w_down):
    d, de = w_gate.shape[2:]
    chunks = d // LANES
    xchunks = chunks // 2
    t = h.shape[0] // xchunks
    tb = EXPERT_TB
    tm = min(PRE_TM, t)
    plan = _moe_plan(experts, t, tb)
    nb = plan["be"].shape[0]
    p = nb * tb
    xb = pl.pallas_call(
        functools.partial(_dispatch_kernel, chunks=xchunks),
        grid_spec=pltpu.PrefetchScalarGridSpec(
            num_scalar_prefetch=4, grid=(t // tm,),
            in_specs=[pl.BlockSpec((tm * xchunks, LANES), lambda i, *_: (i, 0))],
            out_specs=pl.BlockSpec(memory_space=pl.ANY),
            scratch_shapes=[pltpu.VMEM((tb * xchunks, LANES), h.dtype),
                            pltpu.SemaphoreType.DMA(()), pltpu.SemaphoreType.DMA(())]),
        out_shape=jax.ShapeDtypeStruct((p * xchunks, LANES), h.dtype),
        compiler_params=_cparams("arbitrary"), name="moe_dispatch",
    )(plan["dest"], plan["pad0"], plan["padn"], plan["n_act"], h)
    hbm = pl.BlockSpec(memory_space=pl.ANY)
    yb = pl.pallas_call(
        functools.partial(_expert_kernel, layer=layer),
        grid_spec=pltpu.PrefetchScalarGridSpec(
            num_scalar_prefetch=7, grid=(nb,),
            in_specs=[pl.BlockSpec((tb * xchunks, LANES), lambda b, be, bi, *_: (bi[b], 0)), hbm, hbm, hbm],
            out_specs=pl.BlockSpec((tb * xchunks, LANES), lambda b, *_: (b, 0)),
            scratch_shapes=[pltpu.VMEM((2, d, de), F32), pltpu.VMEM((2, d, de), F32), pltpu.VMEM((2, de, d), F32),
                            pltpu.VMEM((d, de), BF16), pltpu.VMEM((d, de), BF16), pltpu.VMEM((de, d), BF16),
                            pltpu.SemaphoreType.DMA((2, 3))]),
        out_shape=jax.ShapeDtypeStruct((p * xchunks, LANES), jnp.uint32),
        compiler_params=_cparams("arbitrary"), name="moe_experts",
    )(plan["be"], plan["blk_idx"], plan["first"], plan["parity"], plan["nxt1"], plan["nxt2"], plan["n_act"],
      xb, w_gate, w_up, w_down)
    return yb, plan["dest"]


def kernel(x, c, norm1_g, norm2_g, ada_w, ada_b, mix_in_w, na_rpb, na_out_w, fourier_out_w, branch_gate_w,
           branch_gate_b, mix_out_w, router_group_w, router_group_b, router_expert_w, router_expert_b,
           expert_w_gate, expert_w_up, expert_w_down, final_g):
    bsz, s, d = x.shape
    assert bsz == 1 and s % GRID_W == 0
    depth = ada_w.shape[0]
    na_w = NA_HEADS * NA_HEAD_DIM
    f_w = F_GROUPS * F_GROUP_DIM
    xs = x.reshape(s, d)
    mod = _ada_all(c, ada_w, ada_b)
    pad = ROUTER_LANES - N_GROUPS - N_EXPERTS
    wr_all = jnp.concatenate([router_group_w, router_expert_w, jnp.zeros((depth, d, pad), F32)], axis=2)
    br_all = jnp.concatenate([router_group_b, router_expert_b, jnp.zeros((depth, pad), F32)], axis=1)
    moe = g2 = None
    for l in range(depth):
        sh1, sc1, g1, sh2, sc2, g2_l = [mod[l, :, i * d:(i + 1) * d] for i in range(N_MOD)]
        if moe is None:
            (h,) = _pre(xs, norm1_g[l], shift=sh1, scale=sc1)
        else:
            xs, h = _pre(xs, norm1_g[l], moe=moe, g2=g2, shift=sh1, scale=sc1)
        qkv = _mm([(h, mix_in_w, l, 0)], [], lambda prod: prod(0), 3 * na_w, BF16, "mix_in_qkv", tn=MM_TN_WIDE)
        u = _mm([(h, mix_in_w, l, 3 * na_w)], [], lambda prod: prod(0), f_w, F32, "mix_in_fourier", tn=MM_TN_WIDE)
        y_att = _attention(qkv, na_rpb[l])
        y_fft = _fourier(u)
        bias_bg = branch_gate_b[l].reshape(1, -1)
        mixed = _mm([(h, branch_gate_w, l, 0), (h, branch_gate_w, l, d),
                     (y_att, na_out_w, l, 0), (y_fft, fourier_out_w, l, 0)],
                    [(bias_bg, _rowvec, _vec_cols(0)), (bias_bg, _rowvec, _vec_cols(d))],
                    lambda prod, ba, bf: (_sigmoid(prod(0) + ba) * prod(2) + _sigmoid(prod(1) + bf) * prod(3)),
                    d, BF16, "gated_branch_mix", tn=MM_TN, tm=MM_TM_FUSED)
        xs = _mm([(mixed, mix_out_w, l, 0)],
                 [(xs, _tile, _at_cols(0)), (g1, _rowvec, _vec_cols(0))],
                 lambda prod, xr, g: xr + g * prod(0), d, F32, "mix_out", tn=MM_TN_WIDE)
        h2, wts, ids = _pre(xs, norm2_g[l], shift=sh2, scale=sc2, wr=wr_all[l], br=br_all[l].reshape(1, -1),
                            h_dtype=F32)
        yb, dest = _experts(h2, ids[:, :TOP_K], l, expert_w_gate, expert_w_up, expert_w_down)
        moe, g2 = (yb, dest, wts), g2_l
    _, out = _pre(xs, final_g, moe=moe, g2=g2, h_dtype=F32)
    return out.reshape(bsz, s, d)
```

```python
import functools
import math

import numpy as np
import jax
import jax.numpy as jnp
from jax import lax
from jax.experimental import pallas as pl
from jax.experimental.pallas import tpu as pltpu

F32 = jnp.float32
BF16 = jnp.bfloat16

GRID_W = 64
NA_HEADS = 16
NA_HEAD_DIM = 64
WIN_ROWS = 8
WIN_COLS = 16
F_GROUPS = 4
F_GROUP_DIM = 256
N_GROUPS = 4
EXPERTS_PER_GROUP = 8
N_EXPERTS = N_GROUPS * EXPERTS_PER_GROUP
TOP_K = 2
N_MOD = 6
EPS = 1e-6
NEG_INF = -1e30

LANES = 128
SUBLANES = 8

ADA_TN = 1536
ADA_ROWS = 256
PRE_TM = 256
MM_TM = 1024
MM_TM_FUSED = 512
MM_TN = 512
MM_TN_WIDE = 1024
ATTN_ROWS = 16
EXPERT_TB = 256
ROW_DMA_UNROLL = 8
FFT_JB = 8
FFT_KB = 16
ROUTER_LANES = LANES

VMEM_LIMIT = 56 * 1024 * 1024


def _cparams(*sem):
    return pltpu.CompilerParams(dimension_semantics=sem, vmem_limit_bytes=VMEM_LIMIT)


def _sigmoid(x):
    return 1.0 / (1.0 + jnp.exp(-x))


def _ada_kernel(c_ref, w_ref, b_ref, o_ref, cs_ref):
    c = c_ref[...]
    cs_ref[...] = c * _sigmoid(c)
    d, tn = w_ref.shape

    def body(i, acc):
        r0 = pl.multiple_of(i * ADA_ROWS, ADA_ROWS)
        prod = w_ref[pl.ds(r0, ADA_ROWS), :] * cs_ref[pl.ds(r0, ADA_ROWS), :]
        return acc + jnp.sum(prod.reshape(ADA_ROWS // SUBLANES, SUBLANES, tn), axis=0)

    acc = lax.fori_loop(0, d // ADA_ROWS, body, jnp.zeros((SUBLANES, tn), F32))
    o_ref[...] = jnp.sum(acc, axis=0, keepdims=True) + b_ref[...]


def _ada_all(c, ada_w, ada_b):
    nl, d, n = ada_w.shape
    tn = min(ADA_TN, n)
    return pl.pallas_call(
        _ada_kernel,
        grid=(nl, n // tn),
        in_specs=[
            pl.BlockSpec((d, 1), lambda l, j: (0, 0)),
            pl.BlockSpec((None, d, tn), lambda l, j: (l, 0, j)),
            pl.BlockSpec((None, 1, tn), lambda l, j: (l, 0, j)),
        ],
        out_specs=pl.BlockSpec((None, 1, tn), lambda l, j: (l, 0, j)),
        out_shape=jax.ShapeDtypeStruct((nl, 1, n), F32),
        scratch_shapes=[pltpu.VMEM((d, 1), F32)],
        compiler_params=_cparams("parallel", "parallel"),
        name="ada_proj",
    )(c.reshape(d, 1), ada_w, ada_b.reshape(nl, 1, n))


def _rms(x, g):
    return x * lax.rsqrt(jnp.mean(x * x, axis=-1, keepdims=True) + EPS) * g


def _router(h, wr_ref, br_ref):
    w_split = wr_ref[...]
    h_hi = h.astype(BF16)
    h_lo = (h - h_hi.astype(F32)).astype(BF16)
    dot = functools.partial(jnp.dot, preferred_element_type=F32)
    both = dot(h_hi, w_split)
    logits = both[:, :ROUTER_LANES] + (both[:, ROUTER_LANES:] + dot(h_lo, w_split[:, :ROUTER_LANES])) + br_ref[...]
    lane = lax.broadcasted_iota(jnp.int32, logits.shape, 1)
    big = jnp.int32(ROUTER_LANES)
    is_g = lane < N_GROUPS
    gl = jnp.where(is_g, logits, NEG_INF)
    gmax = jnp.max(gl, axis=-1, keepdims=True)
    gsum = jnp.sum(jnp.where(is_g, jnp.exp(gl - gmax), 0.0), axis=-1, keepdims=True)
    g_top_p = 1.0 / gsum
    g_top = jnp.min(jnp.where(is_g & (gl == gmax), lane, big), axis=-1, keepdims=True)
    e_lane = lane - N_GROUPS
    sel = (e_lane >= 0) & (e_lane < N_EXPERTS) & ((e_lane >> 3) == g_top)
    l1 = jnp.where(sel, logits, NEG_INF)
    m1 = jnp.max(l1, axis=-1, keepdims=True)
    i1 = jnp.min(jnp.where(sel & (l1 == m1), lane, big), axis=-1, keepdims=True)
    sel2 = sel & (lane != i1)
    l2 = jnp.where(sel2, logits, NEG_INF)
    m2 = jnp.max(l2, axis=-1, keepdims=True)
    i2 = jnp.min(jnp.where(sel2 & (l2 == m2), lane, big), axis=-1, keepdims=True)
    e21 = jnp.exp(m2 - m1)
    p1 = 1.0 / (1.0 + e21)
    p2 = e21 / (1.0 + e21)
    wts = jnp.where(lane == 0, g_top_p * p1, jnp.where(lane == 1, g_top_p * p2, 0.0))
    ids = jnp.where(lane == 0, i1 - N_GROUPS, jnp.where(lane == 1, i2 - N_GROUPS, 0))
    return wts, ids


def _load_token_major(ref, n_tok, chunks):
    return jnp.concatenate([ref[pl.ds(c, n_tok, stride=chunks), :] for c in range(chunks)], axis=1)


def _store_token_major(ref, val):
    n_tok, d = val.shape
    chunks = d // LANES
    for c in range(chunks):
        ref[pl.ds(c, n_tok, stride=chunks), :] = val[:, c * LANES:(c + 1) * LANES]


def _token_rows(tok, chunks):
    return pl.ds(pl.multiple_of(tok * chunks, chunks), chunks)


def _pre_kernel(*refs, combine, modulate, router, emit_h, h_dtype):
    it = iter(refs)
    if combine:
        dest_ref = next(it)
    x_ref = next(it)
    if combine:
        y_hbm, cw_ref, g2_ref = next(it), next(it), next(it)
    g_ref = next(it)
    if modulate:
        sh_ref, sc_ref = next(it), next(it)
    if router:
        wr_ref, br_ref = next(it), next(it)
    outs = [next(it) for _ in range(int(combine) + int(emit_h) + 2 * int(router))]
    x = x_ref[...]
    if combine:
        gbuf, gsem = next(it), next(it)
        i = pl.program_id(0)
        slot = i % 2
        tm, d = x_ref.shape
        chunks = d // LANES // 2

        def row_copy(step, sl, r, k):
            src = dest_ref[TOP_K * (step * tm + r) + k]
            return pltpu.make_async_copy(y_hbm.at[_token_rows(src, chunks)],
                                         gbuf.at[sl, k, _token_rows(r, chunks)], gsem.at[sl])

        def for_rows(fn):
            def body(r, carry):
                for k in range(TOP_K):
                    fn(r, k)
                return carry
            lax.fori_loop(0, tm, body, 0, unroll=ROW_DMA_UNROLL)

        @pl.when(i == 0)
        def _():
            for_rows(lambda r, k: row_copy(0, 0, r, k).start(priority=k))

        @pl.when(i + 1 < pl.num_programs(0))
        def _():
            for_rows(lambda r, k: row_copy(i + 1, 1 - slot, r, k).start(priority=k))

        for_rows(lambda r, k: row_copy(i, slot, r, k).wait())
        cw = cw_ref[...]
        y0 = _unpack_bf16_pairs(_load_token_major(gbuf.at[slot, 0], tm, chunks))
        y1 = _unpack_bf16_pairs(_load_token_major(gbuf.at[slot, 1], tm, chunks))
        x = x + g2_ref[...] * (cw[:, 0:1] * y0 + cw[:, 1:2] * y1)
        outs.pop(0)[...] = x
    h = _rms(x, g_ref[...])
    if modulate:
        h = h * (1.0 + sc_ref[...]) + sh_ref[...]
    if emit_h and router:
        _store_token_major(outs.pop(0), _pack_bf16_pairs(h))
    elif emit_h:
        outs.pop(0)[...] = h.astype(h_dtype)
    if router:
        wts, ids = _router(h, wr_ref, br_ref)
        outs.pop(0)[...] = wts
        outs.pop(0)[...] = ids


def _pre(x, g, *, moe=None, g2=None, shift=None, scale=None, wr=None, br=None, h_dtype=BF16):
    t, d = x.shape
    tm = min(PRE_TM, t)
    chunks = d // LANES
    combine, modulate, router = moe is not None, shift is not None, wr is not None
    row = pl.BlockSpec((tm, d), lambda i, *_: (i, 0))
    vec = pl.BlockSpec((1, d), lambda i, *_: (0, 0))
    lane_row = pl.BlockSpec((tm, ROUTER_LANES), lambda i, *_: (i, 0))
    prefetch, args, specs, scratch = [], [x], [row], []
    if combine:
        yb, dest, cw = moe
        prefetch.append(dest)
        args += [yb, cw, g2]
        specs += [pl.BlockSpec(memory_space=pl.ANY), lane_row, vec]
        scratch = [pltpu.VMEM((2, TOP_K, tm * chunks // 2, LANES), jnp.uint32), pltpu.SemaphoreType.DMA((2,))]
    args.append(g.reshape(1, d))
    specs.append(vec)
    if modulate:
        args += [shift, scale]
        specs += [vec, vec]
    if router:
        args += [wr, br]
        specs += [pl.BlockSpec(wr.shape, lambda i, *_: (0, 0)), pl.BlockSpec(br.shape, lambda i, *_: (0, 0))]
    out_shape, out_specs = [], []
    if combine:
        out_shape.append(jax.ShapeDtypeStruct((t, d), F32))
        out_specs.append(row)
    if router:
        out_shape.append(jax.ShapeDtypeStruct((t * chunks // 2, LANES), jnp.uint32))
        out_specs.append(pl.BlockSpec((tm * chunks // 2, LANES), lambda i, *_: (i, 0)))
    else:
        out_shape.append(jax.ShapeDtypeStruct((t, d), h_dtype))
        out_specs.append(row)
    if router:
        out_shape += [jax.ShapeDtypeStruct((t, ROUTER_LANES), F32), jax.ShapeDtypeStruct((t, ROUTER_LANES), jnp.int32)]
        out_specs += [lane_row, lane_row]
    kern = functools.partial(_pre_kernel, combine=combine, modulate=modulate, router=router,
                             emit_h=True, h_dtype=h_dtype)
    grid_spec = pltpu.PrefetchScalarGridSpec(
        num_scalar_prefetch=len(prefetch), grid=(t // tm,), in_specs=specs, out_specs=out_specs,
        scratch_shapes=scratch)
    return pl.pallas_call(
        kern, grid_spec=grid_spec, out_shape=out_shape,
        compiler_params=_cparams("arbitrary" if combine else "parallel"), name="pre_norm",
    )(*prefetch, *args)


def _mm_kernel(*refs, lhs_of, n_lhs, n_extras, epilogue):
    n_pairs = len(lhs_of)
    lhs = refs[:n_lhs]
    weights = refs[n_lhs:n_lhs + n_pairs]
    extras = refs[n_lhs + n_pairs:n_lhs + n_pairs + n_extras]
    o_ref = refs[n_lhs + n_pairs + n_extras]
    caches = refs[n_lhs + n_pairs + n_extras + 1:]

    @pl.when(pl.program_id(1) == 0)
    def _():
        for p in range(n_pairs):
            caches[p][...] = weights[p][...].astype(BF16)

    def product(p):
        a = lhs[lhs_of[p]][...]
        a = _unpack_bf16_pairs(a) if a.dtype == jnp.uint32 else a
        return jnp.dot(a.astype(BF16), caches[p][...], preferred_element_type=F32)

    out = epilogue(product, *[r[...] for r in extras])
    o_ref[...] = _pack_bf16_pairs(out) if o_ref.dtype == jnp.uint32 else out.astype(o_ref.dtype)


def _mm(pairs, extras, epilogue, n, out_dtype, name, tn=MM_TN, tm=MM_TM):
    m = pairs[0][0].shape[0]
    tm, tn = min(tm, m), min(tn, n)
    packed_out = out_dtype == jnp.uint32
    assert not packed_out or tn == n
    lhs, lhs_of, w_args, w_specs, scratch = [], [], [], [], []
    for a, w, layer, off in pairs:
        k = a.shape[1] * (2 if a.dtype == jnp.uint32 else 1)
        if not any(a is seen for seen in lhs):
            lhs.append(a)
        lhs_of.append([a is seen for seen in lhs].index(True))
        w_args.append(w)
        w_specs.append(pl.BlockSpec((None, k, tn), lambda j, i, layer=layer, off=off: (layer, 0, j + off // tn)))
        scratch.append(pltpu.VMEM((k, tn), BF16))
    args = lhs + w_args
    specs = [pl.BlockSpec((tm, a.shape[1]), lambda j, i: (i, 0)) for a in lhs] + w_specs
    for arr, bshape, imap in extras:
        args.append(arr)
        specs.append(pl.BlockSpec(bshape(tm, tn), imap(tn)))
    return pl.pallas_call(
        functools.partial(_mm_kernel, lhs_of=tuple(lhs_of), n_lhs=len(lhs), n_extras=len(extras),
                          epilogue=epilogue),
        grid=(n // tn, m // tm), in_specs=specs,
        out_specs=pl.BlockSpec((tm, tn // 2 if packed_out else tn), lambda j, i: (i, j)),
        out_shape=jax.ShapeDtypeStruct((m, n // 2 if packed_out else n), out_dtype),
        scratch_shapes=scratch,
        compiler_params=_cparams("arbitrary", "arbitrary"), name=name,
    )(*args)


def _tile(tm, tn):
    return (tm, tn)


def _rowvec(tm, tn):
    return (1, tn)


def _at_cols(off):
    return lambda tn: (lambda j, i: (i, j + off // tn))


def _vec_cols(off):
    return lambda tn: (lambda j, i: (0, j + off // tn))


def _attn_bias_fill(rpb_ref, b_ref, head0):
    n_dr, n_dc = 2 * WIN_ROWS - 1, 2 * WIN_COLS - 1
    qc = lax.broadcasted_iota(jnp.int32, (GRID_W, 2 * GRID_W), 0)
    kc = lax.broadcasted_iota(jnp.int32, (GRID_W, 2 * GRID_W), 1) & (GRID_W - 1)
    cs = jnp.clip(qc - WIN_COLS // 2, 0, GRID_W - WIN_COLS)
    inside = (kc >= cs) & (kc < cs + WIN_COLS)
    dc = jnp.clip(kc - qc, -(WIN_COLS - 1), WIN_COLS - 1) + (WIN_COLS - 1)
    for h in range(2):
        base = (head0 + h) * (n_dr * n_dc)
        for dr in range(n_dr):
            t = jnp.zeros((GRID_W, 2 * GRID_W), F32)
            for d in range(n_dc):
                t = jnp.where(dc == d, rpb_ref[base + dr * n_dc + d], t)
            t = jnp.where(inside, t, NEG_INF)
            for var in range(WIN_ROWS):
                j = dr - var
                if 0 <= j < WIN_ROWS:
                    c0 = j * GRID_W
                    lo = c0 % (2 * GRID_W)
                    b_ref[var, h * GRID_W:(h + 1) * GRID_W, c0:c0 + GRID_W] = t[:, lo:lo + GRID_W]


def _attn_kernel(rpb_ref, q_ref, k_ref, v_ref, o_ref, b_ref, s_ref, *, n_rows):
    rb = pl.program_id(1)
    n_q = q_ref.shape[0]
    wkeys = WIN_ROWS * GRID_W
    dh = NA_HEAD_DIM
    lane = lax.broadcasted_iota(jnp.int32, (GRID_W, 2 * dh), 1)
    first = lane < dh

    @pl.when(rb == 0)
    def _():
        _attn_bias_fill(rpb_ref, b_ref, 2 * pl.program_id(0))

    def window(i):
        r = rb * n_q + i
        rs = jnp.clip(r - WIN_ROWS // 2, 0, n_rows - WIN_ROWS)
        return rs, rs - r + (WIN_ROWS - 1)

    for i in range(n_q):
        rs, var = window(i)
        kw = k_ref[pl.ds(rs, WIN_ROWS)].reshape(wkeys, 2 * dh)
        q = q_ref[i] * jnp.asarray(dh ** -0.5, BF16)
        zero = jnp.zeros_like(q)
        q2 = jnp.concatenate([jnp.where(first, q, zero), jnp.where(first, zero, q)], axis=0)
        s = lax.dot_general(q2, kw, (((1,), (1,)), ((), ())), preferred_element_type=F32)
        b = b_ref[var]
        s_ref[i] = s + b
    for i in range(n_q):
        rs, _ = window(i)
        vw = v_ref[pl.ds(rs, WIN_ROWS)].reshape(wkeys, 2 * dh)
        s = s_ref[i]
        m = jnp.max(s, axis=-1, keepdims=True)
        p = jnp.exp(s - m)
        l = jnp.sum(p, axis=-1, keepdims=True)
        o = jnp.dot(p.astype(BF16), vw, preferred_element_type=F32) / l
        o_ref[i] = jnp.where(first, o[:GRID_W], o[GRID_W:]).astype(o_ref.dtype)


def _attention(proj, rpb):
    t, width = proj.shape
    n_rows = t // GRID_W
    pairs = NA_HEADS // 2
    p3 = proj.reshape(n_rows, GRID_W, width)
    blk = 2 * NA_HEAD_DIM
    rows = min(ATTN_ROWS, n_rows)
    wkeys = WIN_ROWS * GRID_W
    out = pl.pallas_call(
        functools.partial(_attn_kernel, n_rows=n_rows),
        grid_spec=pltpu.PrefetchScalarGridSpec(
            num_scalar_prefetch=1, grid=(pairs, n_rows // rows),
            in_specs=[
                pl.BlockSpec((rows, GRID_W, blk), lambda hp, rb, _: (rb, 0, hp)),
                pl.BlockSpec((n_rows, GRID_W, blk), lambda hp, rb, _: (0, 0, pairs + hp)),
                pl.BlockSpec((n_rows, GRID_W, blk), lambda hp, rb, _: (0, 0, 2 * pairs + hp)),
            ],
            out_specs=pl.BlockSpec((rows, GRID_W, blk), lambda hp, rb, _: (rb, 0, hp)),
            scratch_shapes=[pltpu.VMEM((WIN_ROWS, 2 * GRID_W, wkeys), F32),
                            pltpu.VMEM((rows, 2 * GRID_W, wkeys), F32)]),
        out_shape=jax.ShapeDtypeStruct((n_rows, GRID_W, NA_HEADS * NA_HEAD_DIM), BF16),
        compiler_params=_cparams("arbitrary", "arbitrary"), name="na_attention",
    )(rpb.reshape(-1), p3, p3, p3)
    return out.reshape(t, NA_HEADS * NA_HEAD_DIM)


def _dft_tables(n1, n2, dc):
    s = n1 * n2
    a = np.arange(dc, dtype=np.float64)
    ang_c = 2 * np.pi * np.outer(a, a) / dc
    w_c = np.concatenate([np.cos(ang_c), -np.sin(ang_c)], axis=1)
    a1 = np.arange(n1, dtype=np.float64)
    ang1 = 2 * np.pi * np.outer(a1, a1) / n1
    c1, s1 = np.cos(ang1), np.sin(ang1)
    m1 = np.block([[c1, s1], [-s1, c1]])
    a2 = np.arange(n2, dtype=np.float64)
    ang_t = 2 * np.pi * np.outer(a2, a1) / s
    tw_c = np.repeat(np.cos(ang_t)[:, :, None], LANES, axis=2)
    tw_s = np.repeat(np.sin(ang_t)[:, :, None], LANES, axis=2)
    ang2 = 2 * np.pi * np.outer(a2, a2) / n2
    m2 = np.concatenate([np.cos(ang2), np.sin(ang2)], axis=1)
    return (jnp.asarray(w_c, BF16), jnp.asarray(m1, BF16), jnp.asarray(tw_c, F32), jnp.asarray(tw_s, F32),
            jnp.asarray(m2, BF16))


def _fft1_kernel(u_hbm, wc_ref, m1_ref, tc_ref, ts_ref, br_hbm, bi_hbm,
                 ubuf, zz_ref, obuf, isem, osem, *, n1, dc):
    s = pl.program_id(0)
    ns = pl.num_programs(0)
    jb = ubuf.shape[1]
    slot = s % 2

    def in_copy(step, sl, jj):
        return pltpu.make_async_copy(u_hbm.at[:, step * jb + jj], ubuf.at[sl, jj], isem.at[sl])

    def out_copies(step, sl, jj):
        j = step * jb + jj
        return (pltpu.make_async_copy(obuf.at[sl, 0, jj], br_hbm.at[:, j], osem.at[sl]),
                pltpu.make_async_copy(obuf.at[sl, 1, jj], bi_hbm.at[:, j], osem.at[sl]))

    @pl.when(s == 0)
    def _():
        for jj in range(jb):
            in_copy(0, 0, jj).start()

    @pl.when(s + 1 < ns)
    def _():
        for jj in range(jb):
            in_copy(s + 1, 1 - slot, jj).start()

    for jj in range(jb):
        in_copy(s, slot, jj).wait()

    @pl.when(s >= 2)
    def _():
        for jj in range(jb):
            for cp in out_copies(s - 2, slot, jj):
                cp.wait()

    c = 2 * ubuf.shape[3]
    u = _unpack_bf16_pairs(ubuf[slot].reshape(jb * n1, c // 2)).astype(BF16)
    for g in range(c // dc):
        z = jnp.dot(u[:, g * dc:(g + 1) * dc], wc_ref[...], preferred_element_type=F32)
        z = z.astype(BF16)
        for jj in range(jb):
            zz_ref[jj, :n1, g * dc:(g + 1) * dc] = z[jj * n1:(jj + 1) * n1, :dc]
            zz_ref[jj, n1:, g * dc:(g + 1) * dc] = z[jj * n1:(jj + 1) * n1, dc:]
    for jj in range(jb):
        a = jnp.dot(m1_ref[...], zz_ref[jj], preferred_element_type=F32)
        ar, ai = a[:n1], a[n1:]
        tc = jnp.concatenate([tc_ref[jj]] * (c // LANES), axis=1)
        ts = jnp.concatenate([ts_ref[jj]] * (c // LANES), axis=1)
        obuf[slot, 0, jj] = _pack_bf16_pairs(ar * tc + ai * ts)
        obuf[slot, 1, jj] = _pack_bf16_pairs(ai * tc - ar * ts)
    for jj in range(jb):
        for cp in out_copies(s, slot, jj):
            cp.start()

    @pl.when(s == ns - 1)
    def _():
        for jj in range(jb):
            for cp in out_copies(s, slot, jj):
                cp.wait()

        @pl.when(s >= 1)
        def _():
            for jj in range(jb):
                for cp in out_copies(s - 1, 1 - slot, jj):
                    cp.wait()


def _fft2_kernel(br_ref, bi_ref, m2_ref, y_hbm, obuf, osem, *, n2, scale):
    s = pl.program_id(0)
    ns = pl.num_programs(0)
    kb = obuf.shape[1]
    slot = s % 2

    def out_copy(step, sl, kk):
        return pltpu.make_async_copy(obuf.at[sl, kk], y_hbm.at[:, step * kb + kk], osem.at[sl])

    @pl.when(s >= 2)
    def _():
        for kk in range(kb):
            out_copy(s - 2, slot, kk).wait()

    for kk in range(kb):
        rows = slice(kk * n2, (kk + 1) * n2)
        bb = _unpack_bf16_pairs(jnp.concatenate([br_ref[rows, :], bi_ref[rows, :]], axis=0)).astype(BF16)
        obuf[slot, kk] = _pack_bf16_pairs(jnp.dot(m2_ref[...], bb, preferred_element_type=F32) * scale)
    for kk in range(kb):
        out_copy(s, slot, kk).start()

    @pl.when(s == ns - 1)
    def _():
        for kk in range(kb):
            out_copy(s, slot, kk).wait()

        @pl.when(s >= 1)
        def _():
            for kk in range(kb):
                out_copy(s - 1, 1 - slot, kk).wait()


def _fourier(u):
    t, hw = u.shape
    fw = 2 * hw
    n2 = GRID_W
    n1 = t // n2
    dc = F_GROUP_DIM
    w_c, m1, tw_c, tw_s, m2 = _dft_tables(n1, n2, dc)
    jb = min(FFT_JB, n2)
    kb = min(FFT_KB, n1)
    any_spec = pl.BlockSpec(memory_space=pl.ANY)
    tw_spec = pl.BlockSpec((jb, n1, LANES), lambda s: (s, 0, 0))
    b_re, b_im = pl.pallas_call(
        functools.partial(_fft1_kernel, n1=n1, dc=dc),
        grid=(n2 // jb,),
        in_specs=[
            any_spec,
            pl.BlockSpec((dc, 2 * dc), lambda s: (0, 0)),
            pl.BlockSpec((2 * n1, 2 * n1), lambda s: (0, 0)),
            tw_spec, tw_spec,
        ],
        out_specs=[any_spec, any_spec],
        out_shape=[jax.ShapeDtypeStruct((n1, n2, hw), jnp.uint32)] * 2,
        scratch_shapes=[
            pltpu.VMEM((2, jb, n1, hw), jnp.uint32),
            pltpu.VMEM((jb, 2 * n1, fw), BF16),
            pltpu.VMEM((2, 2, jb, n1, hw), jnp.uint32),
            pltpu.SemaphoreType.DMA((2,)),
            pltpu.SemaphoreType.DMA((2,)),
        ],
        compiler_params=_cparams("arbitrary"), name="fourier_stage1",
    )(u.reshape(n1, n2, hw), w_c, m1, tw_c, tw_s)
    scale = 1.0 / math.sqrt(t * dc)
    in_blk = pl.BlockSpec((kb * n2, hw), lambda s: (s, 0))
    y = pl.pallas_call(
        functools.partial(_fft2_kernel, n2=n2, scale=scale),
        grid=(n1 // kb,),
        in_specs=[in_blk, in_blk, pl.BlockSpec((n2, 2 * n2), lambda s: (0, 0))],
        out_specs=any_spec,
        out_shape=jax.ShapeDtypeStruct((n2, n1, hw), jnp.uint32),
        scratch_shapes=[pltpu.VMEM((2, kb, n2, hw), jnp.uint32), pltpu.SemaphoreType.DMA((2,))],
        compiler_params=_cparams("arbitrary"), name="fourier_stage2",
    )(b_re.reshape(t, hw), b_im.reshape(t, hw), m2)
    return y.reshape(t, hw)


def _dispatch_kernel(dest_ref, pad0_ref, padn_ref, na_ref, h_ref, xb_hbm, zbuf, sem, zsem, *, chunks):
    i = pl.program_id(0)
    tm = h_ref.shape[0] // chunks
    tb = zbuf.shape[0] // chunks
    nb = xb_hbm.shape[0] // (tb * chunks)

    def tail_copies(fn):
        def body(b, carry):
            fn(pltpu.make_async_copy(zbuf, xb_hbm.at[_token_rows(b, tb * chunks)], zsem))
            return carry
        lax.fori_loop(na_ref[0], nb, body, 0)

    def pad_copies(e, fn):
        pos = pad0_ref[e]
        n = padn_ref[e]
        bit = tb // 2
        while bit >= 1:
            @pl.when((n & bit) != 0)
            def _(pos=pos, bit=bit):
                rows = pl.ds(pl.multiple_of(pos * chunks, chunks), bit * chunks)
                fn(pltpu.make_async_copy(zbuf.at[pl.ds(0, bit * chunks)], xb_hbm.at[rows], zsem))
            pos = pos + (n & bit)
            bit //= 2

    @pl.when(i == 0)
    def _():
        zbuf[...] = jnp.zeros_like(zbuf)
        lax.fori_loop(0, N_EXPERTS, lambda e, c: (pad_copies(e, lambda cp: cp.start()), c)[1], 0)
        tail_copies(lambda cp: cp.start())

    def row_copy(r, k):
        dst = dest_ref[TOP_K * (i * tm + r) + k]
        return pltpu.make_async_copy(h_ref.at[_token_rows(r, chunks)], xb_hbm.at[_token_rows(dst, chunks)], sem)

    def for_rows(fn):
        def body(r, carry):
            for k in range(TOP_K):
                fn(r, k)
            return carry
        lax.fori_loop(0, tm, body, 0, unroll=ROW_DMA_UNROLL)

    for_rows(lambda r, k: row_copy(r, k).start(priority=k))
    for_rows(lambda r, k: row_copy(r, k).wait())

    @pl.when(i == 0)
    def _():
        lax.fori_loop(0, N_EXPERTS, lambda e, c: (pad_copies(e, lambda cp: cp.wait()), c)[1], 0)
        tail_copies(lambda cp: cp.wait())


def _pack_bf16_pairs(h):
    half = h.shape[1] // 2
    bits = lax.bitcast_convert_type(h.astype(BF16).astype(F32), jnp.uint32)
    return bits[:, half:] | (bits[:, :half] >> 16)


def _unpack_bf16_pairs(p):
    lo = lax.bitcast_convert_type(p << 16, F32)
    hi = lax.bitcast_convert_type(p & jnp.uint32(0xFFFF0000), F32)
    return jnp.concatenate([lo, hi], axis=1)


def _expert_kernel(be_ref, bi_ref, first_ref, par_ref, nxt1_ref, nxt2_ref, na_ref,
                   x_ref, wg_hbm, wu_hbm, wd_hbm, y_ref,
                   wg_st, wu_st, wd_st, wg_c, wu_c, wd_c, wsem, *, layer):
    b = pl.program_id(0)
    active = b < na_ref[0]
    d = wg_c.shape[0]
    xchunks = d // LANES // 2
    tb = x_ref.shape[0] // xchunks

    def weight_copies(e, slot):
        return [pltpu.make_async_copy(src.at[layer, e], dst.at[slot], wsem.at[slot, n])
                for n, (src, dst) in enumerate(((wg_hbm, wg_st), (wu_hbm, wu_st), (wd_hbm, wd_st)))]

    @pl.when(b == 0)
    def _():
        for cp in weight_copies(be_ref[0], 0):
            cp.start()

        @pl.when(nxt1_ref[0] >= 0)
        def _():
            for cp in weight_copies(nxt1_ref[0], 1):
                cp.start()

    is_first = first_ref[b] == 1

    def block(refresh):
        slot = par_ref[b]
        x = _unpack_bf16_pairs(_load_token_major(x_ref, tb, xchunks)).astype(BF16)
        if refresh:
            wg_c[...] = wg_st[slot].astype(BF16)
        gate = jnp.dot(x, wg_c[...], preferred_element_type=F32)
        if refresh:
            wu_c[...] = wu_st[slot].astype(BF16)
        up = jnp.dot(x, wu_c[...], preferred_element_type=F32)
        act = (gate * _sigmoid(gate) * up).astype(BF16)
        if refresh:
            wd_c[...] = wd_st[slot].astype(BF16)
        _store_token_major(y_ref, _pack_bf16_pairs(jnp.dot(act, wd_c[...], preferred_element_type=F32)))

    @pl.when(active & is_first)
    def _():
        slot = par_ref[b]
        for cp in weight_copies(be_ref[b], slot):
            cp.wait()
        block(True)

        @pl.when(nxt2_ref[b] >= 0)
        def _():
            for cp in weight_copies(nxt2_ref[b], slot):
                cp.start()

    @pl.when(active & jnp.logical_not(is_first))
    def _():
        block(False)

    @pl.when(jnp.logical_not(active))
    def _():
        y_ref[...] = jnp.zeros_like(y_ref)


def _moe_plan(experts, t, tb):
    a = t * TOP_K
    e_flat = experts.reshape(a)
    lanes = jnp.arange(N_EXPERTS, dtype=jnp.int32)[None, :]
    onehot = (e_flat[:, None] == lanes).astype(jnp.int32)
    csum = jnp.cumsum(onehot, axis=0)
    counts = csum[-1]
    pcounts = (counts + tb - 1) // tb * tb
    pends = jnp.cumsum(pcounts)
    pstarts = pends - pcounts
    dest = jnp.sum(onehot * (pstarts[None, :] + csum - 1), axis=1).astype(jnp.int32)
    nb = (a + N_EXPERTS * tb) // tb
    n_act = (pends[-1] // tb).astype(jnp.int32)
    blk = jnp.arange(nb, dtype=jnp.int32)
    blk_idx = jnp.minimum(blk, n_act - 1)
    be = jnp.sum((pends[None, :] <= (blk_idx * tb)[:, None]).astype(jnp.int32), axis=1)
    be = jnp.minimum(be, N_EXPERTS - 1).astype(jnp.int32)
    first = ((blk * tb == pstarts[be]) & (blk < n_act)).astype(jnp.int32)
    parity = ((jnp.cumsum(first) - 1) & 1).astype(jnp.int32)

    def next_expert(e):
        after = pends[jnp.maximum(e, 0)] // tb
        return jnp.where((e >= 0) & (after < n_act), be[jnp.minimum(after, nb - 1)], -1).astype(jnp.int32)

    nxt1 = next_expert(be)
    nxt2 = next_expert(nxt1)
    plan = dict(dest=dest, be=be, blk_idx=blk_idx, first=first, parity=parity, nxt1=nxt1, nxt2=nxt2,
                n_act=n_act.reshape(1),
                pad0=(pstarts + counts).astype(jnp.int32), padn=(pcounts - counts).astype(jnp.int32))
    return plan


def _experts(h, experts, layer, w_gate, w_up, w_down):
    d, de = w_gate.shape[2:]
    chunks = d // LANES
    xchunks = chunks // 2
    t = h.shape[0] // xchunks
    tb = EXPERT_TB
    tm = min(PRE_TM, t)
    plan = _moe_plan(experts, t, tb)
    nb = plan["be"].shape[0]
    p = nb * tb
    xb = pl.pallas_call(
        functools.partial(_dispatch_kernel, chunks=xchunks),
        grid_spec=pltpu.PrefetchScalarGridSpec(
            num_scalar_prefetch=4, grid=(t // tm,),
            in_specs=[pl.BlockSpec((tm * xchunks, LANES), lambda i, *_: (i, 0))],
            out_specs=pl.BlockSpec(memory_space=pl.ANY),
            scratch_shapes=[pltpu.VMEM((tb * xchunks, LANES), h.dtype),
                            pltpu.SemaphoreType.DMA(()), pltpu.SemaphoreType.DMA(())]),
        out_shape=jax.ShapeDtypeStruct((p * xchunks, LANES), h.dtype),
        compiler_params=_cparams("arbitrary"), name="moe_dispatch",
    )(plan["dest"], plan["pad0"], plan["padn"], plan["n_act"], h)
    hbm = pl.BlockSpec(memory_space=pl.ANY)
    yb = pl.pallas_call(
        functools.partial(_expert_kernel, layer=layer),
        grid_spec=pltpu.PrefetchScalarGridSpec(
            num_scalar_prefetch=7, grid=(nb,),
            in_specs=[pl.BlockSpec((tb * xchunks, LANES), lambda b, be, bi, *_: (bi[b], 0)), hbm, hbm, hbm],
            out_specs=pl.BlockSpec((tb * xchunks, LANES), lambda b, *_: (b, 0)),
            scratch_shapes=[pltpu.VMEM((2, d, de), F32), pltpu.VMEM((2, d, de), F32), pltpu.VMEM((2, de, d), F32),
                            pltpu.VMEM((d, de), BF16), pltpu.VMEM((d, de), BF16), pltpu.VMEM((de, d), BF16),
                            pltpu.SemaphoreType.DMA((2, 3))]),
        out_shape=jax.ShapeDtypeStruct((p * xchunks, LANES), jnp.uint32),
        compiler_params=_cparams("arbitrary"), name="moe_experts",
    )(plan["be"], plan["blk_idx"], plan["first"], plan["parity"], plan["nxt1"], plan["nxt2"], plan["n_act"],
      xb, w_gate, w_up, w_down)
    return yb, plan["dest"]


def kernel(x, c, norm1_g, norm2_g, ada_w, ada_b, mix_in_w, na_rpb, na_out_w, fourier_out_w, branch_gate_w,
           branch_gate_b, mix_out_w, router_group_w, router_group_b, router_expert_w, router_expert_b,
           expert_w_gate, expert_w_up, expert_w_down, final_g):
    bsz, s, d = x.shape
    assert bsz == 1 and s % GRID_W == 0
    depth = ada_w.shape[0]
    na_w = NA_HEADS * NA_HEAD_DIM
    f_w = F_GROUPS * F_GROUP_DIM
    xs = x.reshape(s, d)
    mod = _ada_all(c, ada_w, ada_b)
    pad = ROUTER_LANES - N_GROUPS - N_EXPERTS
    wr_all = jnp.concatenate([router_group_w, router_expert_w, jnp.zeros((depth, d, pad), F32)], axis=2)
    wr_hi = wr_all.astype(BF16)
    wr_lo = (wr_all - wr_hi.astype(F32)).astype(BF16)
    wr_all = jnp.concatenate([wr_hi, wr_lo], axis=2)
    br_all = jnp.concatenate([router_group_b, router_expert_b, jnp.zeros((depth, pad), F32)], axis=1)
    moe = g2 = None
    for l in range(depth):
        sh1, sc1, g1, sh2, sc2, g2_l = [mod[l, :, i * d:(i + 1) * d] for i in range(N_MOD)]
        if moe is None:
            (h,) = _pre(xs, norm1_g[l], shift=sh1, scale=sc1)
        else:
            xs, h = _pre(xs, norm1_g[l], moe=moe, g2=g2, shift=sh1, scale=sc1)
        qkv = _mm([(h, mix_in_w, l, 0)], [], lambda prod: prod(0), 3 * na_w, BF16, "mix_in_qkv", tn=MM_TN_WIDE)
        u = _mm([(h, mix_in_w, l, 3 * na_w)], [], lambda prod: prod(0), f_w, jnp.uint32, "mix_in_fourier",
                tn=f_w)
        y_att = _attention(qkv, na_rpb[l])
        y_fft = _fourier(u)
        bias_bg = branch_gate_b[l].reshape(1, -1)
        mixed = _mm([(h, branch_gate_w, l, 0), (h, branch_gate_w, l, d),
                     (y_att, na_out_w, l, 0), (y_fft, fourier_out_w, l, 0)],
                    [(bias_bg, _rowvec, _vec_cols(0)), (bias_bg, _rowvec, _vec_cols(d))],
                    lambda prod, ba, bf: (_sigmoid(prod(0) + ba) * prod(2) + _sigmoid(prod(1) + bf) * prod(3)),
                    d, BF16, "gated_branch_mix", tn=MM_TN, tm=MM_TM_FUSED)
        xs = _mm([(mixed, mix_out_w, l, 0)],
                 [(xs, _tile, _at_cols(0)), (g1, _rowvec, _vec_cols(0))],
                 lambda prod, xr, g: xr + g * prod(0), d, F32, "mix_out", tn=MM_TN_WIDE)
        h2, wts, ids = _pre(xs, norm2_g[l], shift=sh2, scale=sc2, wr=wr_all[l], br=br_all[l].reshape(1, -1),
                            h_dtype=F32)
        yb, dest = _experts(h2, ids[:, :TOP_K], l, expert_w_gate, expert_w_up, expert_w_down)
        moe, g2 = (yb, dest, wts), g2_l
    _, out = _pre(xs, final_g, moe=moe, g2=g2, h_dtype=F32)
    return out.reshape(bsz, s, d)
```

```python
import functools
import math

import numpy as np
import jax
import jax.numpy as jnp
from jax import lax
from jax.experimental import pallas as pl
from jax.experimental.pallas import tpu as pltpu

F32 = jnp.float32
BF16 = jnp.bfloat16

GRID_W = 64
NA_HEADS = 16
NA_HEAD_DIM = 64
WIN_ROWS = 8
WIN_COLS = 16
F_GROUPS = 4
F_GROUP_DIM = 256
N_GROUPS = 4
EXPERTS_PER_GROUP = 8
N_EXPERTS = N_GROUPS * EXPERTS_PER_GROUP
TOP_K = 2
N_MOD = 6
EPS = 1e-6
NEG_INF = -1e30

LANES = 128
SUBLANES = 8

ADA_TN = 1536
ADA_ROWS = 256
PRE_TM = 256
MM_TM = 1024
MM_TM_FUSED = 512
MM_TN = 512
MM_TN_WIDE = 1024
ATTN_ROWS = 16
EXPERT_TB = 256
ROW_DMA_UNROLL = 8
FFT_JB = 8
FFT_KB = 16
ROUTER_LANES = LANES

VMEM_LIMIT = 56 * 1024 * 1024


def _cparams(*sem):
    return pltpu.CompilerParams(dimension_semantics=sem, vmem_limit_bytes=VMEM_LIMIT)


def _sigmoid(x):
    return 1.0 / (1.0 + jnp.exp(-x))


def _ada_kernel(c_ref, w_ref, b_ref, o_ref, cs_ref):
    c = c_ref[...]
    cs_ref[...] = c * _sigmoid(c)
    d, tn = w_ref.shape

    def body(i, acc):
        r0 = pl.multiple_of(i * ADA_ROWS, ADA_ROWS)
        prod = w_ref[pl.ds(r0, ADA_ROWS), :] * cs_ref[pl.ds(r0, ADA_ROWS), :]
        return acc + jnp.sum(prod.reshape(ADA_ROWS // SUBLANES, SUBLANES, tn), axis=0)

    acc = lax.fori_loop(0, d // ADA_ROWS, body, jnp.zeros((SUBLANES, tn), F32))
    o_ref[...] = jnp.sum(acc, axis=0, keepdims=True) + b_ref[...]


def _ada_all(c, ada_w, ada_b):
    nl, d, n = ada_w.shape
    tn = min(ADA_TN, n)
    return pl.pallas_call(
        _ada_kernel,
        grid=(nl, n // tn),
        in_specs=[
            pl.BlockSpec((d, 1), lambda l, j: (0, 0)),
            pl.BlockSpec((None, d, tn), lambda l, j: (l, 0, j)),
            pl.BlockSpec((None, 1, tn), lambda l, j: (l, 0, j)),
        ],
        out_specs=pl.BlockSpec((None, 1, tn), lambda l, j: (l, 0, j)),
        out_shape=jax.ShapeDtypeStruct((nl, 1, n), F32),
        scratch_shapes=[pltpu.VMEM((d, 1), F32)],
        compiler_params=_cparams("parallel", "parallel"),
        name="ada_proj",
    )(c.reshape(d, 1), ada_w, ada_b.reshape(nl, 1, n))


def _rms(x, g):
    return x * lax.rsqrt(jnp.mean(x * x, axis=-1, keepdims=True) + EPS) * g


def _router(h, wr_ref, br_ref):
    w_split = wr_ref[...]
    h_hi = h.astype(BF16)
    h_lo = (h - h_hi.astype(F32)).astype(BF16)
    dot = functools.partial(jnp.dot, preferred_element_type=F32)
    both = dot(h_hi, w_split)
    logits = both[:, :ROUTER_LANES] + (both[:, ROUTER_LANES:] + dot(h_lo, w_split[:, :ROUTER_LANES])) + br_ref[...]
    lane = lax.broadcasted_iota(jnp.int32, logits.shape, 1)
    big = jnp.int32(ROUTER_LANES)
    is_g = lane < N_GROUPS
    gl = jnp.where(is_g, logits, NEG_INF)
    gmax = jnp.max(gl, axis=-1, keepdims=True)
    gsum = jnp.sum(jnp.where(is_g, jnp.exp(gl - gmax), 0.0), axis=-1, keepdims=True)
    g_top_p = 1.0 / gsum
    g_top = jnp.min(jnp.where(is_g & (gl == gmax), lane, big), axis=-1, keepdims=True)
    e_lane = lane - N_GROUPS
    sel = (e_lane >= 0) & (e_lane < N_EXPERTS) & ((e_lane >> 3) == g_top)
    l1 = jnp.where(sel, logits, NEG_INF)
    m1 = jnp.max(l1, axis=-1, keepdims=True)
    i1 = jnp.min(jnp.where(sel & (l1 == m1), lane, big), axis=-1, keepdims=True)
    sel2 = sel & (lane != i1)
    l2 = jnp.where(sel2, logits, NEG_INF)
    m2 = jnp.max(l2, axis=-1, keepdims=True)
    i2 = jnp.min(jnp.where(sel2 & (l2 == m2), lane, big), axis=-1, keepdims=True)
    e21 = jnp.exp(m2 - m1)
    p1 = 1.0 / (1.0 + e21)
    p2 = e21 / (1.0 + e21)
    wts = jnp.where(lane == 0, g_top_p * p1, jnp.where(lane == 1, g_top_p * p2, 0.0))
    ids = jnp.where(lane == 0, i1 - N_GROUPS, jnp.where(lane == 1, i2 - N_GROUPS, 0))
    return wts, ids


def _load_token_major(ref, n_tok, chunks):
    return jnp.concatenate([ref[pl.ds(c, n_tok, stride=chunks), :] for c in range(chunks)], axis=1)


def _store_token_major(ref, val):
    n_tok, d = val.shape
    chunks = d // LANES
    for c in range(chunks):
        ref[pl.ds(c, n_tok, stride=chunks), :] = val[:, c * LANES:(c + 1) * LANES]


def _token_rows(tok, chunks):
    return pl.ds(pl.multiple_of(tok * chunks, chunks), chunks)


def _pre_kernel(*refs, combine, modulate, router, emit_h, h_dtype):
    it = iter(refs)
    if combine:
        dest_ref = next(it)
    x_ref = next(it)
    if combine:
        y_hbm, cw_ref, g2_ref = next(it), next(it), next(it)
    g_ref = next(it)
    if modulate:
        sh_ref, sc_ref = next(it), next(it)
    if router:
        wr_ref, br_ref = next(it), next(it)
    outs = [next(it) for _ in range(int(combine) + int(emit_h) + 2 * int(router))]
    x = x_ref[...]
    if combine:
        gbuf, gsem = next(it), next(it)
        i = pl.program_id(0)
        slot = i % 2
        tm, d = x_ref.shape
        chunks = d // LANES // 2

        def row_copy(step, sl, r, k):
            src = dest_ref[TOP_K * (step * tm + r) + k]
            return pltpu.make_async_copy(y_hbm.at[_token_rows(src, chunks)],
                                         gbuf.at[sl, k, _token_rows(r, chunks)], gsem.at[sl])

        def for_rows(fn):
            def body(r, carry):
                for k in range(TOP_K):
                    fn(r, k)
                return carry
            lax.fori_loop(0, tm, body, 0, unroll=ROW_DMA_UNROLL)

        @pl.when(i == 0)
        def _():
            for_rows(lambda r, k: row_copy(0, 0, r, k).start(priority=k))

        @pl.when(i + 1 < pl.num_programs(0))
        def _():
            for_rows(lambda r, k: row_copy(i + 1, 1 - slot, r, k).start(priority=k))

        for_rows(lambda r, k: row_copy(i, slot, r, k).wait())
        cw = cw_ref[...]
        y0 = _unpack_bf16_pairs(_load_token_major(gbuf.at[slot, 0], tm, chunks))
        y1 = _unpack_bf16_pairs(_load_token_major(gbuf.at[slot, 1], tm, chunks))
        x = x + g2_ref[...] * (cw[:, 0:1] * y0 + cw[:, 1:2] * y1)
        outs.pop(0)[...] = x
    h = _rms(x, g_ref[...])
    if modulate:
        h = h * (1.0 + sc_ref[...]) + sh_ref[...]
    if emit_h and router:
        _store_token_major(outs.pop(0), _pack_bf16_pairs(h))
    elif emit_h:
        outs.pop(0)[...] = h.astype(h_dtype)
    if router:
        wts, ids = _router(h, wr_ref, br_ref)
        outs.pop(0)[...] = wts
        outs.pop(0)[...] = ids


def _pre(x, g, *, moe=None, g2=None, shift=None, scale=None, wr=None, br=None, h_dtype=BF16):
    t, d = x.shape
    tm = min(PRE_TM, t)
    chunks = d // LANES
    combine, modulate, router = moe is not None, shift is not None, wr is not None
    row = pl.BlockSpec((tm, d), lambda i, *_: (i, 0))
    vec = pl.BlockSpec((1, d), lambda i, *_: (0, 0))
    lane_row = pl.BlockSpec((tm, ROUTER_LANES), lambda i, *_: (i, 0))
    prefetch, args, specs, scratch = [], [x], [row], []
    if combine:
        yb, dest, cw = moe
        prefetch.append(dest)
        args += [yb, cw, g2]
        specs += [pl.BlockSpec(memory_space=pl.ANY), lane_row, vec]
        scratch = [pltpu.VMEM((2, TOP_K, tm * chunks // 2, LANES), jnp.uint32), pltpu.SemaphoreType.DMA((2,))]
    args.append(g.reshape(1, d))
    specs.append(vec)
    if modulate:
        args += [shift, scale]
        specs += [vec, vec]
    if router:
        args += [wr, br]
        specs += [pl.BlockSpec(wr.shape, lambda i, *_: (0, 0)), pl.BlockSpec(br.shape, lambda i, *_: (0, 0))]
    out_shape, out_specs = [], []
    if combine:
        out_shape.append(jax.ShapeDtypeStruct((t, d), F32))
        out_specs.append(row)
    if router:
        out_shape.append(jax.ShapeDtypeStruct((t * chunks // 2, LANES), jnp.uint32))
        out_specs.append(pl.BlockSpec((tm * chunks // 2, LANES), lambda i, *_: (i, 0)))
    else:
        out_shape.append(jax.ShapeDtypeStruct((t, d), h_dtype))
        out_specs.append(row)
    if router:
        out_shape += [jax.ShapeDtypeStruct((t, ROUTER_LANES), F32), jax.ShapeDtypeStruct((t, ROUTER_LANES), jnp.int32)]
        out_specs += [lane_row, lane_row]
    kern = functools.partial(_pre_kernel, combine=combine, modulate=modulate, router=router,
                             emit_h=True, h_dtype=h_dtype)
    grid_spec = pltpu.PrefetchScalarGridSpec(
        num_scalar_prefetch=len(prefetch), grid=(t // tm,), in_specs=specs, out_specs=out_specs,
        scratch_shapes=scratch)
    return pl.pallas_call(
        kern, grid_spec=grid_spec, out_shape=out_shape,
        compiler_params=_cparams("arbitrary" if combine else "parallel"), name="pre_norm",
    )(*prefetch, *args)


def _mm_kernel(*refs, lhs_of, n_lhs, n_extras, epilogue):
    n_pairs = len(lhs_of)
    lhs = refs[:n_lhs]
    weights = refs[n_lhs:n_lhs + n_pairs]
    extras = refs[n_lhs + n_pairs:n_lhs + n_pairs + n_extras]
    o_ref = refs[n_lhs + n_pairs + n_extras]
    caches = refs[n_lhs + n_pairs + n_extras + 1:]

    @pl.when(pl.program_id(1) == 0)
    def _():
        for p in range(n_pairs):
            caches[p][...] = weights[p][...].astype(BF16)

    def product(p):
        a = lhs[lhs_of[p]][...]
        a = _unpack_bf16_pairs(a) if a.dtype == jnp.uint32 else a
        return jnp.dot(a.astype(BF16), caches[p][...], preferred_element_type=F32)

    out = epilogue(product, *[r[...] for r in extras])
    o_ref[...] = _pack_bf16_pairs(out) if o_ref.dtype == jnp.uint32 else out.astype(o_ref.dtype)


def _mm(pairs, extras, epilogue, n, out_dtype, name, tn=MM_TN, tm=MM_TM):
    m = pairs[0][0].shape[0]
    tm, tn = min(tm, m), min(tn, n)
    packed_out = out_dtype == jnp.uint32
    assert not packed_out or tn == n
    lhs, lhs_of, w_args, w_specs, scratch = [], [], [], [], []
    for a, w, layer, off in pairs:
        k = a.shape[1] * (2 if a.dtype == jnp.uint32 else 1)
        if not any(a is seen for seen in lhs):
            lhs.append(a)
        lhs_of.append([a is seen for seen in lhs].index(True))
        w_args.append(w)
        w_specs.append(pl.BlockSpec((None, k, tn), lambda j, i, layer=layer, off=off: (layer, 0, j + off // tn)))
        scratch.append(pltpu.VMEM((k, tn), BF16))
    args = lhs + w_args
    specs = [pl.BlockSpec((tm, a.shape[1]), lambda j, i: (i, 0)) for a in lhs] + w_specs
    for arr, bshape, imap in extras:
        args.append(arr)
        specs.append(pl.BlockSpec(bshape(tm, tn), imap(tn)))
    return pl.pallas_call(
        functools.partial(_mm_kernel, lhs_of=tuple(lhs_of), n_lhs=len(lhs), n_extras=len(extras),
                          epilogue=epilogue),
        grid=(n // tn, m // tm), in_specs=specs,
        out_specs=pl.BlockSpec((tm, tn // 2 if packed_out else tn), lambda j, i: (i, j)),
        out_shape=jax.ShapeDtypeStruct((m, n // 2 if packed_out else n), out_dtype),
        scratch_shapes=scratch,
        compiler_params=_cparams("arbitrary", "arbitrary"), name=name,
    )(*args)


def _tile(tm, tn):
    return (tm, tn)


def _rowvec(tm, tn):
    return (1, tn)


def _at_cols(off):
    return lambda tn: (lambda j, i: (i, j + off // tn))


def _vec_cols(off):
    return lambda tn: (lambda j, i: (0, j + off // tn))


def _attn_bias_fill(rpb_ref, b_ref, head0):
    n_dr, n_dc = 2 * WIN_ROWS - 1, 2 * WIN_COLS - 1
    qc = lax.broadcasted_iota(jnp.int32, (GRID_W, 2 * GRID_W), 0)
    kc = lax.broadcasted_iota(jnp.int32, (GRID_W, 2 * GRID_W), 1) & (GRID_W - 1)
    cs = jnp.clip(qc - WIN_COLS // 2, 0, GRID_W - WIN_COLS)
    inside = (kc >= cs) & (kc < cs + WIN_COLS)
    dc = jnp.clip(kc - qc, -(WIN_COLS - 1), WIN_COLS - 1) + (WIN_COLS - 1)
    lane = lax.broadcasted_iota(jnp.int32, (SUBLANES, 2 * GRID_W), 1)
    for h in range(2):
        base = (head0 + h) * (n_dr * n_dc)
        for dr in range(n_dr):
            row = jnp.zeros((SUBLANES, 2 * GRID_W), F32)
            for d in range(n_dc):
                row = jnp.where(lane == d, rpb_ref[base + dr * n_dc + d], row)
            table = jnp.concatenate([row] * (GRID_W // SUBLANES), axis=0)
            t = jnp.where(inside, jnp.take_along_axis(table, dc, axis=1), NEG_INF)
            for var in range(WIN_ROWS):
                j = dr - var
                if 0 <= j < WIN_ROWS:
                    c0 = j * GRID_W
                    lo = c0 % (2 * GRID_W)
                    b_ref[var, h * GRID_W:(h + 1) * GRID_W, c0:c0 + GRID_W] = t[:, lo:lo + GRID_W]


def _attn_kernel(rpb_ref, q_ref, k_ref, v_ref, o_ref, b_ref, s_ref, *, n_rows):
    rb = pl.program_id(1)
    n_q = q_ref.shape[0]
    wkeys = WIN_ROWS * GRID_W
    dh = NA_HEAD_DIM
    lane = lax.broadcasted_iota(jnp.int32, (GRID_W, 2 * dh), 1)
    first = lane < dh

    @pl.when(rb == 0)
    def _():
        _attn_bias_fill(rpb_ref, b_ref, 2 * pl.program_id(0))

    def window(i):
        r = rb * n_q + i
        rs = jnp.clip(r - WIN_ROWS // 2, 0, n_rows - WIN_ROWS)
        return rs, rs - r + (WIN_ROWS - 1)

    for i in range(n_q):
        rs, var = window(i)
        kw = k_ref[pl.ds(rs, WIN_ROWS)].reshape(wkeys, 2 * dh)
        q = q_ref[i] * jnp.asarray(dh ** -0.5, BF16)
        zero = jnp.zeros_like(q)
        q2 = jnp.concatenate([jnp.where(first, q, zero), jnp.where(first, zero, q)], axis=0)
        s = lax.dot_general(q2, kw, (((1,), (1,)), ((), ())), preferred_element_type=F32)
        b = b_ref[var]
        s_ref[i] = s + b
    for i in range(n_q):
        rs, _ = window(i)
        vw = v_ref[pl.ds(rs, WIN_ROWS)].reshape(wkeys, 2 * dh)
        s = s_ref[i]
        m = jnp.max(s, axis=-1, keepdims=True)
        p = jnp.exp(s - m)
        l = jnp.sum(p, axis=-1, keepdims=True)
        o = jnp.dot(p.astype(BF16), vw, preferred_element_type=F32) / l
        o_ref[i] = jnp.where(first, o[:GRID_W], o[GRID_W:]).astype(o_ref.dtype)


def _attention(proj, rpb):
    t, width = proj.shape
    n_rows = t // GRID_W
    pairs = NA_HEADS // 2
    p3 = proj.reshape(n_rows, GRID_W, width)
    blk = 2 * NA_HEAD_DIM
    rows = min(ATTN_ROWS, n_rows)
    wkeys = WIN_ROWS * GRID_W
    out = pl.pallas_call(
        functools.partial(_attn_kernel, n_rows=n_rows),
        grid_spec=pltpu.PrefetchScalarGridSpec(
            num_scalar_prefetch=1, grid=(pairs, n_rows // rows),
            in_specs=[
                pl.BlockSpec((rows, GRID_W, blk), lambda hp, rb, _: (rb, 0, hp)),
                pl.BlockSpec((n_rows, GRID_W, blk), lambda hp, rb, _: (0, 0, pairs + hp)),
                pl.BlockSpec((n_rows, GRID_W, blk), lambda hp, rb, _: (0, 0, 2 * pairs + hp)),
            ],
            out_specs=pl.BlockSpec((rows, GRID_W, blk), lambda hp, rb, _: (rb, 0, hp)),
            scratch_shapes=[pltpu.VMEM((WIN_ROWS, 2 * GRID_W, wkeys), F32),
                            pltpu.VMEM((rows, 2 * GRID_W, wkeys), F32)]),
        out_shape=jax.ShapeDtypeStruct((n_rows, GRID_W, NA_HEADS * NA_HEAD_DIM), BF16),
        compiler_params=_cparams("arbitrary", "arbitrary"), name="na_attention",
    )(rpb.reshape(-1), p3, p3, p3)
    return out.reshape(t, NA_HEADS * NA_HEAD_DIM)


def _dft_tables(n1, n2, dc):
    s = n1 * n2
    a = np.arange(dc, dtype=np.float64)
    ang_c = 2 * np.pi * np.outer(a, a) / dc
    w_c = np.concatenate([np.cos(ang_c), -np.sin(ang_c)], axis=1)
    a1 = np.arange(n1, dtype=np.float64)
    ang1 = 2 * np.pi * np.outer(a1, a1) / n1
    c1, s1 = np.cos(ang1), np.sin(ang1)
    m1 = np.block([[c1, s1], [-s1, c1]])
    a2 = np.arange(n2, dtype=np.float64)
    ang_t = 2 * np.pi * np.outer(a2, a1) / s
    tw_c = np.repeat(np.cos(ang_t)[:, :, None], LANES, axis=2)
    tw_s = np.repeat(np.sin(ang_t)[:, :, None], LANES, axis=2)
    ang2 = 2 * np.pi * np.outer(a2, a2) / n2
    m2 = np.concatenate([np.cos(ang2), np.sin(ang2)], axis=1)
    return (jnp.asarray(w_c, BF16), jnp.asarray(m1, BF16), jnp.asarray(tw_c, F32), jnp.asarray(tw_s, F32),
            jnp.asarray(m2, BF16))


def _fft1_kernel(u_hbm, wc_ref, m1_ref, tc_ref, ts_ref, br_hbm, bi_hbm,
                 ubuf, zz_ref, obuf, isem, osem, *, n1, dc):
    s = pl.program_id(0)
    ns = pl.num_programs(0)
    jb = ubuf.shape[1]
    slot = s % 2

    def in_copy(step, sl, jj):
        return pltpu.make_async_copy(u_hbm.at[:, step * jb + jj], ubuf.at[sl, jj], isem.at[sl])

    def out_copies(step, sl, jj):
        j = step * jb + jj
        return (pltpu.make_async_copy(obuf.at[sl, 0, jj], br_hbm.at[:, j], osem.at[sl]),
                pltpu.make_async_copy(obuf.at[sl, 1, jj], bi_hbm.at[:, j], osem.at[sl]))

    @pl.when(s == 0)
    def _():
        for jj in range(jb):
            in_copy(0, 0, jj).start()

    @pl.when(s + 1 < ns)
    def _():
        for jj in range(jb):
            in_copy(s + 1, 1 - slot, jj).start()

    for jj in range(jb):
        in_copy(s, slot, jj).wait()

    @pl.when(s >= 2)
    def _():
        for jj in range(jb):
            for cp in out_copies(s - 2, slot, jj):
                cp.wait()

    c = 2 * ubuf.shape[3]
    u = _unpack_bf16_pairs(ubuf[slot].reshape(jb * n1, c // 2)).astype(BF16)
    for g in range(c // dc):
        z = jnp.dot(u[:, g * dc:(g + 1) * dc], wc_ref[...], preferred_element_type=F32)
        z = z.astype(BF16)
        for jj in range(jb):
            zz_ref[jj, :n1, g * dc:(g + 1) * dc] = z[jj * n1:(jj + 1) * n1, :dc]
            zz_ref[jj, n1:, g * dc:(g + 1) * dc] = z[jj * n1:(jj + 1) * n1, dc:]
    for jj in range(jb):
        a = jnp.dot(m1_ref[...], zz_ref[jj], preferred_element_type=F32)
        ar, ai = a[:n1], a[n1:]
        tc = jnp.concatenate([tc_ref[jj]] * (c // LANES), axis=1)
        ts = jnp.concatenate([ts_ref[jj]] * (c // LANES), axis=1)
        obuf[slot, 0, jj] = _pack_bf16_pairs(ar * tc + ai * ts)
        obuf[slot, 1, jj] = _pack_bf16_pairs(ai * tc - ar * ts)
    for jj in range(jb):
        for cp in out_copies(s, slot, jj):
            cp.start()

    @pl.when(s == ns - 1)
    def _():
        for jj in range(jb):
            for cp in out_copies(s, slot, jj):
                cp.wait()

        @pl.when(s >= 1)
        def _():
            for jj in range(jb):
                for cp in out_copies(s - 1, 1 - slot, jj):
                    cp.wait()


def _fft2_kernel(br_ref, bi_ref, m2_ref, y_hbm, obuf, osem, *, n2, scale):
    s = pl.program_id(0)
    ns = pl.num_programs(0)
    kb = obuf.shape[1]
    slot = s % 2

    def out_copy(step, sl, kk):
        return pltpu.make_async_copy(obuf.at[sl, kk], y_hbm.at[:, step * kb + kk], osem.at[sl])

    @pl.when(s >= 2)
    def _():
        for kk in range(kb):
            out_copy(s - 2, slot, kk).wait()

    for kk in range(kb):
        rows = slice(kk * n2, (kk + 1) * n2)
        bb = _unpack_bf16_pairs(jnp.concatenate([br_ref[rows, :], bi_ref[rows, :]], axis=0)).astype(BF16)
        obuf[slot, kk] = _pack_bf16_pairs(jnp.dot(m2_ref[...], bb, preferred_element_type=F32) * scale)
    for kk in range(kb):
        out_copy(s, slot, kk).start()

    @pl.when(s == ns - 1)
    def _():
        for kk in range(kb):
            out_copy(s, slot, kk).wait()

        @pl.when(s >= 1)
        def _():
            for kk in range(kb):
                out_copy(s - 1, 1 - slot, kk).wait()


def _fourier(u):
    t, hw = u.shape
    fw = 2 * hw
    n2 = GRID_W
    n1 = t // n2
    dc = F_GROUP_DIM
    w_c, m1, tw_c, tw_s, m2 = _dft_tables(n1, n2, dc)
    jb = min(FFT_JB, n2)
    kb = min(FFT_KB, n1)
    any_spec = pl.BlockSpec(memory_space=pl.ANY)
    tw_spec = pl.BlockSpec((jb, n1, LANES), lambda s: (s, 0, 0))
    b_re, b_im = pl.pallas_call(
        functools.partial(_fft1_kernel, n1=n1, dc=dc),
        grid=(n2 // jb,),
        in_specs=[
            any_spec,
            pl.BlockSpec((dc, 2 * dc), lambda s: (0, 0)),
            pl.BlockSpec((2 * n1, 2 * n1), lambda s: (0, 0)),
            tw_spec, tw_spec,
        ],
        out_specs=[any_spec, any_spec],
        out_shape=[jax.ShapeDtypeStruct((n1, n2, hw), jnp.uint32)] * 2,
        scratch_shapes=[
            pltpu.VMEM((2, jb, n1, hw), jnp.uint32),
            pltpu.VMEM((jb, 2 * n1, fw), BF16),
            pltpu.VMEM((2, 2, jb, n1, hw), jnp.uint32),
            pltpu.SemaphoreType.DMA((2,)),
            pltpu.SemaphoreType.DMA((2,)),
        ],
        compiler_params=_cparams("arbitrary"), name="fourier_stage1",
    )(u.reshape(n1, n2, hw), w_c, m1, tw_c, tw_s)
    scale = 1.0 / math.sqrt(t * dc)
    in_blk = pl.BlockSpec((kb * n2, hw), lambda s: (s, 0))
    y = pl.pallas_call(
        functools.partial(_fft2_kernel, n2=n2, scale=scale),
        grid=(n1 // kb,),
        in_specs=[in_blk, in_blk, pl.BlockSpec((n2, 2 * n2), lambda s: (0, 0))],
        out_specs=any_spec,
        out_shape=jax.ShapeDtypeStruct((n2, n1, hw), jnp.uint32),
        scratch_shapes=[pltpu.VMEM((2, kb, n2, hw), jnp.uint32), pltpu.SemaphoreType.DMA((2,))],
        compiler_params=_cparams("arbitrary"), name="fourier_stage2",
    )(b_re.reshape(t, hw), b_im.reshape(t, hw), m2)
    return y.reshape(t, hw)


def _dispatch_kernel(dest_ref, pad0_ref, padn_ref, na_ref, h_ref, xb_hbm, zbuf, sem, zsem, *, chunks):
    i = pl.program_id(0)
    tm = h_ref.shape[0] // chunks
    tb = zbuf.shape[0] // chunks
    nb = xb_hbm.shape[0] // (tb * chunks)

    def tail_copies(fn):
        def body(b, carry):
            fn(pltpu.make_async_copy(zbuf, xb_hbm.at[_token_rows(b, tb * chunks)], zsem))
            return carry
        lax.fori_loop(na_ref[0], nb, body, 0)

    def pad_copies(e, fn):
        pos = pad0_ref[e]
        n = padn_ref[e]
        bit = tb // 2
        while bit >= 1:
            @pl.when((n & bit) != 0)
            def _(pos=pos, bit=bit):
                rows = pl.ds(pl.multiple_of(pos * chunks, chunks), bit * chunks)
                fn(pltpu.make_async_copy(zbuf.at[pl.ds(0, bit * chunks)], xb_hbm.at[rows], zsem))
            pos = pos + (n & bit)
            bit //= 2

    @pl.when(i == 0)
    def _():
        zbuf[...] = jnp.zeros_like(zbuf)
        lax.fori_loop(0, N_EXPERTS, lambda e, c: (pad_copies(e, lambda cp: cp.start()), c)[1], 0)
        tail_copies(lambda cp: cp.start())

    def row_copy(r, k):
        dst = dest_ref[TOP_K * (i * tm + r) + k]
        return pltpu.make_async_copy(h_ref.at[_token_rows(r, chunks)], xb_hbm.at[_token_rows(dst, chunks)], sem)

    def for_rows(fn):
        def body(r, carry):
            for k in range(TOP_K):
                fn(r, k)
            return carry
        lax.fori_loop(0, tm, body, 0, unroll=ROW_DMA_UNROLL)

    for_rows(lambda r, k: row_copy(r, k).start(priority=k))
    for_rows(lambda r, k: row_copy(r, k).wait())

    @pl.when(i == 0)
    def _():
        lax.fori_loop(0, N_EXPERTS, lambda e, c: (pad_copies(e, lambda cp: cp.wait()), c)[1], 0)
        tail_copies(lambda cp: cp.wait())


def _pack_bf16_pairs(h):
    half = h.shape[1] // 2
    bits = lax.bitcast_convert_type(h.astype(BF16).astype(F32), jnp.uint32)
    return bits[:, half:] | (bits[:, :half] >> 16)


def _unpack_bf16_pairs(p):
    lo = lax.bitcast_convert_type(p << 16, F32)
    hi = lax.bitcast_convert_type(p & jnp.uint32(0xFFFF0000), F32)
    return jnp.concatenate([lo, hi], axis=1)


def _expert_kernel(be_ref, bi_ref, first_ref, par_ref, nxt1_ref, nxt2_ref, na_ref,
                   x_ref, wg_hbm, wu_hbm, wd_hbm, y_ref,
                   wg_st, wu_st, wd_st, wg_c, wu_c, wd_c, wsem, *, layer):
    b = pl.program_id(0)
    active = b < na_ref[0]
    d = wg_c.shape[0]
    xchunks = d // LANES // 2
    tb = x_ref.shape[0] // xchunks

    def weight_copies(e, slot):
        return [pltpu.make_async_copy(src.at[layer, e], dst.at[slot], wsem.at[slot, n])
                for n, (src, dst) in enumerate(((wg_hbm, wg_st), (wu_hbm, wu_st), (wd_hbm, wd_st)))]

    def start_weights(e, slot):
        for cp, priority in zip(weight_copies(e, slot), (1, 1, 0)):
            cp.start(priority=priority)

    @pl.when(b == 0)
    def _():
        start_weights(be_ref[0], 0)

        @pl.when(nxt1_ref[0] >= 0)
        def _():
            start_weights(nxt1_ref[0], 1)

    is_first = first_ref[b] == 1

    def block(refresh):
        slot = par_ref[b]
        x = _unpack_bf16_pairs(_load_token_major(x_ref, tb, xchunks)).astype(BF16)
        if refresh:
            wg_c[...] = wg_st[slot].astype(BF16)
        gate = jnp.dot(x, wg_c[...], preferred_element_type=F32)
        if refresh:
            wu_c[...] = wu_st[slot].astype(BF16)
        up = jnp.dot(x, wu_c[...], preferred_element_type=F32)
        act = (gate * _sigmoid(gate) * up).astype(BF16)
        if refresh:
            wd_c[...] = wd_st[slot].astype(BF16)
        _store_token_major(y_ref, _pack_bf16_pairs(jnp.dot(act, wd_c[...], preferred_element_type=F32)))

    @pl.when(active & is_first)
    def _():
        slot = par_ref[b]
        for cp in weight_copies(be_ref[b], slot):
            cp.wait()
        block(True)

        @pl.when(nxt2_ref[b] >= 0)
        def _():
            start_weights(nxt2_ref[b], slot)

    @pl.when(active & jnp.logical_not(is_first))
    def _():
        block(False)

    @pl.when(jnp.logical_not(active))
    def _():
        y_ref[...] = jnp.zeros_like(y_ref)


def _moe_plan(experts, t, tb):
    a = t * TOP_K
    e_flat = experts.reshape(a)
    lanes = jnp.arange(N_EXPERTS, dtype=jnp.int32)[None, :]
    onehot = (e_flat[:, None] == lanes).astype(jnp.int32)
    csum = jnp.cumsum(onehot, axis=0)
    counts = csum[-1]
    pcounts = (counts + tb - 1) // tb * tb
    pends = jnp.cumsum(pcounts)
    pstarts = pends - pcounts
    dest = jnp.sum(onehot * (pstarts[None, :] + csum - 1), axis=1).astype(jnp.int32)
    nb = (a + N_EXPERTS * tb) // tb
    n_act = (pends[-1] // tb).astype(jnp.int32)
    blk = jnp.arange(nb, dtype=jnp.int32)
    blk_idx = jnp.minimum(blk, n_act - 1)
    be = jnp.sum((pends[None, :] <= (blk_idx * tb)[:, None]).astype(jnp.int32), axis=1)
    be = jnp.minimum(be, N_EXPERTS - 1).astype(jnp.int32)
    first = ((blk * tb == pstarts[be]) & (blk < n_act)).astype(jnp.int32)
    parity = ((jnp.cumsum(first) - 1) & 1).astype(jnp.int32)

    def next_expert(e):
        after = pends[jnp.maximum(e, 0)] // tb
        return jnp.where((e >= 0) & (after < n_act), be[jnp.minimum(after, nb - 1)], -1).astype(jnp.int32)

    nxt1 = next_expert(be)
    nxt2 = next_expert(nxt1)
    plan = dict(dest=dest, be=be, blk_idx=blk_idx, first=first, parity=parity, nxt1=nxt1, nxt2=nxt2,
                n_act=n_act.reshape(1),
                pad0=(pstarts + counts).astype(jnp.int32), padn=(pcounts - counts).astype(jnp.int32))
    return plan


def _experts(h, experts, layer, w_gate, w_up, w_down):
    d, de = w_gate.shape[2:]
    chunks = d // LANES
    xchunks = chunks // 2
    t = h.shape[0] // xchunks
    tb = EXPERT_TB
    tm = min(PRE_TM, t)
    plan = _moe_plan(experts, t, tb)
    nb = plan["be"].shape[0]
    p = nb * tb
    xb = pl.pallas_call(
        functools.partial(_dispatch_kernel, chunks=xchunks),
        grid_spec=pltpu.PrefetchScalarGridSpec(
            num_scalar_prefetch=4, grid=(t // tm,),
            in_specs=[pl.BlockSpec((tm * xchunks, LANES), lambda i, *_: (i, 0))],
            out_specs=pl.BlockSpec(memory_space=pl.ANY),
            scratch_shapes=[pltpu.VMEM((tb * xchunks, LANES), h.dtype),
                            pltpu.SemaphoreType.DMA(()), pltpu.SemaphoreType.DMA(())]),
        out_shape=jax.ShapeDtypeStruct((p * xchunks, LANES), h.dtype),
        compiler_params=_cparams("arbitrary"), name="moe_dispatch",
    )(plan["dest"], plan["pad0"], plan["padn"], plan["n_act"], h)
    hbm = pl.BlockSpec(memory_space=pl.ANY)
    yb = pl.pallas_call(
        functools.partial(_expert_kernel, layer=layer),
        grid_spec=pltpu.PrefetchScalarGridSpec(
            num_scalar_prefetch=7, grid=(nb,),
            in_specs=[pl.BlockSpec((tb * xchunks, LANES), lambda b, be, bi, *_: (bi[b], 0)), hbm, hbm, hbm],
            out_specs=pl.BlockSpec((tb * xchunks, LANES), lambda b, *_: (b, 0)),
            scratch_shapes=[pltpu.VMEM((2, d, de), F32), pltpu.VMEM((2, d, de), F32), pltpu.VMEM((2, de, d), F32),
                            pltpu.VMEM((d, de), BF16), pltpu.VMEM((d, de), BF16), pltpu.VMEM((de, d), BF16),
                            pltpu.SemaphoreType.DMA((2, 3))]),
        out_shape=jax.ShapeDtypeStruct((p * xchunks, LANES), jnp.uint32),
        compiler_params=_cparams("arbitrary"), name="moe_experts",
    )(plan["be"], plan["blk_idx"], plan["first"], plan["parity"], plan["nxt1"], plan["nxt2"], plan["n_act"],
      xb, w_gate, w_up, w_down)
    return yb, plan["dest"]


def kernel(x, c, norm1_g, norm2_g, ada_w, ada_b, mix_in_w, na_rpb, na_out_w, fourier_out_w, branch_gate_w,
           branch_gate_b, mix_out_w, router_group_w, router_group_b, router_expert_w, router_expert_b,
           expert_w_gate, expert_w_up, expert_w_down, final_g):
    bsz, s, d = x.shape
    assert bsz == 1 and s % GRID_W == 0
    depth = ada_w.shape[0]
    na_w = NA_HEADS * NA_HEAD_DIM
    f_w = F_GROUPS * F_GROUP_DIM
    xs = x.reshape(s, d)
    mod = _ada_all(c, ada_w, ada_b)
    pad = ROUTER_LANES - N_GROUPS - N_EXPERTS
    wr_all = jnp.concatenate([router_group_w, router_expert_w, jnp.zeros((depth, d, pad), F32)], axis=2)
    wr_hi = wr_all.astype(BF16)
    wr_lo = (wr_all - wr_hi.astype(F32)).astype(BF16)
    wr_all = jnp.concatenate([wr_hi, wr_lo], axis=2)
    br_all = jnp.concatenate([router_group_b, router_expert_b, jnp.zeros((depth, pad), F32)], axis=1)
    moe = g2 = None
    for l in range(depth):
        sh1, sc1, g1, sh2, sc2, g2_l = [mod[l, :, i * d:(i + 1) * d] for i in range(N_MOD)]
        if moe is None:
            (h,) = _pre(xs, norm1_g[l], shift=sh1, scale=sc1)
        else:
            xs, h = _pre(xs, norm1_g[l], moe=moe, g2=g2, shift=sh1, scale=sc1)
        qkv = _mm([(h, mix_in_w, l, 0)], [], lambda prod: prod(0), 3 * na_w, BF16, "mix_in_qkv", tn=MM_TN_WIDE)
        u = _mm([(h, mix_in_w, l, 3 * na_w)], [], lambda prod: prod(0), f_w, jnp.uint32, "mix_in_fourier",
                tn=f_w)
        y_att = _attention(qkv, na_rpb[l])
        y_fft = _fourier(u)
        bias_bg = branch_gate_b[l].reshape(1, -1)
        mixed = _mm([(h, branch_gate_w, l, 0), (h, branch_gate_w, l, d),
                     (y_att, na_out_w, l, 0), (y_fft, fourier_out_w, l, 0)],
                    [(bias_bg, _rowvec, _vec_cols(0)), (bias_bg, _rowvec, _vec_cols(d))],
                    lambda prod, ba, bf: (_sigmoid(prod(0) + ba) * prod(2) + _sigmoid(prod(1) + bf) * prod(3)),
                    d, BF16, "gated_branch_mix", tn=MM_TN, tm=MM_TM_FUSED)
        xs = _mm([(mixed, mix_out_w, l, 0)],
                 [(xs, _tile, _at_cols(0)), (g1, _rowvec, _vec_cols(0))],
                 lambda prod, xr, g: xr + g * prod(0), d, F32, "mix_out", tn=MM_TN_WIDE)
        h2, wts, ids = _pre(xs, norm2_g[l], shift=sh2, scale=sc2, wr=wr_all[l], br=br_all[l].reshape(1, -1),
                            h_dtype=F32)
        yb, dest = _experts(h2, ids[:, :TOP_K], l, expert_w_gate, expert_w_up, expert_w_down)
        moe, g2 = (yb, dest, wts), g2_l
    _, out = _pre(xs, final_g, moe=moe, g2=g2, h_dtype=F32)
    return out.reshape(bsz, s, d)
```

```python
import functools
import math

import numpy as np
import jax
import jax.numpy as jnp
from jax import lax
from jax.experimental import pallas as pl
from jax.experimental.pallas import tpu as pltpu

F32 = jnp.float32
BF16 = jnp.bfloat16

GRID_W = 64
NA_HEADS = 16
NA_HEAD_DIM = 64
WIN_ROWS = 8
WIN_COLS = 16
F_GROUPS = 4
F_GROUP_DIM = 256
N_GROUPS = 4
EXPERTS_PER_GROUP = 8
N_EXPERTS = N_GROUPS * EXPERTS_PER_GROUP
TOP_K = 2
N_MOD = 6
EPS = 1e-6
NEG_INF = -1e30

LANES = 128
SUBLANES = 8

ADA_TN = 1536
ADA_ROWS = 256
PRE_TM = 512
MM_TM = 1024
MM_TM_FUSED = 512
MM_TN = 512
MM_TN_WIDE = 1024
ATTN_ROWS = 32
EXPERT_TB = 256
ROW_DMA_UNROLL = 8
FFT_JB = 8
FFT_KB = 16
ROUTER_LANES = LANES

VMEM_LIMIT = 56 * 1024 * 1024


def _cparams(*sem):
    return pltpu.CompilerParams(dimension_semantics=sem, vmem_limit_bytes=VMEM_LIMIT)


def _sigmoid(x):
    return 1.0 / (1.0 + jnp.exp(-x))


def _ada_kernel(c_ref, w_ref, b_ref, o_ref, cs_ref):
    c = c_ref[...]
    cs_ref[...] = c * _sigmoid(c)
    d, tn = w_ref.shape

    def body(i, acc):
        r0 = pl.multiple_of(i * ADA_ROWS, ADA_ROWS)
        prod = w_ref[pl.ds(r0, ADA_ROWS), :] * cs_ref[pl.ds(r0, ADA_ROWS), :]
        return acc + jnp.sum(prod.reshape(ADA_ROWS // SUBLANES, SUBLANES, tn), axis=0)

    acc = lax.fori_loop(0, d // ADA_ROWS, body, jnp.zeros((SUBLANES, tn), F32))
    o_ref[...] = jnp.sum(acc, axis=0, keepdims=True) + b_ref[...]


def _ada_all(c, ada_w, ada_b):
    nl, d, n = ada_w.shape
    tn = min(ADA_TN, n)
    return pl.pallas_call(
        _ada_kernel,
        grid=(nl, n // tn),
        in_specs=[
            pl.BlockSpec((d, 1), lambda l, j: (0, 0)),
            pl.BlockSpec((None, d, tn), lambda l, j: (l, 0, j)),
            pl.BlockSpec((None, 1, tn), lambda l, j: (l, 0, j)),
        ],
        out_specs=pl.BlockSpec((None, 1, tn), lambda l, j: (l, 0, j)),
        out_shape=jax.ShapeDtypeStruct((nl, 1, n), F32),
        scratch_shapes=[pltpu.VMEM((d, 1), F32)],
        compiler_params=_cparams("parallel", "parallel"),
        name="ada_proj",
    )(c.reshape(d, 1), ada_w, ada_b.reshape(nl, 1, n))


def _rms(x, g):
    return x * lax.rsqrt(jnp.mean(x * x, axis=-1, keepdims=True) + EPS) * g


def _router(h, wr_ref, br_ref):
    w_split = wr_ref[...]
    h_hi = h.astype(BF16)
    h_lo = (h - h_hi.astype(F32)).astype(BF16)
    dot = functools.partial(jnp.dot, preferred_element_type=F32)
    both = dot(h_hi, w_split)
    logits = both[:, :ROUTER_LANES] + (both[:, ROUTER_LANES:] + dot(h_lo, w_split[:, :ROUTER_LANES])) + br_ref[...]
    lane = lax.broadcasted_iota(jnp.int32, logits.shape, 1)
    big = jnp.int32(ROUTER_LANES)
    is_g = lane < N_GROUPS
    gl = jnp.where(is_g, logits, NEG_INF)
    gmax = jnp.max(gl, axis=-1, keepdims=True)
    gsum = jnp.sum(jnp.where(is_g, jnp.exp(gl - gmax), 0.0), axis=-1, keepdims=True)
    g_top_p = 1.0 / gsum
    g_top = jnp.min(jnp.where(is_g & (gl == gmax), lane, big), axis=-1, keepdims=True)
    e_lane = lane - N_GROUPS
    sel = (e_lane >= 0) & (e_lane < N_EXPERTS) & ((e_lane >> 3) == g_top)
    l1 = jnp.where(sel, logits, NEG_INF)
    m1 = jnp.max(l1, axis=-1, keepdims=True)
    i1 = jnp.min(jnp.where(sel & (l1 == m1), lane, big), axis=-1, keepdims=True)
    sel2 = sel & (lane != i1)
    l2 = jnp.where(sel2, logits, NEG_INF)
    m2 = jnp.max(l2, axis=-1, keepdims=True)
    i2 = jnp.min(jnp.where(sel2 & (l2 == m2), lane, big), axis=-1, keepdims=True)
    e21 = jnp.exp(m2 - m1)
    p1 = 1.0 / (1.0 + e21)
    p2 = e21 / (1.0 + e21)
    wts = jnp.where(lane == 0, g_top_p * p1, jnp.where(lane == 1, g_top_p * p2, 0.0))
    ids = jnp.where(lane == 0, i1 - N_GROUPS, jnp.where(lane == 1, i2 - N_GROUPS, 0))
    return wts, ids


def _load_token_major(ref, n_tok, chunks):
    return jnp.concatenate([ref[pl.ds(c, n_tok, stride=chunks), :] for c in range(chunks)], axis=1)


def _store_token_major(ref, val):
    n_tok, d = val.shape
    chunks = d // LANES
    for c in range(chunks):
        ref[pl.ds(c, n_tok, stride=chunks), :] = val[:, c * LANES:(c + 1) * LANES]


def _token_rows(tok, chunks):
    return pl.ds(pl.multiple_of(tok * chunks, chunks), chunks)


def _pre_kernel(*refs, combine, modulate, router, emit_h, h_dtype):
    it = iter(refs)
    if combine:
        dest_ref = next(it)
    x_ref = next(it)
    if combine:
        y_hbm, cw_ref, g2_ref = next(it), next(it), next(it)
    g_ref = next(it)
    if modulate:
        sh_ref, sc_ref = next(it), next(it)
    if router:
        wr_ref, br_ref = next(it), next(it)
    outs = [next(it) for _ in range(int(combine) + int(emit_h) + 2 * int(router))]
    x = x_ref[...]
    if combine:
        gbuf, gsem = next(it), next(it)
        i = pl.program_id(0)
        slot = i % 2
        tm, d = x_ref.shape
        chunks = d // LANES // 2

        def row_copy(step, sl, r, k):
            src = dest_ref[TOP_K * (step * tm + r) + k]
            return pltpu.make_async_copy(y_hbm.at[_token_rows(src, chunks)],
                                         gbuf.at[sl, k, _token_rows(r, chunks)], gsem.at[sl])

        def for_rows(fn):
            def body(r, carry):
                for k in range(TOP_K):
                    fn(r, k)
                return carry
            lax.fori_loop(0, tm, body, 0, unroll=ROW_DMA_UNROLL)

        @pl.when(i == 0)
        def _():
            for_rows(lambda r, k: row_copy(0, 0, r, k).start(priority=k))

        @pl.when(i + 1 < pl.num_programs(0))
        def _():
            for_rows(lambda r, k: row_copy(i + 1, 1 - slot, r, k).start(priority=k))

        for_rows(lambda r, k: row_copy(i, slot, r, k).wait())
        cw = cw_ref[...]
        y0 = _unpack_bf16_pairs(_load_token_major(gbuf.at[slot, 0], tm, chunks))
        y1 = _unpack_bf16_pairs(_load_token_major(gbuf.at[slot, 1], tm, chunks))
        x = x + g2_ref[...] * (cw[:, 0:1] * y0 + cw[:, 1:2] * y1)
        outs.pop(0)[...] = x
    h = _rms(x, g_ref[...])
    if modulate:
        h = h * (1.0 + sc_ref[...]) + sh_ref[...]
    if emit_h and router:
        _store_token_major(outs.pop(0), _pack_bf16_pairs(h))
    elif emit_h:
        outs.pop(0)[...] = h.astype(h_dtype)
    if router:
        wts, ids = _router(h, wr_ref, br_ref)
        outs.pop(0)[...] = wts
        outs.pop(0)[...] = ids


def _pre(x, g, *, moe=None, g2=None, shift=None, scale=None, wr=None, br=None, h_dtype=BF16):
    t, d = x.shape
    tm = min(PRE_TM, t)
    chunks = d // LANES
    combine, modulate, router = moe is not None, shift is not None, wr is not None
    row = pl.BlockSpec((tm, d), lambda i, *_: (i, 0))
    vec = pl.BlockSpec((1, d), lambda i, *_: (0, 0))
    lane_row = pl.BlockSpec((tm, ROUTER_LANES), lambda i, *_: (i, 0))
    prefetch, args, specs, scratch = [], [x], [row], []
    if combine:
        yb, dest, cw = moe
        prefetch.append(dest)
        args += [yb, cw, g2]
        specs += [pl.BlockSpec(memory_space=pl.ANY), lane_row, vec]
        scratch = [pltpu.VMEM((2, TOP_K, tm * chunks // 2, LANES), jnp.uint32), pltpu.SemaphoreType.DMA((2,))]
    args.append(g.reshape(1, d))
    specs.append(vec)
    if modulate:
        args += [shift, scale]
        specs += [vec, vec]
    if router:
        args += [wr, br]
        specs += [pl.BlockSpec(wr.shape, lambda i, *_: (0, 0)), pl.BlockSpec(br.shape, lambda i, *_: (0, 0))]
    out_shape, out_specs = [], []
    if combine:
        out_shape.append(jax.ShapeDtypeStruct((t, d), F32))
        out_specs.append(row)
    if router:
        out_shape.append(jax.ShapeDtypeStruct((t * chunks // 2, LANES), jnp.uint32))
        out_specs.append(pl.BlockSpec((tm * chunks // 2, LANES), lambda i, *_: (i, 0)))
    else:
        out_shape.append(jax.ShapeDtypeStruct((t, d), h_dtype))
        out_specs.append(row)
    if router:
        out_shape += [jax.ShapeDtypeStruct((t, ROUTER_LANES), F32), jax.ShapeDtypeStruct((t, ROUTER_LANES), jnp.int32)]
        out_specs += [lane_row, lane_row]
    kern = functools.partial(_pre_kernel, combine=combine, modulate=modulate, router=router,
                             emit_h=True, h_dtype=h_dtype)
    grid_spec = pltpu.PrefetchScalarGridSpec(
        num_scalar_prefetch=len(prefetch), grid=(t // tm,), in_specs=specs, out_specs=out_specs,
        scratch_shapes=scratch)
    return pl.pallas_call(
        kern, grid_spec=grid_spec, out_shape=out_shape,
        compiler_params=_cparams("arbitrary" if combine else "parallel"), name="pre_norm",
    )(*prefetch, *args)


def _mm_kernel(*refs, lhs_of, n_lhs, n_extras, epilogue):
    n_pairs = len(lhs_of)
    lhs = refs[:n_lhs]
    weights = refs[n_lhs:n_lhs + n_pairs]
    extras = refs[n_lhs + n_pairs:n_lhs + n_pairs + n_extras]
    o_ref = refs[n_lhs + n_pairs + n_extras]
    caches = refs[n_lhs + n_pairs + n_extras + 1:]

    @pl.when(pl.program_id(1) == 0)
    def _():
        for p in range(n_pairs):
            caches[p][...] = weights[p][...].astype(BF16)

    def product(p):
        a = lhs[lhs_of[p]][...]
        a = _unpack_bf16_pairs(a) if a.dtype == jnp.uint32 else a
        return jnp.dot(a.astype(BF16), caches[p][...], preferred_element_type=F32)

    out = epilogue(product, *[r[...] for r in extras])
    o_ref[...] = _pack_bf16_pairs(out) if o_ref.dtype == jnp.uint32 else out.astype(o_ref.dtype)


def _mm(pairs, extras, epilogue, n, out_dtype, name, tn=MM_TN, tm=MM_TM):
    m = pairs[0][0].shape[0]
    tm, tn = min(tm, m), min(tn, n)
    packed_out = out_dtype == jnp.uint32
    assert not packed_out or tn == n
    lhs, lhs_of, w_args, w_specs, scratch = [], [], [], [], []
    for a, w, layer, off in pairs:
        k = a.shape[1] * (2 if a.dtype == jnp.uint32 else 1)
        if not any(a is seen for seen in lhs):
            lhs.append(a)
        lhs_of.append([a is seen for seen in lhs].index(True))
        w_args.append(w)
        w_specs.append(pl.BlockSpec((None, k, tn), lambda j, i, layer=layer, off=off: (layer, 0, j + off // tn)))
        scratch.append(pltpu.VMEM((k, tn), BF16))
    args = lhs + w_args
    specs = [pl.BlockSpec((tm, a.shape[1]), lambda j, i: (i, 0)) for a in lhs] + w_specs
    for arr, bshape, imap in extras:
        args.append(arr)
        specs.append(pl.BlockSpec(bshape(tm, tn), imap(tn)))
    return pl.pallas_call(
        functools.partial(_mm_kernel, lhs_of=tuple(lhs_of), n_lhs=len(lhs), n_extras=len(extras),
                          epilogue=epilogue),
        grid=(n // tn, m // tm), in_specs=specs,
        out_specs=pl.BlockSpec((tm, tn // 2 if packed_out else tn), lambda j, i: (i, j)),
        out_shape=jax.ShapeDtypeStruct((m, n // 2 if packed_out else n), out_dtype),
        scratch_shapes=scratch,
        compiler_params=_cparams("arbitrary", "arbitrary"), name=name,
    )(*args)


def _tile(tm, tn):
    return (tm, tn)


def _rowvec(tm, tn):
    return (1, tn)


def _at_cols(off):
    return lambda tn: (lambda j, i: (i, j + off // tn))


def _vec_cols(off):
    return lambda tn: (lambda j, i: (0, j + off // tn))


def _attn_bias_fill(rpb_ref, b_ref, head0):
    n_dr, n_dc = 2 * WIN_ROWS - 1, 2 * WIN_COLS - 1
    qc = lax.broadcasted_iota(jnp.int32, (GRID_W, 2 * GRID_W), 0)
    kc = lax.broadcasted_iota(jnp.int32, (GRID_W, 2 * GRID_W), 1) & (GRID_W - 1)
    cs = jnp.clip(qc - WIN_COLS // 2, 0, GRID_W - WIN_COLS)
    inside = (kc >= cs) & (kc < cs + WIN_COLS)
    dc = jnp.clip(kc - qc, -(WIN_COLS - 1), WIN_COLS - 1) + (WIN_COLS - 1)
    lane = lax.broadcasted_iota(jnp.int32, (SUBLANES, 2 * GRID_W), 1)
    for h in range(2):
        base = (head0 + h) * (n_dr * n_dc)
        for dr in range(n_dr):
            row = jnp.zeros((SUBLANES, 2 * GRID_W), F32)
            for d in range(n_dc):
                row = jnp.where(lane == d, rpb_ref[base + dr * n_dc + d], row)
            table = jnp.concatenate([row] * (GRID_W // SUBLANES), axis=0)
            t = jnp.where(inside, jnp.take_along_axis(table, dc, axis=1), NEG_INF)
            for var in range(WIN_ROWS):
                j = dr - var
                if 0 <= j < WIN_ROWS:
                    c0 = j * GRID_W
                    lo = c0 % (2 * GRID_W)
                    b_ref[var, h * GRID_W:(h + 1) * GRID_W, c0:c0 + GRID_W] = t[:, lo:lo + GRID_W]


def _attn_kernel(rpb_ref, q_ref, k_ref, v_ref, o_ref, b_ref, s_ref, *, n_rows):
    rb = pl.program_id(1)
    n_q = q_ref.shape[0]
    wkeys = WIN_ROWS * GRID_W
    dh = NA_HEAD_DIM
    lane = lax.broadcasted_iota(jnp.int32, (GRID_W, 2 * dh), 1)
    first = lane < dh

    @pl.when(rb == 0)
    def _():
        _attn_bias_fill(rpb_ref, b_ref, 2 * pl.program_id(0))

    def window(i):
        r = rb * n_q + i
        rs = jnp.clip(r - WIN_ROWS // 2, 0, n_rows - WIN_ROWS)
        return rs, rs - r + (WIN_ROWS - 1)

    for i in range(n_q):
        rs, var = window(i)
        kw = k_ref[pl.ds(rs, WIN_ROWS)].reshape(wkeys, 2 * dh)
        q = q_ref[i] * jnp.asarray(dh ** -0.5, BF16)
        zero = jnp.zeros_like(q)
        q2 = jnp.concatenate([jnp.where(first, q, zero), jnp.where(first, zero, q)], axis=0)
        s = lax.dot_general(q2, kw, (((1,), (1,)), ((), ())), preferred_element_type=F32)
        b = b_ref[var]
        s_ref[i] = s + b
    for i in range(n_q):
        rs, _ = window(i)
        vw = v_ref[pl.ds(rs, WIN_ROWS)].reshape(wkeys, 2 * dh)
        s = s_ref[i]
        m = jnp.max(s, axis=-1, keepdims=True)
        p = jnp.exp(s - m)
        l = jnp.sum(p, axis=-1, keepdims=True)
        o = jnp.dot(p.astype(BF16), vw, preferred_element_type=F32) / l
        o_ref[i] = jnp.where(first, o[:GRID_W], o[GRID_W:]).astype(o_ref.dtype)


def _attention(proj, rpb):
    t, width = proj.shape
    n_rows = t // GRID_W
    pairs = NA_HEADS // 2
    p3 = proj.reshape(n_rows, GRID_W, width)
    blk = 2 * NA_HEAD_DIM
    rows = min(ATTN_ROWS, n_rows)
    wkeys = WIN_ROWS * GRID_W
    out = pl.pallas_call(
        functools.partial(_attn_kernel, n_rows=n_rows),
        grid_spec=pltpu.PrefetchScalarGridSpec(
            num_scalar_prefetch=1, grid=(pairs, n_rows // rows),
            in_specs=[
                pl.BlockSpec((rows, GRID_W, blk), lambda hp, rb, _: (rb, 0, hp)),
                pl.BlockSpec((n_rows, GRID_W, blk), lambda hp, rb, _: (0, 0, pairs + hp)),
                pl.BlockSpec((n_rows, GRID_W, blk), lambda hp, rb, _: (0, 0, 2 * pairs + hp)),
            ],
            out_specs=pl.BlockSpec((rows, GRID_W, blk), lambda hp, rb, _: (rb, 0, hp)),
            scratch_shapes=[pltpu.VMEM((WIN_ROWS, 2 * GRID_W, wkeys), F32),
                            pltpu.VMEM((rows, 2 * GRID_W, wkeys), F32)]),
        out_shape=jax.ShapeDtypeStruct((n_rows, GRID_W, NA_HEADS * NA_HEAD_DIM), BF16),
        compiler_params=_cparams("arbitrary", "arbitrary"), name="na_attention",
    )(rpb.reshape(-1), p3, p3, p3)
    return out.reshape(t, NA_HEADS * NA_HEAD_DIM)


def _dft_tables(n1, n2, dc):
    s = n1 * n2
    a = np.arange(dc, dtype=np.float64)
    ang_c = 2 * np.pi * np.outer(a, a) / dc
    w_c = np.concatenate([np.cos(ang_c), -np.sin(ang_c)], axis=1)
    a1 = np.arange(n1, dtype=np.float64)
    ang1 = 2 * np.pi * np.outer(a1, a1) / n1
    c1, s1 = np.cos(ang1), np.sin(ang1)
    m1 = np.block([[c1, s1], [-s1, c1]])
    a2 = np.arange(n2, dtype=np.float64)
    ang_t = 2 * np.pi * np.outer(a2, a1) / s
    tw_c = np.repeat(np.cos(ang_t)[:, :, None], LANES, axis=2)
    tw_s = np.repeat(np.sin(ang_t)[:, :, None], LANES, axis=2)
    ang2 = 2 * np.pi * np.outer(a2, a2) / n2
    m2 = np.concatenate([np.cos(ang2), np.sin(ang2)], axis=1)
    return (jnp.asarray(w_c, BF16), jnp.asarray(m1, BF16), jnp.asarray(tw_c, F32), jnp.asarray(tw_s, F32),
            jnp.asarray(m2, BF16))


def _fft1_kernel(u_hbm, wc_ref, m1_ref, tc_ref, ts_ref, br_hbm, bi_hbm,
                 ubuf, zz_ref, obuf, isem, osem, *, n1, dc):
    s = pl.program_id(0)
    ns = pl.num_programs(0)
    jb = ubuf.shape[1]
    slot = s % 2

    def in_copy(step, sl, jj):
        return pltpu.make_async_copy(u_hbm.at[:, step * jb + jj], ubuf.at[sl, jj], isem.at[sl])

    def out_copies(step, sl, jj):
        j = step * jb + jj
        return (pltpu.make_async_copy(obuf.at[sl, 0, jj], br_hbm.at[:, j], osem.at[sl]),
                pltpu.make_async_copy(obuf.at[sl, 1, jj], bi_hbm.at[:, j], osem.at[sl]))

    @pl.when(s == 0)
    def _():
        for jj in range(jb):
            in_copy(0, 0, jj).start()

    @pl.when(s + 1 < ns)
    def _():
        for jj in range(jb):
            in_copy(s + 1, 1 - slot, jj).start()

    for jj in range(jb):
        in_copy(s, slot, jj).wait()

    @pl.when(s >= 2)
    def _():
        for jj in range(jb):
            for cp in out_copies(s - 2, slot, jj):
                cp.wait()

    c = 2 * ubuf.shape[3]
    u = _unpack_bf16_pairs(ubuf[slot].reshape(jb * n1, c // 2)).astype(BF16)
    for g in range(c // dc):
        z = jnp.dot(u[:, g * dc:(g + 1) * dc], wc_ref[...], preferred_element_type=F32)
        z = z.astype(BF16)
        for jj in range(jb):
            zz_ref[jj, :n1, g * dc:(g + 1) * dc] = z[jj * n1:(jj + 1) * n1, :dc]
            zz_ref[jj, n1:, g * dc:(g + 1) * dc] = z[jj * n1:(jj + 1) * n1, dc:]
    for jj in range(jb):
        a = jnp.dot(m1_ref[...], zz_ref[jj], preferred_element_type=F32)
        ar, ai = a[:n1], a[n1:]
        tc = jnp.concatenate([tc_ref[jj]] * (c // LANES), axis=1)
        ts = jnp.concatenate([ts_ref[jj]] * (c // LANES), axis=1)
        obuf[slot, 0, jj] = _pack_bf16_pairs(ar * tc + ai * ts)
        obuf[slot, 1, jj] = _pack_bf16_pairs(ai * tc - ar * ts)
    for jj in range(jb):
        for cp in out_copies(s, slot, jj):
            cp.start()

    @pl.when(s == ns - 1)
    def _():
        for jj in range(jb):
            for cp in out_copies(s, slot, jj):
                cp.wait()

        @pl.when(s >= 1)
        def _():
            for jj in range(jb):
                for cp in out_copies(s - 1, 1 - slot, jj):
                    cp.wait()


def _fft2_kernel(br_ref, bi_ref, m2_ref, y_hbm, obuf, osem, *, n2, scale):
    s = pl.program_id(0)
    ns = pl.num_programs(0)
    kb = obuf.shape[1]
    slot = s % 2

    def out_copy(step, sl, kk):
        return pltpu.make_async_copy(obuf.at[sl, kk], y_hbm.at[:, step * kb + kk], osem.at[sl])

    @pl.when(s >= 2)
    def _():
        for kk in range(kb):
            out_copy(s - 2, slot, kk).wait()

    for kk in range(kb):
        rows = slice(kk * n2, (kk + 1) * n2)
        bb = _unpack_bf16_pairs(jnp.concatenate([br_ref[rows, :], bi_ref[rows, :]], axis=0)).astype(BF16)
        obuf[slot, kk] = _pack_bf16_pairs(jnp.dot(m2_ref[...], bb, preferred_element_type=F32) * scale)
    for kk in range(kb):
        out_copy(s, slot, kk).start()

    @pl.when(s == ns - 1)
    def _():
        for kk in range(kb):
            out_copy(s, slot, kk).wait()

        @pl.when(s >= 1)
        def _():
            for kk in range(kb):
                out_copy(s - 1, 1 - slot, kk).wait()


def _fourier(u):
    t, hw = u.shape
    fw = 2 * hw
    n2 = GRID_W
    n1 = t // n2
    dc = F_GROUP_DIM
    w_c, m1, tw_c, tw_s, m2 = _dft_tables(n1, n2, dc)
    jb = min(FFT_JB, n2)
    kb = min(FFT_KB, n1)
    any_spec = pl.BlockSpec(memory_space=pl.ANY)
    tw_spec = pl.BlockSpec((jb, n1, LANES), lambda s: (s, 0, 0))
    b_re, b_im = pl.pallas_call(
        functools.partial(_fft1_kernel, n1=n1, dc=dc),
        grid=(n2 // jb,),
        in_specs=[
            any_spec,
            pl.BlockSpec((dc, 2 * dc), lambda s: (0, 0)),
            pl.BlockSpec((2 * n1, 2 * n1), lambda s: (0, 0)),
            tw_spec, tw_spec,
        ],
        out_specs=[any_spec, any_spec],
        out_shape=[jax.ShapeDtypeStruct((n1, n2, hw), jnp.uint32)] * 2,
        scratch_shapes=[
            pltpu.VMEM((2, jb, n1, hw), jnp.uint32),
            pltpu.VMEM((jb, 2 * n1, fw), BF16),
            pltpu.VMEM((2, 2, jb, n1, hw), jnp.uint32),
            pltpu.SemaphoreType.DMA((2,)),
            pltpu.SemaphoreType.DMA((2,)),
        ],
        compiler_params=_cparams("arbitrary"), name="fourier_stage1",
    )(u.reshape(n1, n2, hw), w_c, m1, tw_c, tw_s)
    scale = 1.0 / math.sqrt(t * dc)
    in_blk = pl.BlockSpec((kb * n2, hw), lambda s: (s, 0))
    y = pl.pallas_call(
        functools.partial(_fft2_kernel, n2=n2, scale=scale),
        grid=(n1 // kb,),
        in_specs=[in_blk, in_blk, pl.BlockSpec((n2, 2 * n2), lambda s: (0, 0))],
        out_specs=any_spec,
        out_shape=jax.ShapeDtypeStruct((n2, n1, hw), jnp.uint32),
        scratch_shapes=[pltpu.VMEM((2, kb, n2, hw), jnp.uint32), pltpu.SemaphoreType.DMA((2,))],
        compiler_params=_cparams("arbitrary"), name="fourier_stage2",
    )(b_re.reshape(t, hw), b_im.reshape(t, hw), m2)
    return y.reshape(t, hw)


def _dispatch_kernel(dest_ref, pad0_ref, padn_ref, na_ref, h_ref, xb_hbm, zbuf, sem, zsem, *, chunks):
    i = pl.program_id(0)
    tm = h_ref.shape[0] // chunks
    tb = zbuf.shape[0] // chunks
    nb = xb_hbm.shape[0] // (tb * chunks)

    def tail_copies(fn):
        def body(b, carry):
            fn(pltpu.make_async_copy(zbuf, xb_hbm.at[_token_rows(b, tb * chunks)], zsem))
            return carry
        lax.fori_loop(na_ref[0], nb, body, 0)

    def pad_copies(e, fn):
        pos = pad0_ref[e]
        n = padn_ref[e]
        bit = tb // 2
        while bit >= 1:
            @pl.when((n & bit) != 0)
            def _(pos=pos, bit=bit):
                rows = pl.ds(pl.multiple_of(pos * chunks, chunks), bit * chunks)
                fn(pltpu.make_async_copy(zbuf.at[pl.ds(0, bit * chunks)], xb_hbm.at[rows], zsem))
            pos = pos + (n & bit)
            bit //= 2

    @pl.when(i == 0)
    def _():
        zbuf[...] = jnp.zeros_like(zbuf)
        lax.fori_loop(0, N_EXPERTS, lambda e, c: (pad_copies(e, lambda cp: cp.start()), c)[1], 0)
        tail_copies(lambda cp: cp.start())

    def row_copy(r, k):
        dst = dest_ref[TOP_K * (i * tm + r) + k]
        return pltpu.make_async_copy(h_ref.at[_token_rows(r, chunks)], xb_hbm.at[_token_rows(dst, chunks)], sem)

    def for_rows(fn):
        def body(r, carry):
            for k in range(TOP_K):
                fn(r, k)
            return carry
        lax.fori_loop(0, tm, body, 0, unroll=ROW_DMA_UNROLL)

    for_rows(lambda r, k: row_copy(r, k).start(priority=k))
    for_rows(lambda r, k: row_copy(r, k).wait())

    @pl.when(i == 0)
    def _():
        lax.fori_loop(0, N_EXPERTS, lambda e, c: (pad_copies(e, lambda cp: cp.wait()), c)[1], 0)
        tail_copies(lambda cp: cp.wait())


def _pack_bf16_pairs(h):
    half = h.shape[1] // 2
    bits = lax.bitcast_convert_type(h.astype(BF16).astype(F32), jnp.uint32)
    return bits[:, half:] | (bits[:, :half] >> 16)


def _unpack_bf16_pairs(p):
    lo = lax.bitcast_convert_type(p << 16, F32)
    hi = lax.bitcast_convert_type(p & jnp.uint32(0xFFFF0000), F32)
    return jnp.concatenate([lo, hi], axis=1)


def _expert_kernel(be_ref, bi_ref, first_ref, par_ref, nxt1_ref, nxt2_ref, na_ref,
                   x_ref, wg_hbm, wu_hbm, wd_hbm, y_ref,
                   wg_st, wu_st, wd_st, wg_c, wu_c, wd_c, wsem, *, layer):
    b = pl.program_id(0)
    active = b < na_ref[0]
    d = wg_c.shape[0]
    xchunks = d // LANES // 2
    tb = x_ref.shape[0] // xchunks

    def weight_copies(e, slot):
        return [pltpu.make_async_copy(src.at[layer, e], dst.at[slot], wsem.at[slot, n])
                for n, (src, dst) in enumerate(((wg_hbm, wg_st), (wu_hbm, wu_st), (wd_hbm, wd_st)))]

    def start_weights(e, slot):
        for cp, priority in zip(weight_copies(e, slot), (1, 1, 0)):
            cp.start(priority=priority)

    @pl.when(b == 0)
    def _():
        start_weights(be_ref[0], 0)

        @pl.when(nxt1_ref[0] >= 0)
        def _():
            start_weights(nxt1_ref[0], 1)

    is_first = first_ref[b] == 1

    def block(refresh):
        slot = par_ref[b]
        x = _unpack_bf16_pairs(_load_token_major(x_ref, tb, xchunks)).astype(BF16)
        if refresh:
            wg_c[...] = wg_st[slot].astype(BF16)
        gate = jnp.dot(x, wg_c[...], preferred_element_type=F32)
        if refresh:
            wu_c[...] = wu_st[slot].astype(BF16)
        up = jnp.dot(x, wu_c[...], preferred_element_type=F32)
        act = (gate * _sigmoid(gate) * up).astype(BF16)
        if refresh:
            wd_c[...] = wd_st[slot].astype(BF16)
        _store_token_major(y_ref, _pack_bf16_pairs(jnp.dot(act, wd_c[...], preferred_element_type=F32)))

    @pl.when(active & is_first)
    def _():
        slot = par_ref[b]
        for cp in weight_copies(be_ref[b], slot):
            cp.wait()
        block(True)

        @pl.when(nxt2_ref[b] >= 0)
        def _():
            start_weights(nxt2_ref[b], slot)

    @pl.when(active & jnp.logical_not(is_first))
    def _():
        block(False)

    @pl.when(jnp.logical_not(active))
    def _():
        y_ref[...] = jnp.zeros_like(y_ref)


def _moe_plan(experts, t, tb):
    a = t * TOP_K
    e_flat = experts.reshape(a)
    lanes = jnp.arange(N_EXPERTS, dtype=jnp.int32)[None, :]
    onehot = (e_flat[:, None] == lanes).astype(jnp.int32)
    csum = jnp.cumsum(onehot, axis=0)
    counts = csum[-1]
    pcounts = (counts + tb - 1) // tb * tb
    pends = jnp.cumsum(pcounts)
    pstarts = pends - pcounts
    dest = jnp.sum(onehot * (pstarts[None, :] + csum - 1), axis=1).astype(jnp.int32)
    nb = (a + N_EXPERTS * tb) // tb
    n_act = (pends[-1] // tb).astype(jnp.int32)
    blk = jnp.arange(nb, dtype=jnp.int32)
    blk_idx = jnp.minimum(blk, n_act - 1)
    be = jnp.sum((pends[None, :] <= (blk_idx * tb)[:, None]).astype(jnp.int32), axis=1)
    be = jnp.minimum(be, N_EXPERTS - 1).astype(jnp.int32)
    first = ((blk * tb == pstarts[be]) & (blk < n_act)).astype(jnp.int32)
    parity = ((jnp.cumsum(first) - 1) & 1).astype(jnp.int32)

    def next_expert(e):
        after = pends[jnp.maximum(e, 0)] // tb
        return jnp.where((e >= 0) & (after < n_act), be[jnp.minimum(after, nb - 1)], -1).astype(jnp.int32)

    nxt1 = next_expert(be)
    nxt2 = next_expert(nxt1)
    plan = dict(dest=dest, be=be, blk_idx=blk_idx, first=first, parity=parity, nxt1=nxt1, nxt2=nxt2,
                n_act=n_act.reshape(1),
                pad0=(pstarts + counts).astype(jnp.int32), padn=(pcounts - counts).astype(jnp.int32))
    return plan


def _experts(h, experts, layer, w_gate, w_up, w_down):
    d, de = w_gate.shape[2:]
    chunks = d // LANES
    xchunks = chunks // 2
    t = h.shape[0] // xchunks
    tb = EXPERT_TB
    tm = min(PRE_TM, t)
    plan = _moe_plan(experts, t, tb)
    nb = plan["be"].shape[0]
    p = nb * tb
    xb = pl.pallas_call(
        functools.partial(_dispatch_kernel, chunks=xchunks),
        grid_spec=pltpu.PrefetchScalarGridSpec(
            num_scalar_prefetch=4, grid=(t // tm,),
            in_specs=[pl.BlockSpec((tm * xchunks, LANES), lambda i, *_: (i, 0))],
            out_specs=pl.BlockSpec(memory_space=pl.ANY),
            scratch_shapes=[pltpu.VMEM((tb * xchunks, LANES), h.dtype),
                            pltpu.SemaphoreType.DMA(()), pltpu.SemaphoreType.DMA(())]),
        out_shape=jax.ShapeDtypeStruct((p * xchunks, LANES), h.dtype),
        compiler_params=_cparams("arbitrary"), name="moe_dispatch",
    )(plan["dest"], plan["pad0"], plan["padn"], plan["n_act"], h)
    hbm = pl.BlockSpec(memory_space=pl.ANY)
    yb = pl.pallas_call(
        functools.partial(_expert_kernel, layer=layer),
        grid_spec=pltpu.PrefetchScalarGridSpec(
            num_scalar_prefetch=7, grid=(nb,),
            in_specs=[pl.BlockSpec((tb * xchunks, LANES), lambda b, be, bi, *_: (bi[b], 0)), hbm, hbm, hbm],
            out_specs=pl.BlockSpec((tb * xchunks, LANES), lambda b, *_: (b, 0)),
            scratch_shapes=[pltpu.VMEM((2, d, de), F32), pltpu.VMEM((2, d, de), F32), pltpu.VMEM((2, de, d), F32),
                            pltpu.VMEM((d, de), BF16), pltpu.VMEM((d, de), BF16), pltpu.VMEM((de, d), BF16),
                            pltpu.SemaphoreType.DMA((2, 3))]),
        out_shape=jax.ShapeDtypeStruct((p * xchunks, LANES), jnp.uint32),
        compiler_params=_cparams("arbitrary"), name="moe_experts",
    )(plan["be"], plan["blk_idx"], plan["first"], plan["parity"], plan["nxt1"], plan["nxt2"], plan["n_act"],
      xb, w_gate, w_up, w_down)
    return yb, plan["dest"]


def kernel(x, c, norm1_g, norm2_g, ada_w, ada_b, mix_in_w, na_rpb, na_out_w, fourier_out_w, branch_gate_w,
           branch_gate_b, mix_out_w, router_group_w, router_group_b, router_expert_w, router_expert_b,
           expert_w_gate, expert_w_up, expert_w_down, final_g):
    bsz, s, d = x.shape
    assert bsz == 1 and s % GRID_W == 0
    depth = ada_w.shape[0]
    na_w = NA_HEADS * NA_HEAD_DIM
    f_w = F_GROUPS * F_GROUP_DIM
    xs = x.reshape(s, d)
    mod = _ada_all(c, ada_w, ada_b)
    pad = ROUTER_LANES - N_GROUPS - N_EXPERTS
    wr_all = jnp.concatenate([router_group_w, router_expert_w, jnp.zeros((depth, d, pad), F32)], axis=2)
    wr_hi = wr_all.astype(BF16)
    wr_lo = (wr_all - wr_hi.astype(F32)).astype(BF16)
    wr_all = jnp.concatenate([wr_hi, wr_lo], axis=2)
    br_all = jnp.concatenate([router_group_b, router_expert_b, jnp.zeros((depth, pad), F32)], axis=1)
    moe = g2 = None
    for l in range(depth):
        sh1, sc1, g1, sh2, sc2, g2_l = [mod[l, :, i * d:(i + 1) * d] for i in range(N_MOD)]
        if moe is None:
            (h,) = _pre(xs, norm1_g[l], shift=sh1, scale=sc1)
        else:
            xs, h = _pre(xs, norm1_g[l], moe=moe, g2=g2, shift=sh1, scale=sc1)
        qkv = _mm([(h, mix_in_w, l, 0)], [], lambda prod: prod(0), 3 * na_w, BF16, "mix_in_qkv", tn=MM_TN_WIDE)
        u = _mm([(h, mix_in_w, l, 3 * na_w)], [], lambda prod: prod(0), f_w, jnp.uint32, "mix_in_fourier",
                tn=f_w)
        y_att = _attention(qkv, na_rpb[l])
        y_fft = _fourier(u)
        bias_bg = branch_gate_b[l].reshape(1, -1)
        mixed = _mm([(h, branch_gate_w, l, 0), (h, branch_gate_w, l, d),
                     (y_att, na_out_w, l, 0), (y_fft, fourier_out_w, l, 0)],
                    [(bias_bg, _rowvec, _vec_cols(0)), (bias_bg, _rowvec, _vec_cols(d))],
                    lambda prod, ba, bf: (_sigmoid(prod(0) + ba) * prod(2) + _sigmoid(prod(1) + bf) * prod(3)),
                    d, BF16, "gated_branch_mix", tn=MM_TN, tm=MM_TM_FUSED)
        xs = _mm([(mixed, mix_out_w, l, 0)],
                 [(xs, _tile, _at_cols(0)), (g1, _rowvec, _vec_cols(0))],
                 lambda prod, xr, g: xr + g * prod(0), d, F32, "mix_out", tn=MM_TN_WIDE)
        h2, wts, ids = _pre(xs, norm2_g[l], shift=sh2, scale=sc2, wr=wr_all[l], br=br_all[l].reshape(1, -1),
                            h_dtype=F32)
        yb, dest = _experts(h2, ids[:, :TOP_K], l, expert_w_gate, expert_w_up, expert_w_down)
        moe, g2 = (yb, dest, wts), g2_l
    _, out = _pre(xs, final_g, moe=moe, g2=g2, h_dtype=F32)
    return out.reshape(bsz, s, d)
```

```python
import functools
import math

import numpy as np
import jax
import jax.numpy as jnp
from jax import lax
from jax.experimental import pallas as pl
from jax.experimental.pallas import tpu as pltpu

F32 = jnp.float32
BF16 = jnp.bfloat16

GRID_W = 64
NA_HEADS = 16
NA_HEAD_DIM = 64
WIN_ROWS = 8
WIN_COLS = 16
F_GROUPS = 4
F_GROUP_DIM = 256
N_GROUPS = 4
EXPERTS_PER_GROUP = 8
N_EXPERTS = N_GROUPS * EXPERTS_PER_GROUP
TOP_K = 2
N_MOD = 6
EPS = 1e-6
NEG_INF = -1e30

LANES = 128
SUBLANES = 8

ADA_TN = 1536
ADA_ROWS = 256
PRE_TM = 512
COMBINE_TM = 256
MM_TM = 1024
MM_TM_FUSED = 512
MM_TN = 512
MM_TN_WIDE = 1024
ATTN_ROWS = 32
EXPERT_TB = 256
ROW_DMA_UNROLL = 8
FFT_JB = 8
FFT_KB = 16
ROUTER_LANES = LANES

VMEM_LIMIT = 56 * 1024 * 1024


def _cparams(*sem):
    return pltpu.CompilerParams(dimension_semantics=sem, vmem_limit_bytes=VMEM_LIMIT)


def _sigmoid(x):
    return 1.0 / (1.0 + jnp.exp(-x))


def _ada_kernel(c_ref, w_ref, b_ref, o_ref, cs_ref):
    c = c_ref[...]
    cs_ref[...] = c * _sigmoid(c)
    d, tn = w_ref.shape

    def body(i, acc):
        r0 = pl.multiple_of(i * ADA_ROWS, ADA_ROWS)
        prod = w_ref[pl.ds(r0, ADA_ROWS), :] * cs_ref[pl.ds(r0, ADA_ROWS), :]
        return acc + jnp.sum(prod.reshape(ADA_ROWS // SUBLANES, SUBLANES, tn), axis=0)

    acc = lax.fori_loop(0, d // ADA_ROWS, body, jnp.zeros((SUBLANES, tn), F32))
    o_ref[...] = jnp.sum(acc, axis=0, keepdims=True) + b_ref[...]


def _ada_all(c, ada_w, ada_b):
    nl, d, n = ada_w.shape
    tn = min(ADA_TN, n)
    return pl.pallas_call(
        _ada_kernel,
        grid=(nl, n // tn),
        in_specs=[
            pl.BlockSpec((d, 1), lambda l, j: (0, 0)),
            pl.BlockSpec((None, d, tn), lambda l, j: (l, 0, j)),
            pl.BlockSpec((None, 1, tn), lambda l, j: (l, 0, j)),
        ],
        out_specs=pl.BlockSpec((None, 1, tn), lambda l, j: (l, 0, j)),
        out_shape=jax.ShapeDtypeStruct((nl, 1, n), F32),
        scratch_shapes=[pltpu.VMEM((d, 1), F32)],
        compiler_params=_cparams("parallel", "parallel"),
        name="ada_proj",
    )(c.reshape(d, 1), ada_w, ada_b.reshape(nl, 1, n))


def _rms(x, g):
    return x * lax.rsqrt(jnp.mean(x * x, axis=-1, keepdims=True) + EPS) * g


def _router(h, wr_ref, br_ref):
    w_split = wr_ref[...]
    h_hi = h.astype(BF16)
    h_lo = (h - h_hi.astype(F32)).astype(BF16)
    dot = functools.partial(jnp.dot, preferred_element_type=F32)
    both = dot(h_hi, w_split)
    logits = both[:, :ROUTER_LANES] + (both[:, ROUTER_LANES:] + dot(h_lo, w_split[:, :ROUTER_LANES])) + br_ref[...]
    lane = lax.broadcasted_iota(jnp.int32, logits.shape, 1)
    big = jnp.int32(ROUTER_LANES)
    is_g = lane < N_GROUPS
    gl = jnp.where(is_g, logits, NEG_INF)
    gmax = jnp.max(gl, axis=-1, keepdims=True)
    gsum = jnp.sum(jnp.where(is_g, jnp.exp(gl - gmax), 0.0), axis=-1, keepdims=True)
    g_top_p = 1.0 / gsum
    g_top = jnp.min(jnp.where(is_g & (gl == gmax), lane, big), axis=-1, keepdims=True)
    e_lane = lane - N_GROUPS
    sel = (e_lane >= 0) & (e_lane < N_EXPERTS) & ((e_lane >> 3) == g_top)
    l1 = jnp.where(sel, logits, NEG_INF)
    m1 = jnp.max(l1, axis=-1, keepdims=True)
    i1 = jnp.min(jnp.where(sel & (l1 == m1), lane, big), axis=-1, keepdims=True)
    sel2 = sel & (lane != i1)
    l2 = jnp.where(sel2, logits, NEG_INF)
    m2 = jnp.max(l2, axis=-1, keepdims=True)
    i2 = jnp.min(jnp.where(sel2 & (l2 == m2), lane, big), axis=-1, keepdims=True)
    e21 = jnp.exp(m2 - m1)
    p1 = 1.0 / (1.0 + e21)
    p2 = e21 / (1.0 + e21)
    wts = jnp.where(lane == 0, g_top_p * p1, jnp.where(lane == 1, g_top_p * p2, 0.0))
    ids = jnp.where(lane == 0, i1 - N_GROUPS, jnp.where(lane == 1, i2 - N_GROUPS, 0))
    return wts, ids


def _load_token_major(ref, n_tok, chunks):
    return jnp.concatenate([ref[pl.ds(c, n_tok, stride=chunks), :] for c in range(chunks)], axis=1)


def _store_token_major(ref, val):
    n_tok, d = val.shape
    chunks = d // LANES
    for c in range(chunks):
        ref[pl.ds(c, n_tok, stride=chunks), :] = val[:, c * LANES:(c + 1) * LANES]


def _token_rows(tok, chunks):
    return pl.ds(pl.multiple_of(tok * chunks, chunks), chunks)


def _pre_kernel(*refs, combine, modulate, router, emit_h, h_dtype):
    it = iter(refs)
    if combine:
        dest_ref = next(it)
    x_ref = next(it)
    if combine:
        y_hbm, cw_ref, g2_ref = next(it), next(it), next(it)
    g_ref = next(it)
    if modulate:
        sh_ref, sc_ref = next(it), next(it)
    if router:
        wr_ref, br_ref = next(it), next(it)
    outs = [next(it) for _ in range(int(combine) + int(emit_h) + 2 * int(router))]
    x = x_ref[...]
    if combine:
        gbuf, gsem = next(it), next(it)
        i = pl.program_id(0)
        slot = i % 2
        tm, d = x_ref.shape
        chunks = d // LANES // 2

        def row_copy(step, sl, r, k):
            src = dest_ref[TOP_K * (step * tm + r) + k]
            return pltpu.make_async_copy(y_hbm.at[_token_rows(src, chunks)],
                                         gbuf.at[sl, k, _token_rows(r, chunks)], gsem.at[sl])

        def for_rows(fn):
            def body(r, carry):
                for k in range(TOP_K):
                    fn(r, k)
                return carry
            lax.fori_loop(0, tm, body, 0, unroll=ROW_DMA_UNROLL)

        @pl.when(i == 0)
        def _():
            for_rows(lambda r, k: row_copy(0, 0, r, k).start(priority=k))

        @pl.when(i + 1 < pl.num_programs(0))
        def _():
            for_rows(lambda r, k: row_copy(i + 1, 1 - slot, r, k).start(priority=k))

        for_rows(lambda r, k: row_copy(i, slot, r, k).wait())
        cw = cw_ref[...]
        y0 = _unpack_bf16_pairs(_load_token_major(gbuf.at[slot, 0], tm, chunks))
        y1 = _unpack_bf16_pairs(_load_token_major(gbuf.at[slot, 1], tm, chunks))
        x = x + g2_ref[...] * (cw[:, 0:1] * y0 + cw[:, 1:2] * y1)
        outs.pop(0)[...] = x
    h = _rms(x, g_ref[...])
    if modulate:
        h = h * (1.0 + sc_ref[...]) + sh_ref[...]
    if emit_h and router:
        _store_token_major(outs.pop(0), _pack_bf16_pairs(h))
    elif emit_h:
        outs.pop(0)[...] = h.astype(h_dtype)
    if router:
        wts, ids = _router(h, wr_ref, br_ref)
        outs.pop(0)[...] = wts
        outs.pop(0)[...] = ids


def _pre(x, g, *, moe=None, g2=None, shift=None, scale=None, wr=None, br=None, h_dtype=BF16):
    t, d = x.shape
    combine, modulate, router = moe is not None, shift is not None, wr is not None
    tm = min(COMBINE_TM if combine else PRE_TM, t)
    chunks = d // LANES
    row = pl.BlockSpec((tm, d), lambda i, *_: (i, 0))
    vec = pl.BlockSpec((1, d), lambda i, *_: (0, 0))
    lane_row = pl.BlockSpec((tm, ROUTER_LANES), lambda i, *_: (i, 0))
    prefetch, args, specs, scratch = [], [x], [row], []
    if combine:
        yb, dest, cw = moe
        prefetch.append(dest)
        args += [yb, cw, g2]
        specs += [pl.BlockSpec(memory_space=pl.ANY), lane_row, vec]
        scratch = [pltpu.VMEM((2, TOP_K, tm * chunks // 2, LANES), jnp.uint32), pltpu.SemaphoreType.DMA((2,))]
    args.append(g.reshape(1, d))
    specs.append(vec)
    if modulate:
        args += [shift, scale]
        specs += [vec, vec]
    if router:
        args += [wr, br]
        specs += [pl.BlockSpec(wr.shape, lambda i, *_: (0, 0)), pl.BlockSpec(br.shape, lambda i, *_: (0, 0))]
    out_shape, out_specs = [], []
    if combine:
        out_shape.append(jax.ShapeDtypeStruct((t, d), F32))
        out_specs.append(row)
    if router:
        out_shape.append(jax.ShapeDtypeStruct((t * chunks // 2, LANES), jnp.uint32))
        out_specs.append(pl.BlockSpec((tm * chunks // 2, LANES), lambda i, *_: (i, 0)))
    else:
        out_shape.append(jax.ShapeDtypeStruct((t, d), h_dtype))
        out_specs.append(row)
    if router:
        out_shape += [jax.ShapeDtypeStruct((t, ROUTER_LANES), F32), jax.ShapeDtypeStruct((t, ROUTER_LANES), jnp.int32)]
        out_specs += [lane_row, lane_row]
    kern = functools.partial(_pre_kernel, combine=combine, modulate=modulate, router=router,
                             emit_h=True, h_dtype=h_dtype)
    grid_spec = pltpu.PrefetchScalarGridSpec(
        num_scalar_prefetch=len(prefetch), grid=(t // tm,), in_specs=specs, out_specs=out_specs,
        scratch_shapes=scratch)
    return pl.pallas_call(
        kern, grid_spec=grid_spec, out_shape=out_shape,
        compiler_params=_cparams("arbitrary" if combine else "parallel"), name="pre_norm",
    )(*prefetch, *args)


def _mm_kernel(*refs, lhs_of, n_lhs, n_extras, epilogue):
    n_pairs = len(lhs_of)
    lhs = refs[:n_lhs]
    weights = refs[n_lhs:n_lhs + n_pairs]
    extras = refs[n_lhs + n_pairs:n_lhs + n_pairs + n_extras]
    o_ref = refs[n_lhs + n_pairs + n_extras]
    caches = refs[n_lhs + n_pairs + n_extras + 1:]

    @pl.when(pl.program_id(1) == 0)
    def _():
        for p in range(n_pairs):
            caches[p][...] = weights[p][...].astype(BF16)

    def product(p):
        a = lhs[lhs_of[p]][...]
        a = _unpack_bf16_pairs(a) if a.dtype == jnp.uint32 else a
        return jnp.dot(a.astype(BF16), caches[p][...], preferred_element_type=F32)

    out = epilogue(product, *[r[...] for r in extras])
    o_ref[...] = _pack_bf16_pairs(out) if o_ref.dtype == jnp.uint32 else out.astype(o_ref.dtype)


def _mm(pairs, extras, epilogue, n, out_dtype, name, tn=MM_TN, tm=MM_TM):
    m = pairs[0][0].shape[0]
    tm, tn = min(tm, m), min(tn, n)
    packed_out = out_dtype == jnp.uint32
    assert not packed_out or tn == n
    lhs, lhs_of, w_args, w_specs, scratch = [], [], [], [], []
    for a, w, layer, off in pairs:
        k = a.shape[1] * (2 if a.dtype == jnp.uint32 else 1)
        if not any(a is seen for seen in lhs):
            lhs.append(a)
        lhs_of.append([a is seen for seen in lhs].index(True))
        w_args.append(w)
        w_specs.append(pl.BlockSpec((None, k, tn), lambda j, i, layer=layer, off=off: (layer, 0, j + off // tn)))
        scratch.append(pltpu.VMEM((k, tn), BF16))
    args = lhs + w_args
    specs = [pl.BlockSpec((tm, a.shape[1]), lambda j, i: (i, 0)) for a in lhs] + w_specs
    for arr, bshape, imap in extras:
        args.append(arr)
        specs.append(pl.BlockSpec(bshape(tm, tn), imap(tn)))
    return pl.pallas_call(
        functools.partial(_mm_kernel, lhs_of=tuple(lhs_of), n_lhs=len(lhs), n_extras=len(extras),
                          epilogue=epilogue),
        grid=(n // tn, m // tm), in_specs=specs,
        out_specs=pl.BlockSpec((tm, tn // 2 if packed_out else tn), lambda j, i: (i, j)),
        out_shape=jax.ShapeDtypeStruct((m, n // 2 if packed_out else n), out_dtype),
        scratch_shapes=scratch,
        compiler_params=_cparams("arbitrary", "arbitrary"), name=name,
    )(*args)


def _tile(tm, tn):
    return (tm, tn)


def _rowvec(tm, tn):
    return (1, tn)


def _at_cols(off):
    return lambda tn: (lambda j, i: (i, j + off // tn))


def _vec_cols(off):
    return lambda tn: (lambda j, i: (0, j + off // tn))


def _attn_bias_fill(rpb_ref, b_ref, head0):
    n_dr, n_dc = 2 * WIN_ROWS - 1, 2 * WIN_COLS - 1
    qc = lax.broadcasted_iota(jnp.int32, (GRID_W, 2 * GRID_W), 0)
    kc = lax.broadcasted_iota(jnp.int32, (GRID_W, 2 * GRID_W), 1) & (GRID_W - 1)
    cs = jnp.clip(qc - WIN_COLS // 2, 0, GRID_W - WIN_COLS)
    inside = (kc >= cs) & (kc < cs + WIN_COLS)
    dc = jnp.clip(kc - qc, -(WIN_COLS - 1), WIN_COLS - 1) + (WIN_COLS - 1)
    lane = lax.broadcasted_iota(jnp.int32, (SUBLANES, 2 * GRID_W), 1)
    for h in range(2):
        base = (head0 + h) * (n_dr * n_dc)
        for dr in range(n_dr):
            row = jnp.zeros((SUBLANES, 2 * GRID_W), F32)
            for d in range(n_dc):
                row = jnp.where(lane == d, rpb_ref[base + dr * n_dc + d], row)
            table = jnp.concatenate([row] * (GRID_W // SUBLANES), axis=0)
            t = jnp.where(inside, jnp.take_along_axis(table, dc, axis=1), NEG_INF)
            for var in range(WIN_ROWS):
                j = dr - var
                if 0 <= j < WIN_ROWS:
                    c0 = j * GRID_W
                    lo = c0 % (2 * GRID_W)
                    b_ref[var, h * GRID_W:(h + 1) * GRID_W, c0:c0 + GRID_W] = t[:, lo:lo + GRID_W]


def _attn_kernel(rpb_ref, q_ref, k_ref, v_ref, o_ref, b_ref, s_ref, *, n_rows):
    rb = pl.program_id(1)
    n_q = q_ref.shape[0]
    wkeys = WIN_ROWS * GRID_W
    dh = NA_HEAD_DIM
    lane = lax.broadcasted_iota(jnp.int32, (GRID_W, 2 * dh), 1)
    first = lane < dh

    @pl.when(rb == 0)
    def _():
        _attn_bias_fill(rpb_ref, b_ref, 2 * pl.program_id(0))

    def window(i):
        r = rb * n_q + i
        rs = jnp.clip(r - WIN_ROWS // 2, 0, n_rows - WIN_ROWS)
        return rs, rs - r + (WIN_ROWS - 1)

    for i in range(n_q):
        rs, var = window(i)
        kw = k_ref[pl.ds(rs, WIN_ROWS)].reshape(wkeys, 2 * dh)
        q = q_ref[i] * jnp.asarray(dh ** -0.5, BF16)
        zero = jnp.zeros_like(q)
        q2 = jnp.concatenate([jnp.where(first, q, zero), jnp.where(first, zero, q)], axis=0)
        s = lax.dot_general(q2, kw, (((1,), (1,)), ((), ())), preferred_element_type=F32)
        b = b_ref[var]
        s_ref[i] = s + b
    for i in range(n_q):
        rs, _ = window(i)
        vw = v_ref[pl.ds(rs, WIN_ROWS)].reshape(wkeys, 2 * dh)
        s = s_ref[i]
        m = jnp.max(s, axis=-1, keepdims=True)
        p = jnp.exp(s - m)
        l = jnp.sum(p, axis=-1, keepdims=True)
        o = jnp.dot(p.astype(BF16), vw, preferred_element_type=F32) / l
        o_ref[i] = jnp.where(first, o[:GRID_W], o[GRID_W:]).astype(o_ref.dtype)


def _attention(proj, rpb):
    t, width = proj.shape
    n_rows = t // GRID_W
    pairs = NA_HEADS // 2
    p3 = proj.reshape(n_rows, GRID_W, width)
    blk = 2 * NA_HEAD_DIM
    rows = min(ATTN_ROWS, n_rows)
    wkeys = WIN_ROWS * GRID_W
    out = pl.pallas_call(
        functools.partial(_attn_kernel, n_rows=n_rows),
        grid_spec=pltpu.PrefetchScalarGridSpec(
            num_scalar_prefetch=1, grid=(pairs, n_rows // rows),
            in_specs=[
                pl.BlockSpec((rows, GRID_W, blk), lambda hp, rb, _: (rb, 0, hp)),
                pl.BlockSpec((n_rows, GRID_W, blk), lambda hp, rb, _: (0, 0, pairs + hp)),
                pl.BlockSpec((n_rows, GRID_W, blk), lambda hp, rb, _: (0, 0, 2 * pairs + hp)),
            ],
            out_specs=pl.BlockSpec((rows, GRID_W, blk), lambda hp, rb, _: (rb, 0, hp)),
            scratch_shapes=[pltpu.VMEM((WIN_ROWS, 2 * GRID_W, wkeys), F32),
                            pltpu.VMEM((rows, 2 * GRID_W, wkeys), F32)]),
        out_shape=jax.ShapeDtypeStruct((n_rows, GRID_W, NA_HEADS * NA_HEAD_DIM), BF16),
        compiler_params=_cparams("arbitrary", "arbitrary"), name="na_attention",
    )(rpb.reshape(-1), p3, p3, p3)
    return out.reshape(t, NA_HEADS * NA_HEAD_DIM)


def _dft_tables(n1, n2, dc):
    s = n1 * n2
    a = np.arange(dc, dtype=np.float64)
    ang_c = 2 * np.pi * np.outer(a, a) / dc
    w_c = np.concatenate([np.cos(ang_c), -np.sin(ang_c)], axis=1)
    a1 = np.arange(n1, dtype=np.float64)
    ang1 = 2 * np.pi * np.outer(a1, a1) / n1
    c1, s1 = np.cos(ang1), np.sin(ang1)
    m1 = np.block([[c1, s1], [-s1, c1]])
    a2 = np.arange(n2, dtype=np.float64)
    ang_t = 2 * np.pi * np.outer(a2, a1) / s
    tw_c = np.repeat(np.cos(ang_t)[:, :, None], LANES, axis=2)
    tw_s = np.repeat(np.sin(ang_t)[:, :, None], LANES, axis=2)
    ang2 = 2 * np.pi * np.outer(a2, a2) / n2
    m2 = np.concatenate([np.cos(ang2), np.sin(ang2)], axis=1)
    return (jnp.asarray(w_c, BF16), jnp.asarray(m1, BF16), jnp.asarray(tw_c, F32), jnp.asarray(tw_s, F32),
            jnp.asarray(m2, BF16))


def _fft1_kernel(u_hbm, wc_ref, m1_ref, tc_ref, ts_ref, br_hbm, bi_hbm,
                 ubuf, zz_ref, obuf, isem, osem, *, n1, dc):
    s = pl.program_id(0)
    ns = pl.num_programs(0)
    jb = ubuf.shape[1]
    slot = s % 2

    def in_copy(step, sl, jj):
        return pltpu.make_async_copy(u_hbm.at[:, step * jb + jj], ubuf.at[sl, jj], isem.at[sl])

    def out_copies(step, sl, jj):
        j = step * jb + jj
        return (pltpu.make_async_copy(obuf.at[sl, 0, jj], br_hbm.at[:, j], osem.at[sl]),
                pltpu.make_async_copy(obuf.at[sl, 1, jj], bi_hbm.at[:, j], osem.at[sl]))

    @pl.when(s == 0)
    def _():
        for jj in range(jb):
            in_copy(0, 0, jj).start()

    @pl.when(s + 1 < ns)
    def _():
        for jj in range(jb):
            in_copy(s + 1, 1 - slot, jj).start()

    for jj in range(jb):
        in_copy(s, slot, jj).wait()

    @pl.when(s >= 2)
    def _():
        for jj in range(jb):
            for cp in out_copies(s - 2, slot, jj):
                cp.wait()

    c = 2 * ubuf.shape[3]
    u = _unpack_bf16_pairs(ubuf[slot].reshape(jb * n1, c // 2)).astype(BF16)
    for g in range(c // dc):
        z = jnp.dot(u[:, g * dc:(g + 1) * dc], wc_ref[...], preferred_element_type=F32)
        z = z.astype(BF16)
        for jj in range(jb):
            zz_ref[jj, :n1, g * dc:(g + 1) * dc] = z[jj * n1:(jj + 1) * n1, :dc]
            zz_ref[jj, n1:, g * dc:(g + 1) * dc] = z[jj * n1:(jj + 1) * n1, dc:]
    for jj in range(jb):
        a = jnp.dot(m1_ref[...], zz_ref[jj], preferred_element_type=F32)
        ar, ai = a[:n1], a[n1:]
        tc = jnp.concatenate([tc_ref[jj]] * (c // LANES), axis=1)
        ts = jnp.concatenate([ts_ref[jj]] * (c // LANES), axis=1)
        obuf[slot, 0, jj] = _pack_bf16_pairs(ar * tc + ai * ts)
        obuf[slot, 1, jj] = _pack_bf16_pairs(ai * tc - ar * ts)
    for jj in range(jb):
        for cp in out_copies(s, slot, jj):
            cp.start()

    @pl.when(s == ns - 1)
    def _():
        for jj in range(jb):
            for cp in out_copies(s, slot, jj):
                cp.wait()

        @pl.when(s >= 1)
        def _():
            for jj in range(jb):
                for cp in out_copies(s - 1, 1 - slot, jj):
                    cp.wait()


def _fft2_kernel(br_ref, bi_ref, m2_ref, y_hbm, obuf, osem, *, n2, scale):
    s = pl.program_id(0)
    ns = pl.num_programs(0)
    kb = obuf.shape[1]
    slot = s % 2

    def out_copy(step, sl, kk):
        return pltpu.make_async_copy(obuf.at[sl, kk], y_hbm.at[:, step * kb + kk], osem.at[sl])

    @pl.when(s >= 2)
    def _():
        for kk in range(kb):
            out_copy(s - 2, slot, kk).wait()

    for kk in range(kb):
        rows = slice(kk * n2, (kk + 1) * n2)
        bb = _unpack_bf16_pairs(jnp.concatenate([br_ref[rows, :], bi_ref[rows, :]], axis=0)).astype(BF16)
        obuf[slot, kk] = _pack_bf16_pairs(jnp.dot(m2_ref[...], bb, preferred_element_type=F32) * scale)
    for kk in range(kb):
        out_copy(s, slot, kk).start()

    @pl.when(s == ns - 1)
    def _():
        for kk in range(kb):
            out_copy(s, slot, kk).wait()

        @pl.when(s >= 1)
        def _():
            for kk in range(kb):
                out_copy(s - 1, 1 - slot, kk).wait()


def _fourier(u):
    t, hw = u.shape
    fw = 2 * hw
    n2 = GRID_W
    n1 = t // n2
    dc = F_GROUP_DIM
    w_c, m1, tw_c, tw_s, m2 = _dft_tables(n1, n2, dc)
    jb = min(FFT_JB, n2)
    kb = min(FFT_KB, n1)
    any_spec = pl.BlockSpec(memory_space=pl.ANY)
    tw_spec = pl.BlockSpec((jb, n1, LANES), lambda s: (s, 0, 0))
    b_re, b_im = pl.pallas_call(
        functools.partial(_fft1_kernel, n1=n1, dc=dc),
        grid=(n2 // jb,),
        in_specs=[
            any_spec,
            pl.BlockSpec((dc, 2 * dc), lambda s: (0, 0)),
            pl.BlockSpec((2 * n1, 2 * n1), lambda s: (0, 0)),
            tw_spec, tw_spec,
        ],
        out_specs=[any_spec, any_spec],
        out_shape=[jax.ShapeDtypeStruct((n1, n2, hw), jnp.uint32)] * 2,
        scratch_shapes=[
            pltpu.VMEM((2, jb, n1, hw), jnp.uint32),
            pltpu.VMEM((jb, 2 * n1, fw), BF16),
            pltpu.VMEM((2, 2, jb, n1, hw), jnp.uint32),
            pltpu.SemaphoreType.DMA((2,)),
            pltpu.SemaphoreType.DMA((2,)),
        ],
        compiler_params=_cparams("arbitrary"), name="fourier_stage1",
    )(u.reshape(n1, n2, hw), w_c, m1, tw_c, tw_s)
    scale = 1.0 / math.sqrt(t * dc)
    in_blk = pl.BlockSpec((kb * n2, hw), lambda s: (s, 0))
    y = pl.pallas_call(
        functools.partial(_fft2_kernel, n2=n2, scale=scale),
        grid=(n1 // kb,),
        in_specs=[in_blk, in_blk, pl.BlockSpec((n2, 2 * n2), lambda s: (0, 0))],
        out_specs=any_spec,
        out_shape=jax.ShapeDtypeStruct((n2, n1, hw), jnp.uint32),
        scratch_shapes=[pltpu.VMEM((2, kb, n2, hw), jnp.uint32), pltpu.SemaphoreType.DMA((2,))],
        compiler_params=_cparams("arbitrary"), name="fourier_stage2",
    )(b_re.reshape(t, hw), b_im.reshape(t, hw), m2)
    return y.reshape(t, hw)


def _dispatch_kernel(dest_ref, pad0_ref, padn_ref, na_ref, h_ref, xb_hbm, zbuf, sem, zsem, *, chunks):
    i = pl.program_id(0)
    tm = h_ref.shape[0] // chunks
    tb = zbuf.shape[0] // chunks
    nb = xb_hbm.shape[0] // (tb * chunks)

    def tail_copies(fn):
        def body(b, carry):
            fn(pltpu.make_async_copy(zbuf, xb_hbm.at[_token_rows(b, tb * chunks)], zsem))
            return carry
        lax.fori_loop(na_ref[0], nb, body, 0)

    def pad_copies(e, fn):
        pos = pad0_ref[e]
        n = padn_ref[e]
        bit = tb // 2
        while bit >= 1:
            @pl.when((n & bit) != 0)
            def _(pos=pos, bit=bit):
                rows = pl.ds(pl.multiple_of(pos * chunks, chunks), bit * chunks)
                fn(pltpu.make_async_copy(zbuf.at[pl.ds(0, bit * chunks)], xb_hbm.at[rows], zsem))
            pos = pos + (n & bit)
            bit //= 2

    @pl.when(i == 0)
    def _():
        zbuf[...] = jnp.zeros_like(zbuf)
        lax.fori_loop(0, N_EXPERTS, lambda e, c: (pad_copies(e, lambda cp: cp.start()), c)[1], 0)
        tail_copies(lambda cp: cp.start())

    def row_copy(r, k):
        dst = dest_ref[TOP_K * (i * tm + r) + k]
        return pltpu.make_async_copy(h_ref.at[_token_rows(r, chunks)], xb_hbm.at[_token_rows(dst, chunks)], sem)

    def for_rows(fn):
        def body(r, carry):
            for k in range(TOP_K):
                fn(r, k)
            return carry
        lax.fori_loop(0, tm, body, 0, unroll=ROW_DMA_UNROLL)

    for_rows(lambda r, k: row_copy(r, k).start(priority=k))
    for_rows(lambda r, k: row_copy(r, k).wait())

    @pl.when(i == 0)
    def _():
        lax.fori_loop(0, N_EXPERTS, lambda e, c: (pad_copies(e, lambda cp: cp.wait()), c)[1], 0)
        tail_copies(lambda cp: cp.wait())


def _pack_bf16_pairs(h):
    half = h.shape[1] // 2
    bits = lax.bitcast_convert_type(h.astype(BF16).astype(F32), jnp.uint32)
    return bits[:, half:] | (bits[:, :half] >> 16)


def _unpack_bf16_pairs(p):
    lo = lax.bitcast_convert_type(p << 16, F32)
    hi = lax.bitcast_convert_type(p & jnp.uint32(0xFFFF0000), F32)
    return jnp.concatenate([lo, hi], axis=1)


def _expert_kernel(be_ref, bi_ref, first_ref, par_ref, nxt1_ref, nxt2_ref, na_ref,
                   x_ref, wg_hbm, wu_hbm, wd_hbm, y_ref,
                   wg_st, wu_st, wd_st, wg_c, wu_c, wd_c, wsem, *, layer):
    b = pl.program_id(0)
    active = b < na_ref[0]
    d = wg_c.shape[0]
    xchunks = d // LANES // 2
    tb = x_ref.shape[0] // xchunks

    def weight_copies(e, slot):
        return [pltpu.make_async_copy(src.at[layer, e], dst.at[slot], wsem.at[slot, n])
                for n, (src, dst) in enumerate(((wg_hbm, wg_st), (wu_hbm, wu_st), (wd_hbm, wd_st)))]

    def start_weights(e, slot):
        for cp, priority in zip(weight_copies(e, slot), (1, 1, 0)):
            cp.start(priority=priority)

    @pl.when(b == 0)
    def _():
        start_weights(be_ref[0], 0)

        @pl.when(nxt1_ref[0] >= 0)
        def _():
            start_weights(nxt1_ref[0], 1)

    is_first = first_ref[b] == 1

    def block(refresh):
        slot = par_ref[b]
        x = _unpack_bf16_pairs(_load_token_major(x_ref, tb, xchunks)).astype(BF16)
        if refresh:
            wg_c[...] = wg_st[slot].astype(BF16)
        gate = jnp.dot(x, wg_c[...], preferred_element_type=F32)
        if refresh:
            wu_c[...] = wu_st[slot].astype(BF16)
        up = jnp.dot(x, wu_c[...], preferred_element_type=F32)
        act = (gate * _sigmoid(gate) * up).astype(BF16)
        if refresh:
            wd_c[...] = wd_st[slot].astype(BF16)
        _store_token_major(y_ref, _pack_bf16_pairs(jnp.dot(act, wd_c[...], preferred_element_type=F32)))

    @pl.when(active & is_first)
    def _():
        slot = par_ref[b]
        for cp in weight_copies(be_ref[b], slot):
            cp.wait()
        block(True)

        @pl.when(nxt2_ref[b] >= 0)
        def _():
            start_weights(nxt2_ref[b], slot)

    @pl.when(active & jnp.logical_not(is_first))
    def _():
        block(False)

    @pl.when(jnp.logical_not(active))
    def _():
        y_ref[...] = jnp.zeros_like(y_ref)


def _moe_plan(experts, t, tb):
    a = t * TOP_K
    e_flat = experts.reshape(a)
    lanes = jnp.arange(N_EXPERTS, dtype=jnp.int32)[None, :]
    onehot = (e_flat[:, None] == lanes).astype(jnp.int32)
    csum = jnp.cumsum(onehot, axis=0)
    counts = csum[-1]
    pcounts = (counts + tb - 1) // tb * tb
    pends = jnp.cumsum(pcounts)
    pstarts = pends - pcounts
    dest = jnp.sum(onehot * (pstarts[None, :] + csum - 1), axis=1).astype(jnp.int32)
    nb = (a + N_EXPERTS * tb) // tb
    n_act = (pends[-1] // tb).astype(jnp.int32)
    blk = jnp.arange(nb, dtype=jnp.int32)
    blk_idx = jnp.minimum(blk, n_act - 1)
    be = jnp.sum((pends[None, :] <= (blk_idx * tb)[:, None]).astype(jnp.int32), axis=1)
    be = jnp.minimum(be, N_EXPERTS - 1).astype(jnp.int32)
    first = ((blk * tb == pstarts[be]) & (blk < n_act)).astype(jnp.int32)
    parity = ((jnp.cumsum(first) - 1) & 1).astype(jnp.int32)

    def next_expert(e):
        after = pends[jnp.maximum(e, 0)] // tb
        return jnp.where((e >= 0) & (after < n_act), be[jnp.minimum(after, nb - 1)], -1).astype(jnp.int32)

    nxt1 = next_expert(be)
    nxt2 = next_expert(nxt1)
    plan = dict(dest=dest, be=be, blk_idx=blk_idx, first=first, parity=parity, nxt1=nxt1, nxt2=nxt2,
                n_act=n_act.reshape(1),
                pad0=(pstarts + counts).astype(jnp.int32), padn=(pcounts - counts).astype(jnp.int32))
    return plan


def _experts(h, experts, layer, w_gate, w_up, w_down):
    d, de = w_gate.shape[2:]
    chunks = d // LANES
    xchunks = chunks // 2
    t = h.shape[0] // xchunks
    tb = EXPERT_TB
    tm = min(PRE_TM, t)
    plan = _moe_plan(experts, t, tb)
    nb = plan["be"].shape[0]
    p = nb * tb
    xb = pl.pallas_call(
        functools.partial(_dispatch_kernel, chunks=xchunks),
        grid_spec=pltpu.PrefetchScalarGridSpec(
            num_scalar_prefetch=4, grid=(t // tm,),
            in_specs=[pl.BlockSpec((tm * xchunks, LANES), lambda i, *_: (i, 0))],
            out_specs=pl.BlockSpec(memory_space=pl.ANY),
            scratch_shapes=[pltpu.VMEM((tb * xchunks, LANES), h.dtype),
                            pltpu.SemaphoreType.DMA(()), pltpu.SemaphoreType.DMA(())]),
        out_shape=jax.ShapeDtypeStruct((p * xchunks, LANES), h.dtype),
        compiler_params=_cparams("arbitrary"), name="moe_dispatch",
    )(plan["dest"], plan["pad0"], plan["padn"], plan["n_act"], h)
    hbm = pl.BlockSpec(memory_space=pl.ANY)
    yb = pl.pallas_call(
        functools.partial(_expert_kernel, layer=layer),
        grid_spec=pltpu.PrefetchScalarGridSpec(
            num_scalar_prefetch=7, grid=(nb,),
            in_specs=[pl.BlockSpec((tb * xchunks, LANES), lambda b, be, bi, *_: (bi[b], 0)), hbm, hbm, hbm],
            out_specs=pl.BlockSpec((tb * xchunks, LANES), lambda b, *_: (b, 0)),
            scratch_shapes=[pltpu.VMEM((2, d, de), F32), pltpu.VMEM((2, d, de), F32), pltpu.VMEM((2, de, d), F32),
                            pltpu.VMEM((d, de), BF16), pltpu.VMEM((d, de), BF16), pltpu.VMEM((de, d), BF16),
                            pltpu.SemaphoreType.DMA((2, 3))]),
        out_shape=jax.ShapeDtypeStruct((p * xchunks, LANES), jnp.uint32),
        compiler_params=_cparams("arbitrary"), name="moe_experts",
    )(plan["be"], plan["blk_idx"], plan["first"], plan["parity"], plan["nxt1"], plan["nxt2"], plan["n_act"],
      xb, w_gate, w_up, w_down)
    return yb, plan["dest"]


def kernel(x, c, norm1_g, norm2_g, ada_w, ada_b, mix_in_w, na_rpb, na_out_w, fourier_out_w, branch_gate_w,
           branch_gate_b, mix_out_w, router_group_w, router_group_b, router_expert_w, router_expert_b,
           expert_w_gate, expert_w_up, expert_w_down, final_g):
    bsz, s, d = x.shape
    assert bsz == 1 and s % GRID_W == 0
    depth = ada_w.shape[0]
    na_w = NA_HEADS * NA_HEAD_DIM
    f_w = F_GROUPS * F_GROUP_DIM
    xs = x.reshape(s, d)
    mod = _ada_all(c, ada_w, ada_b)
    pad = ROUTER_LANES - N_GROUPS - N_EXPERTS
    wr_all = jnp.concatenate([router_group_w, router_expert_w, jnp.zeros((depth, d, pad), F32)], axis=2)
    wr_hi = wr_all.astype(BF16)
    wr_lo = (wr_all - wr_hi.astype(F32)).astype(BF16)
    wr_all = jnp.concatenate([wr_hi, wr_lo], axis=2)
    br_all = jnp.concatenate([router_group_b, router_expert_b, jnp.zeros((depth, pad), F32)], axis=1)
    moe = g2 = None
    for l in range(depth):
        sh1, sc1, g1, sh2, sc2, g2_l = [mod[l, :, i * d:(i + 1) * d] for i in range(N_MOD)]
        if moe is None:
            (h,) = _pre(xs, norm1_g[l], shift=sh1, scale=sc1)
        else:
            xs, h = _pre(xs, norm1_g[l], moe=moe, g2=g2, shift=sh1, scale=sc1)
        qkv = _mm([(h, mix_in_w, l, 0)], [], lambda prod: prod(0), 3 * na_w, BF16, "mix_in_qkv", tn=MM_TN_WIDE)
        u = _mm([(h, mix_in_w, l, 3 * na_w)], [], lambda prod: prod(0), f_w, jnp.uint32, "mix_in_fourier",
                tn=f_w)
        y_att = _attention(qkv, na_rpb[l])
        y_fft = _fourier(u)
        bias_bg = branch_gate_b[l].reshape(1, -1)
        mixed = _mm([(h, branch_gate_w, l, 0), (h, branch_gate_w, l, d),
                     (y_att, na_out_w, l, 0), (y_fft, fourier_out_w, l, 0)],
                    [(bias_bg, _rowvec, _vec_cols(0)), (bias_bg, _rowvec, _vec_cols(d))],
                    lambda prod, ba, bf: (_sigmoid(prod(0) + ba) * prod(2) + _sigmoid(prod(1) + bf) * prod(3)),
                    d, BF16, "gated_branch_mix", tn=MM_TN, tm=MM_TM_FUSED)
        xs = _mm([(mixed, mix_out_w, l, 0)],
                 [(xs, _tile, _at_cols(0)), (g1, _rowvec, _vec_cols(0))],
                 lambda prod, xr, g: xr + g * prod(0), d, F32, "mix_out", tn=MM_TN_WIDE)
        h2, wts, ids = _pre(xs, norm2_g[l], shift=sh2, scale=sc2, wr=wr_all[l], br=br_all[l].reshape(1, -1),
                            h_dtype=F32)
        yb, dest = _experts(h2, ids[:, :TOP_K], l, expert_w_gate, expert_w_up, expert_w_down)
        moe, g2 = (yb, dest, wts), g2_l
    _, out = _pre(xs, final_g, moe=moe, g2=g2, h_dtype=F32)
    return out.reshape(bsz, s, d)
```

```python
import functools
import math

import numpy as np
import jax
import jax.numpy as jnp
from jax import lax
from jax.experimental import pallas as pl
from jax.experimental.pallas import tpu as pltpu

F32 = jnp.float32
BF16 = jnp.bfloat16

GRID_W = 64
NA_HEADS = 16
NA_HEAD_DIM = 64
WIN_ROWS = 8
WIN_COLS = 16
F_GROUPS = 4
F_GROUP_DIM = 256
N_GROUPS = 4
EXPERTS_PER_GROUP = 8
N_EXPERTS = N_GROUPS * EXPERTS_PER_GROUP
TOP_K = 2
N_MOD = 6
EPS = 1e-6
NEG_INF = -1e30

LANES = 128
SUBLANES = 8

ADA_TN = 1536
ADA_ROWS = 256
PRE_TM = 512
COMBINE_TM = 256
MM_TM = 1024
MM_TM_FUSED = 512
MM_TN = 512
MM_TN_WIDE = 1024
ATTN_ROWS = 32
EXPERT_TB = 256
ROW_DMA_UNROLL = 8
FFT_JB = 8
FFT_KB = 16
ROUTER_LANES = LANES

VMEM_LIMIT = 56 * 1024 * 1024


def _cparams(*sem):
    return pltpu.CompilerParams(dimension_semantics=sem, vmem_limit_bytes=VMEM_LIMIT)


def _sigmoid(x):
    return 1.0 / (1.0 + jnp.exp(-x))


def _ada_kernel(c_ref, w_ref, b_ref, o_ref, cs_ref):
    c = c_ref[...]
    cs_ref[...] = c * _sigmoid(c)
    d, tn = w_ref.shape

    def body(i, acc):
        r0 = pl.multiple_of(i * ADA_ROWS, ADA_ROWS)
        prod = w_ref[pl.ds(r0, ADA_ROWS), :] * cs_ref[pl.ds(r0, ADA_ROWS), :]
        return acc + jnp.sum(prod.reshape(ADA_ROWS // SUBLANES, SUBLANES, tn), axis=0)

    acc = lax.fori_loop(0, d // ADA_ROWS, body, jnp.zeros((SUBLANES, tn), F32))
    o_ref[...] = jnp.sum(acc, axis=0, keepdims=True) + b_ref[...]


def _ada_first(c, ada_w, ada_b, n_layers):
    _, d, n = ada_w.shape
    nl = n_layers
    tn = min(ADA_TN, n)
    return pl.pallas_call(
        _ada_kernel,
        grid=(nl, n // tn),
        in_specs=[
            pl.BlockSpec((d, 1), lambda l, j: (0, 0)),
            pl.BlockSpec((None, d, tn), lambda l, j: (l, 0, j)),
            pl.BlockSpec((None, 1, tn), lambda l, j: (l, 0, j)),
        ],
        out_specs=pl.BlockSpec((None, 1, tn), lambda l, j: (l, 0, j)),
        out_shape=jax.ShapeDtypeStruct((nl, 1, n), F32),
        scratch_shapes=[pltpu.VMEM((d, 1), F32)],
        compiler_params=_cparams("parallel", "parallel"),
        name="ada_proj",
    )(c.reshape(d, 1), ada_w, ada_b.reshape(ada_b.shape[0], 1, n))


def _rms(x, g):
    return x * lax.rsqrt(jnp.mean(x * x, axis=-1, keepdims=True) + EPS) * g


def _router(h, wr_ref, br_ref):
    w_split = wr_ref[...]
    h_hi = h.astype(BF16)
    h_lo = (h - h_hi.astype(F32)).astype(BF16)
    dot = functools.partial(jnp.dot, preferred_element_type=F32)
    both = dot(h_hi, w_split)
    logits = both[:, :ROUTER_LANES] + (both[:, ROUTER_LANES:] + dot(h_lo, w_split[:, :ROUTER_LANES])) + br_ref[...]
    lane = lax.broadcasted_iota(jnp.int32, logits.shape, 1)
    big = jnp.int32(ROUTER_LANES)
    is_g = lane < N_GROUPS
    gl = jnp.where(is_g, logits, NEG_INF)
    gmax = jnp.max(gl, axis=-1, keepdims=True)
    gsum = jnp.sum(jnp.where(is_g, jnp.exp(gl - gmax), 0.0), axis=-1, keepdims=True)
    g_top_p = 1.0 / gsum
    g_top = jnp.min(jnp.where(is_g & (gl == gmax), lane, big), axis=-1, keepdims=True)
    e_lane = lane - N_GROUPS
    sel = (e_lane >= 0) & (e_lane < N_EXPERTS) & ((e_lane >> 3) == g_top)
    l1 = jnp.where(sel, logits, NEG_INF)
    m1 = jnp.max(l1, axis=-1, keepdims=True)
    i1 = jnp.min(jnp.where(sel & (l1 == m1), lane, big), axis=-1, keepdims=True)
    sel2 = sel & (lane != i1)
    l2 = jnp.where(sel2, logits, NEG_INF)
    m2 = jnp.max(l2, axis=-1, keepdims=True)
    i2 = jnp.min(jnp.where(sel2 & (l2 == m2), lane, big), axis=-1, keepdims=True)
    e21 = jnp.exp(m2 - m1)
    p1 = 1.0 / (1.0 + e21)
    p2 = e21 / (1.0 + e21)
    wts = jnp.where(lane == 0, g_top_p * p1, jnp.where(lane == 1, g_top_p * p2, 0.0))
    ids = jnp.where(lane == 0, i1 - N_GROUPS, jnp.where(lane == 1, i2 - N_GROUPS, 0))
    return wts, ids


def _load_token_major(ref, n_tok, chunks):
    return jnp.concatenate([ref[pl.ds(c, n_tok, stride=chunks), :] for c in range(chunks)], axis=1)


def _store_token_major(ref, val):
    n_tok, d = val.shape
    chunks = d // LANES
    for c in range(chunks):
        ref[pl.ds(c, n_tok, stride=chunks), :] = val[:, c * LANES:(c + 1) * LANES]


def _token_rows(tok, chunks):
    return pl.ds(pl.multiple_of(tok * chunks, chunks), chunks)


def _pre_kernel(*refs, combine, modulate, router, emit_h, h_dtype):
    it = iter(refs)
    if combine:
        dest_ref = next(it)
    x_ref = next(it)
    if combine:
        y_hbm, cw_ref, g2_ref = next(it), next(it), next(it)
    g_ref = next(it)
    if modulate:
        sh_ref, sc_ref = next(it), next(it)
    if router:
        wr_ref, br_ref = next(it), next(it)
    outs = [next(it) for _ in range(int(combine) + int(emit_h) + 2 * int(router))]
    x = x_ref[...]
    if combine:
        gbuf, gsem = next(it), next(it)
        i = pl.program_id(0)
        slot = i % 2
        tm, d = x_ref.shape
        chunks = d // LANES // 2

        def row_copy(step, sl, r, k):
            src = dest_ref[TOP_K * (step * tm + r) + k]
            return pltpu.make_async_copy(y_hbm.at[_token_rows(src, chunks)],
                                         gbuf.at[sl, k, _token_rows(r, chunks)], gsem.at[sl])

        def for_rows(fn):
            def body(r, carry):
                for k in range(TOP_K):
                    fn(r, k)
                return carry
            lax.fori_loop(0, tm, body, 0, unroll=ROW_DMA_UNROLL)

        @pl.when(i == 0)
        def _():
            for_rows(lambda r, k: row_copy(0, 0, r, k).start(priority=k))

        @pl.when(i + 1 < pl.num_programs(0))
        def _():
            for_rows(lambda r, k: row_copy(i + 1, 1 - slot, r, k).start(priority=k))

        for_rows(lambda r, k: row_copy(i, slot, r, k).wait())
        cw = cw_ref[...]
        y0 = _unpack_bf16_pairs(_load_token_major(gbuf.at[slot, 0], tm, chunks))
        y1 = _unpack_bf16_pairs(_load_token_major(gbuf.at[slot, 1], tm, chunks))
        x = x + g2_ref[...] * (cw[:, 0:1] * y0 + cw[:, 1:2] * y1)
        outs.pop(0)[...] = x
    h = _rms(x, g_ref[...])
    if modulate:
        h = h * (1.0 + sc_ref[...]) + sh_ref[...]
    if emit_h and router:
        _store_token_major(outs.pop(0), _pack_bf16_pairs(h))
    elif emit_h:
        outs.pop(0)[...] = h.astype(h_dtype)
    if router:
        wts, ids = _router(h, wr_ref, br_ref)
        outs.pop(0)[...] = wts
        outs.pop(0)[...] = ids


def _pre(x, g, *, moe=None, g2=None, shift=None, scale=None, wr=None, br=None, h_dtype=BF16):
    t, d = x.shape
    combine, modulate, router = moe is not None, shift is not None, wr is not None
    tm = min(COMBINE_TM if combine else PRE_TM, t)
    chunks = d // LANES
    row = pl.BlockSpec((tm, d), lambda i, *_: (i, 0))
    vec = pl.BlockSpec((1, d), lambda i, *_: (0, 0))
    lane_row = pl.BlockSpec((tm, ROUTER_LANES), lambda i, *_: (i, 0))
    prefetch, args, specs, scratch = [], [x], [row], []
    if combine:
        yb, dest, cw = moe
        prefetch.append(dest)
        args += [yb, cw, g2]
        specs += [pl.BlockSpec(memory_space=pl.ANY), lane_row, vec]
        scratch = [pltpu.VMEM((2, TOP_K, tm * chunks // 2, LANES), jnp.uint32), pltpu.SemaphoreType.DMA((2,))]
    args.append(g.reshape(1, d))
    specs.append(vec)
    if modulate:
        args += [shift, scale]
        specs += [vec, vec]
    if router:
        args += [wr, br]
        specs += [pl.BlockSpec(wr.shape, lambda i, *_: (0, 0)), pl.BlockSpec(br.shape, lambda i, *_: (0, 0))]
    out_shape, out_specs = [], []
    if combine:
        out_shape.append(jax.ShapeDtypeStruct((t, d), F32))
        out_specs.append(row)
    if router:
        out_shape.append(jax.ShapeDtypeStruct((t * chunks // 2, LANES), jnp.uint32))
        out_specs.append(pl.BlockSpec((tm * chunks // 2, LANES), lambda i, *_: (i, 0)))
    else:
        out_shape.append(jax.ShapeDtypeStruct((t, d), h_dtype))
        out_specs.append(row)
    if router:
        out_shape += [jax.ShapeDtypeStruct((t, ROUTER_LANES), F32), jax.ShapeDtypeStruct((t, ROUTER_LANES), jnp.int32)]
        out_specs += [lane_row, lane_row]
    kern = functools.partial(_pre_kernel, combine=combine, modulate=modulate, router=router,
                             emit_h=True, h_dtype=h_dtype)
    grid_spec = pltpu.PrefetchScalarGridSpec(
        num_scalar_prefetch=len(prefetch), grid=(t // tm,), in_specs=specs, out_specs=out_specs,
        scratch_shapes=scratch)
    return pl.pallas_call(
        kern, grid_spec=grid_spec, out_shape=out_shape,
        compiler_params=_cparams("arbitrary" if combine else "parallel"), name="pre_norm",
    )(*prefetch, *args)


def _mm_kernel(*refs, lhs_of, n_lhs, n_extras, epilogue, side):
    n_pairs = len(lhs_of)
    lhs = refs[:n_lhs]
    weights = refs[n_lhs:n_lhs + n_pairs]
    extras = refs[n_lhs + n_pairs:n_lhs + n_pairs + n_extras]
    pos = n_lhs + n_pairs + n_extras
    if side:
        c_ref, aw_ref, ab_ref = refs[pos:pos + 3]
        pos += 3
    o_ref = refs[pos]
    pos += 1
    if side:
        mod_ref = refs[pos]
        pos += 1
        c = c_ref[...]
        cs = c * _sigmoid(c)
        d, sn = aw_ref.shape
        acc = jnp.zeros((SUBLANES, sn), F32)
        for r0 in range(0, d, ADA_ROWS):
            prod = aw_ref[r0:r0 + ADA_ROWS, :] * cs[r0:r0 + ADA_ROWS]
            acc = acc + jnp.sum(prod.reshape(ADA_ROWS // SUBLANES, SUBLANES, sn), axis=0)
        mod_ref[...] = jnp.sum(acc, axis=0, keepdims=True) + ab_ref[...]
    caches = refs[pos:]

    @pl.when(pl.program_id(1) == 0)
    def _():
        for p in range(n_pairs):
            caches[p][...] = weights[p][...].astype(BF16)

    def product(p):
        a = lhs[lhs_of[p]][...]
        a = _unpack_bf16_pairs(a) if a.dtype == jnp.uint32 else a
        return jnp.dot(a.astype(BF16), caches[p][...], preferred_element_type=F32)

    out = epilogue(product, *[r[...] for r in extras])
    o_ref[...] = _pack_bf16_pairs(out) if o_ref.dtype == jnp.uint32 else out.astype(o_ref.dtype)


def _mm(pairs, extras, epilogue, n, out_dtype, name, tn=MM_TN, tm=MM_TM, side=None):
    m = pairs[0][0].shape[0]
    tm, tn = min(tm, m), min(tn, n)
    packed_out = out_dtype == jnp.uint32
    assert not packed_out or tn == n
    lhs, lhs_of, w_args, w_specs, scratch = [], [], [], [], []
    for a, w, layer, off in pairs:
        k = a.shape[1] * (2 if a.dtype == jnp.uint32 else 1)
        if not any(a is seen for seen in lhs):
            lhs.append(a)
        lhs_of.append([a is seen for seen in lhs].index(True))
        w_args.append(w)
        w_specs.append(pl.BlockSpec((None, k, tn), lambda j, i, layer=layer, off=off: (layer, 0, j + off // tn)))
        scratch.append(pltpu.VMEM((k, tn), BF16))
    args = lhs + w_args
    specs = [pl.BlockSpec((tm, a.shape[1]), lambda j, i: (i, 0)) for a in lhs] + w_specs
    for arr, bshape, imap in extras:
        args.append(arr)
        specs.append(pl.BlockSpec(bshape(tm, tn), imap(tn)))
    nj, ni = n // tn, m // tm
    out_specs = [pl.BlockSpec((tm, tn // 2 if packed_out else tn), lambda j, i: (i, j))]
    out_shape = [jax.ShapeDtypeStruct((m, n // 2 if packed_out else n), out_dtype)]
    if side is not None:
        c, ada_w, ada_b, layer = side
        d, width = ada_w.shape[1:]
        sn = width // (nj * ni)
        assert sn * nj * ni == width and sn % LANES == 0
        args += [c.reshape(d, 1), ada_w, ada_b.reshape(ada_b.shape[0], 1, width)]
        specs += [pl.BlockSpec((d, 1), lambda j, i: (0, 0)),
                  pl.BlockSpec((None, d, sn), lambda j, i: (layer, 0, j * ni + i)),
                  pl.BlockSpec((None, 1, sn), lambda j, i: (layer, 0, j * ni + i))]
        out_specs.append(pl.BlockSpec((1, sn), lambda j, i: (0, j * ni + i)))
        out_shape.append(jax.ShapeDtypeStruct((1, width), F32))
    outs = pl.pallas_call(
        functools.partial(_mm_kernel, lhs_of=tuple(lhs_of), n_lhs=len(lhs), n_extras=len(extras),
                          epilogue=epilogue, side=side is not None),
        grid=(nj, ni), in_specs=specs, out_specs=out_specs, out_shape=out_shape,
        scratch_shapes=scratch,
        compiler_params=_cparams("arbitrary", "arbitrary"), name=name,
    )(*args)
    return outs[0] if side is None else outs


def _tile(tm, tn):
    return (tm, tn)


def _rowvec(tm, tn):
    return (1, tn)


def _at_cols(off):
    return lambda tn: (lambda j, i: (i, j + off // tn))


def _vec_cols(off):
    return lambda tn: (lambda j, i: (0, j + off // tn))


def _attn_bias_fill(rpb_ref, b_ref, head0):
    n_dr, n_dc = 2 * WIN_ROWS - 1, 2 * WIN_COLS - 1
    qc = lax.broadcasted_iota(jnp.int32, (GRID_W, 2 * GRID_W), 0)
    kc = lax.broadcasted_iota(jnp.int32, (GRID_W, 2 * GRID_W), 1) & (GRID_W - 1)
    cs = jnp.clip(qc - WIN_COLS // 2, 0, GRID_W - WIN_COLS)
    inside = (kc >= cs) & (kc < cs + WIN_COLS)
    dc = jnp.clip(kc - qc, -(WIN_COLS - 1), WIN_COLS - 1) + (WIN_COLS - 1)
    lane = lax.broadcasted_iota(jnp.int32, (SUBLANES, 2 * GRID_W), 1)
    for h in range(2):
        base = (head0 + h) * (n_dr * n_dc)
        for dr in range(n_dr):
            row = jnp.zeros((SUBLANES, 2 * GRID_W), F32)
            for d in range(n_dc):
                row = jnp.where(lane == d, rpb_ref[base + dr * n_dc + d], row)
            table = jnp.concatenate([row] * (GRID_W // SUBLANES), axis=0)
            t = jnp.where(inside, jnp.take_along_axis(table, dc, axis=1), NEG_INF)
            for var in range(WIN_ROWS):
                j = dr - var
                if 0 <= j < WIN_ROWS:
                    c0 = j * GRID_W
                    lo = c0 % (2 * GRID_W)
                    b_ref[var, h * GRID_W:(h + 1) * GRID_W, c0:c0 + GRID_W] = t[:, lo:lo + GRID_W]


def _attn_kernel(rpb_ref, q_ref, k_ref, v_ref, o_ref, b_ref, s_ref, *, n_rows):
    rb = pl.program_id(1)
    n_q = q_ref.shape[0]
    wkeys = WIN_ROWS * GRID_W
    dh = NA_HEAD_DIM
    lane = lax.broadcasted_iota(jnp.int32, (GRID_W, 2 * dh), 1)
    first = lane < dh

    @pl.when(rb == 0)
    def _():
        _attn_bias_fill(rpb_ref, b_ref, 2 * pl.program_id(0))

    def window(i):
        r = rb * n_q + i
        rs = jnp.clip(r - WIN_ROWS // 2, 0, n_rows - WIN_ROWS)
        return rs, rs - r + (WIN_ROWS - 1)

    for i in range(n_q):
        rs, var = window(i)
        kw = k_ref[pl.ds(rs, WIN_ROWS)].reshape(wkeys, 2 * dh)
        q = q_ref[i] * jnp.asarray(dh ** -0.5, BF16)
        zero = jnp.zeros_like(q)
        q2 = jnp.concatenate([jnp.where(first, q, zero), jnp.where(first, zero, q)], axis=0)
        s = lax.dot_general(q2, kw, (((1,), (1,)), ((), ())), preferred_element_type=F32)
        b = b_ref[var]
        s_ref[i] = s + b
    for i in range(n_q):
        rs, _ = window(i)
        vw = v_ref[pl.ds(rs, WIN_ROWS)].reshape(wkeys, 2 * dh)
        s = s_ref[i]
        m = jnp.max(s, axis=-1, keepdims=True)
        p = jnp.exp(s - m)
        l = jnp.sum(p, axis=-1, keepdims=True)
        o = jnp.dot(p.astype(BF16), vw, preferred_element_type=F32) / l
        o_ref[i] = jnp.where(first, o[:GRID_W], o[GRID_W:]).astype(o_ref.dtype)


def _attention(proj, rpb):
    t, width = proj.shape
    n_rows = t // GRID_W
    pairs = NA_HEADS // 2
    p3 = proj.reshape(n_rows, GRID_W, width)
    blk = 2 * NA_HEAD_DIM
    rows = min(ATTN_ROWS, n_rows)
    wkeys = WIN_ROWS * GRID_W
    out = pl.pallas_call(
        functools.partial(_attn_kernel, n_rows=n_rows),
        grid_spec=pltpu.PrefetchScalarGridSpec(
            num_scalar_prefetch=1, grid=(pairs, n_rows // rows),
            in_specs=[
                pl.BlockSpec((rows, GRID_W, blk), lambda hp, rb, _: (rb, 0, hp)),
                pl.BlockSpec((n_rows, GRID_W, blk), lambda hp, rb, _: (0, 0, pairs + hp)),
                pl.BlockSpec((n_rows, GRID_W, blk), lambda hp, rb, _: (0, 0, 2 * pairs + hp)),
            ],
            out_specs=pl.BlockSpec((rows, GRID_W, blk), lambda hp, rb, _: (rb, 0, hp)),
            scratch_shapes=[pltpu.VMEM((WIN_ROWS, 2 * GRID_W, wkeys), F32),
                            pltpu.VMEM((rows, 2 * GRID_W, wkeys), F32)]),
        out_shape=jax.ShapeDtypeStruct((n_rows, GRID_W, NA_HEADS * NA_HEAD_DIM), BF16),
        compiler_params=_cparams("arbitrary", "arbitrary"), name="na_attention",
    )(rpb.reshape(-1), p3, p3, p3)
    return out.reshape(t, NA_HEADS * NA_HEAD_DIM)


def _dft_tables(n1, n2, dc):
    s = n1 * n2
    a = np.arange(dc, dtype=np.float64)
    ang_c = 2 * np.pi * np.outer(a, a) / dc
    w_c = np.concatenate([np.cos(ang_c), -np.sin(ang_c)], axis=1)
    a1 = np.arange(n1, dtype=np.float64)
    ang1 = 2 * np.pi * np.outer(a1, a1) / n1
    c1, s1 = np.cos(ang1), np.sin(ang1)
    m1 = np.block([[c1, s1], [-s1, c1]])
    a2 = np.arange(n2, dtype=np.float64)
    ang_t = 2 * np.pi * np.outer(a2, a1) / s
    tw_c = np.repeat(np.cos(ang_t)[:, :, None], LANES, axis=2)
    tw_s = np.repeat(np.sin(ang_t)[:, :, None], LANES, axis=2)
    ang2 = 2 * np.pi * np.outer(a2, a2) / n2
    m2 = np.concatenate([np.cos(ang2), np.sin(ang2)], axis=1)
    return (jnp.asarray(w_c, BF16), jnp.asarray(m1, BF16), jnp.asarray(tw_c, F32), jnp.asarray(tw_s, F32),
            jnp.asarray(m2, BF16))


def _fft1_kernel(u_hbm, wc_ref, m1_ref, tc_ref, ts_ref, br_hbm, bi_hbm,
                 ubuf, zz_ref, obuf, isem, osem, *, n1, dc):
    s = pl.program_id(0)
    ns = pl.num_programs(0)
    jb = ubuf.shape[1]
    slot = s % 2

    def in_copy(step, sl, jj):
        return pltpu.make_async_copy(u_hbm.at[:, step * jb + jj], ubuf.at[sl, jj], isem.at[sl])

    def out_copies(step, sl, jj):
        j = step * jb + jj
        return (pltpu.make_async_copy(obuf.at[sl, 0, jj], br_hbm.at[:, j], osem.at[sl]),
                pltpu.make_async_copy(obuf.at[sl, 1, jj], bi_hbm.at[:, j], osem.at[sl]))

    @pl.when(s == 0)
    def _():
        for jj in range(jb):
            in_copy(0, 0, jj).start()

    @pl.when(s + 1 < ns)
    def _():
        for jj in range(jb):
            in_copy(s + 1, 1 - slot, jj).start()

    for jj in range(jb):
        in_copy(s, slot, jj).wait()

    @pl.when(s >= 2)
    def _():
        for jj in range(jb):
            for cp in out_copies(s - 2, slot, jj):
                cp.wait()

    c = 2 * ubuf.shape[3]
    u = _unpack_bf16_pairs(ubuf[slot].reshape(jb * n1, c // 2)).astype(BF16)
    for g in range(c // dc):
        z = jnp.dot(u[:, g * dc:(g + 1) * dc], wc_ref[...], preferred_element_type=F32)
        z = z.astype(BF16)
        for jj in range(jb):
            zz_ref[jj, :n1, g * dc:(g + 1) * dc] = z[jj * n1:(jj + 1) * n1, :dc]
            zz_ref[jj, n1:, g * dc:(g + 1) * dc] = z[jj * n1:(jj + 1) * n1, dc:]
    for jj in range(jb):
        a = jnp.dot(m1_ref[...], zz_ref[jj], preferred_element_type=F32)
        ar, ai = a[:n1], a[n1:]
        tc = jnp.concatenate([tc_ref[jj]] * (c // LANES), axis=1)
        ts = jnp.concatenate([ts_ref[jj]] * (c // LANES), axis=1)
        obuf[slot, 0, jj] = _pack_bf16_pairs(ar * tc + ai * ts)
        obuf[slot, 1, jj] = _pack_bf16_pairs(ai * tc - ar * ts)
    for jj in range(jb):
        for cp in out_copies(s, slot, jj):
            cp.start()

    @pl.when(s == ns - 1)
    def _():
        for jj in range(jb):
            for cp in out_copies(s, slot, jj):
                cp.wait()

        @pl.when(s >= 1)
        def _():
            for jj in range(jb):
                for cp in out_copies(s - 1, 1 - slot, jj):
                    cp.wait()


def _fft2_kernel(br_ref, bi_ref, m2_ref, y_hbm, obuf, osem, *, n2, scale):
    s = pl.program_id(0)
    ns = pl.num_programs(0)
    kb = obuf.shape[1]
    slot = s % 2

    def out_copy(step, sl, kk):
        return pltpu.make_async_copy(obuf.at[sl, kk], y_hbm.at[:, step * kb + kk], osem.at[sl])

    @pl.when(s >= 2)
    def _():
        for kk in range(kb):
            out_copy(s - 2, slot, kk).wait()

    for kk in range(kb):
        rows = slice(kk * n2, (kk + 1) * n2)
        bb = _unpack_bf16_pairs(jnp.concatenate([br_ref[rows, :], bi_ref[rows, :]], axis=0)).astype(BF16)
        obuf[slot, kk] = _pack_bf16_pairs(jnp.dot(m2_ref[...], bb, preferred_element_type=F32) * scale)
    for kk in range(kb):
        out_copy(s, slot, kk).start()

    @pl.when(s == ns - 1)
    def _():
        for kk in range(kb):
            out_copy(s, slot, kk).wait()

        @pl.when(s >= 1)
        def _():
            for kk in range(kb):
                out_copy(s - 1, 1 - slot, kk).wait()


def _fourier(u):
    t, hw = u.shape
    fw = 2 * hw
    n2 = GRID_W
    n1 = t // n2
    dc = F_GROUP_DIM
    w_c, m1, tw_c, tw_s, m2 = _dft_tables(n1, n2, dc)
    jb = min(FFT_JB, n2)
    kb = min(FFT_KB, n1)
    any_spec = pl.BlockSpec(memory_space=pl.ANY)
    tw_spec = pl.BlockSpec((jb, n1, LANES), lambda s: (s, 0, 0))
    b_re, b_im = pl.pallas_call(
        functools.partial(_fft1_kernel, n1=n1, dc=dc),
        grid=(n2 // jb,),
        in_specs=[
            any_spec,
            pl.BlockSpec((dc, 2 * dc), lambda s: (0, 0)),
            pl.BlockSpec((2 * n1, 2 * n1), lambda s: (0, 0)),
            tw_spec, tw_spec,
        ],
        out_specs=[any_spec, any_spec],
        out_shape=[jax.ShapeDtypeStruct((n1, n2, hw), jnp.uint32)] * 2,
        scratch_shapes=[
            pltpu.VMEM((2, jb, n1, hw), jnp.uint32),
            pltpu.VMEM((jb, 2 * n1, fw), BF16),
            pltpu.VMEM((2, 2, jb, n1, hw), jnp.uint32),
            pltpu.SemaphoreType.DMA((2,)),
            pltpu.SemaphoreType.DMA((2,)),
        ],
        compiler_params=_cparams("arbitrary"), name="fourier_stage1",
    )(u.reshape(n1, n2, hw), w_c, m1, tw_c, tw_s)
    scale = 1.0 / math.sqrt(t * dc)
    in_blk = pl.BlockSpec((kb * n2, hw), lambda s: (s, 0))
    y = pl.pallas_call(
        functools.partial(_fft2_kernel, n2=n2, scale=scale),
        grid=(n1 // kb,),
        in_specs=[in_blk, in_blk, pl.BlockSpec((n2, 2 * n2), lambda s: (0, 0))],
        out_specs=any_spec,
        out_shape=jax.ShapeDtypeStruct((n2, n1, hw), jnp.uint32),
        scratch_shapes=[pltpu.VMEM((2, kb, n2, hw), jnp.uint32), pltpu.SemaphoreType.DMA((2,))],
        compiler_params=_cparams("arbitrary"), name="fourier_stage2",
    )(b_re.reshape(t, hw), b_im.reshape(t, hw), m2)
    return y.reshape(t, hw)


def _dispatch_kernel(dest_ref, pad0_ref, padn_ref, na_ref, h_ref, xb_hbm, zbuf, sem, zsem, *, chunks):
    i = pl.program_id(0)
    tm = h_ref.shape[0] // chunks
    tb = zbuf.shape[0] // chunks
    nb = xb_hbm.shape[0] // (tb * chunks)

    def tail_copies(fn):
        def body(b, carry):
            fn(pltpu.make_async_copy(zbuf, xb_hbm.at[_token_rows(b, tb * chunks)], zsem))
            return carry
        lax.fori_loop(na_ref[0], nb, body, 0)

    def pad_copies(e, fn):
        pos = pad0_ref[e]
        n = padn_ref[e]
        bit = tb // 2
        while bit >= 1:
            @pl.when((n & bit) != 0)
            def _(pos=pos, bit=bit):
                rows = pl.ds(pl.multiple_of(pos * chunks, chunks), bit * chunks)
                fn(pltpu.make_async_copy(zbuf.at[pl.ds(0, bit * chunks)], xb_hbm.at[rows], zsem))
            pos = pos + (n & bit)
            bit //= 2

    @pl.when(i == 0)
    def _():
        zbuf[...] = jnp.zeros_like(zbuf)
        lax.fori_loop(0, N_EXPERTS, lambda e, c: (pad_copies(e, lambda cp: cp.start()), c)[1], 0)
        tail_copies(lambda cp: cp.start())

    def row_copy(r, k):
        dst = dest_ref[TOP_K * (i * tm + r) + k]
        return pltpu.make_async_copy(h_ref.at[_token_rows(r, chunks)], xb_hbm.at[_token_rows(dst, chunks)], sem)

    def for_rows(fn):
        def body(r, carry):
            for k in range(TOP_K):
                fn(r, k)
            return carry
        lax.fori_loop(0, tm, body, 0, unroll=ROW_DMA_UNROLL)

    for_rows(lambda r, k: row_copy(r, k).start(priority=k))
    for_rows(lambda r, k: row_copy(r, k).wait())

    @pl.when(i == 0)
    def _():
        lax.fori_loop(0, N_EXPERTS, lambda e, c: (pad_copies(e, lambda cp: cp.wait()), c)[1], 0)
        tail_copies(lambda cp: cp.wait())


def _pack_bf16_pairs(h):
    half = h.shape[1] // 2
    bits = lax.bitcast_convert_type(h.astype(BF16).astype(F32), jnp.uint32)
    return bits[:, half:] | (bits[:, :half] >> 16)


def _unpack_bf16_pairs(p):
    lo = lax.bitcast_convert_type(p << 16, F32)
    hi = lax.bitcast_convert_type(p & jnp.uint32(0xFFFF0000), F32)
    return jnp.concatenate([lo, hi], axis=1)


def _expert_kernel(be_ref, bi_ref, first_ref, par_ref, nxt1_ref, nxt2_ref, na_ref,
                   x_ref, wg_hbm, wu_hbm, wd_hbm, y_ref,
                   wg_st, wu_st, wd_st, wg_c, wu_c, wd_c, wsem, *, layer):
    b = pl.program_id(0)
    active = b < na_ref[0]
    d = wg_c.shape[0]
    xchunks = d // LANES // 2
    tb = x_ref.shape[0] // xchunks

    def weight_copies(e, slot):
        return [pltpu.make_async_copy(src.at[layer, e], dst.at[slot], wsem.at[slot, n])
                for n, (src, dst) in enumerate(((wg_hbm, wg_st), (wu_hbm, wu_st), (wd_hbm, wd_st)))]

    def start_weights(e, slot):
        for cp, priority in zip(weight_copies(e, slot), (1, 1, 0)):
            cp.start(priority=priority)

    @pl.when(b == 0)
    def _():
        start_weights(be_ref[0], 0)

        @pl.when(nxt1_ref[0] >= 0)
        def _():
            start_weights(nxt1_ref[0], 1)

    is_first = first_ref[b] == 1

    def block(refresh):
        slot = par_ref[b]
        x = _unpack_bf16_pairs(_load_token_major(x_ref, tb, xchunks)).astype(BF16)
        if refresh:
            wg_c[...] = wg_st[slot].astype(BF16)
        gate = jnp.dot(x, wg_c[...], preferred_element_type=F32)
        if refresh:
            wu_c[...] = wu_st[slot].astype(BF16)
        up = jnp.dot(x, wu_c[...], preferred_element_type=F32)
        act = (gate * _sigmoid(gate) * up).astype(BF16)
        if refresh:
            wd_c[...] = wd_st[slot].astype(BF16)
        _store_token_major(y_ref, _pack_bf16_pairs(jnp.dot(act, wd_c[...], preferred_element_type=F32)))

    @pl.when(active & is_first)
    def _():
        slot = par_ref[b]
        for cp in weight_copies(be_ref[b], slot):
            cp.wait()
        block(True)

        @pl.when(nxt2_ref[b] >= 0)
        def _():
            start_weights(nxt2_ref[b], slot)

    @pl.when(active & jnp.logical_not(is_first))
    def _():
        block(False)

    @pl.when(jnp.logical_not(active))
    def _():
        y_ref[...] = jnp.zeros_like(y_ref)


def _moe_plan(experts, t, tb):
    a = t * TOP_K
    e_flat = experts.reshape(a)
    lanes = jnp.arange(N_EXPERTS, dtype=jnp.int32)[None, :]
    onehot = (e_flat[:, None] == lanes).astype(jnp.int32)
    csum = jnp.cumsum(onehot, axis=0)
    counts = csum[-1]
    pcounts = (counts + tb - 1) // tb * tb
    pends = jnp.cumsum(pcounts)
    pstarts = pends - pcounts
    dest = jnp.sum(onehot * (pstarts[None, :] + csum - 1), axis=1).astype(jnp.int32)
    nb = (a + N_EXPERTS * tb) // tb
    n_act = (pends[-1] // tb).astype(jnp.int32)
    blk = jnp.arange(nb, dtype=jnp.int32)
    blk_idx = jnp.minimum(blk, n_act - 1)
    be = jnp.sum((pends[None, :] <= (blk_idx * tb)[:, None]).astype(jnp.int32), axis=1)
    be = jnp.minimum(be, N_EXPERTS - 1).astype(jnp.int32)
    first = ((blk * tb == pstarts[be]) & (blk < n_act)).astype(jnp.int32)
    parity = ((jnp.cumsum(first) - 1) & 1).astype(jnp.int32)

    def next_expert(e):
        after = pends[jnp.maximum(e, 0)] // tb
        return jnp.where((e >= 0) & (after < n_act), be[jnp.minimum(after, nb - 1)], -1).astype(jnp.int32)

    nxt1 = next_expert(be)
    nxt2 = next_expert(nxt1)
    plan = dict(dest=dest, be=be, blk_idx=blk_idx, first=first, parity=parity, nxt1=nxt1, nxt2=nxt2,
                n_act=n_act.reshape(1),
                pad0=(pstarts + counts).astype(jnp.int32), padn=(pcounts - counts).astype(jnp.int32))
    return plan


def _experts(h, experts, layer, w_gate, w_up, w_down):
    d, de = w_gate.shape[2:]
    chunks = d // LANES
    xchunks = chunks // 2
    t = h.shape[0] // xchunks
    tb = EXPERT_TB
    tm = min(PRE_TM, t)
    plan = _moe_plan(experts, t, tb)
    nb = plan["be"].shape[0]
    p = nb * tb
    xb = pl.pallas_call(
        functools.partial(_dispatch_kernel, chunks=xchunks),
        grid_spec=pltpu.PrefetchScalarGridSpec(
            num_scalar_prefetch=4, grid=(t // tm,),
            in_specs=[pl.BlockSpec((tm * xchunks, LANES), lambda i, *_: (i, 0))],
            out_specs=pl.BlockSpec(memory_space=pl.ANY),
            scratch_shapes=[pltpu.VMEM((tb * xchunks, LANES), h.dtype),
                            pltpu.SemaphoreType.DMA(()), pltpu.SemaphoreType.DMA(())]),
        out_shape=jax.ShapeDtypeStruct((p * xchunks, LANES), h.dtype),
        compiler_params=_cparams("arbitrary"), name="moe_dispatch",
    )(plan["dest"], plan["pad0"], plan["padn"], plan["n_act"], h)
    hbm = pl.BlockSpec(memory_space=pl.ANY)
    yb = pl.pallas_call(
        functools.partial(_expert_kernel, layer=layer),
        grid_spec=pltpu.PrefetchScalarGridSpec(
            num_scalar_prefetch=7, grid=(nb,),
            in_specs=[pl.BlockSpec((tb * xchunks, LANES), lambda b, be, bi, *_: (bi[b], 0)), hbm, hbm, hbm],
            out_specs=pl.BlockSpec((tb * xchunks, LANES), lambda b, *_: (b, 0)),
            scratch_shapes=[pltpu.VMEM((2, d, de), F32), pltpu.VMEM((2, d, de), F32), pltpu.VMEM((2, de, d), F32),
                            pltpu.VMEM((d, de), BF16), pltpu.VMEM((d, de), BF16), pltpu.VMEM((de, d), BF16),
                            pltpu.SemaphoreType.DMA((2, 3))]),
        out_shape=jax.ShapeDtypeStruct((p * xchunks, LANES), jnp.uint32),
        compiler_params=_cparams("arbitrary"), name="moe_experts",
    )(plan["be"], plan["blk_idx"], plan["first"], plan["parity"], plan["nxt1"], plan["nxt2"], plan["n_act"],
      xb, w_gate, w_up, w_down)
    return yb, plan["dest"]


def kernel(x, c, norm1_g, norm2_g, ada_w, ada_b, mix_in_w, na_rpb, na_out_w, fourier_out_w, branch_gate_w,
           branch_gate_b, mix_out_w, router_group_w, router_group_b, router_expert_w, router_expert_b,
           expert_w_gate, expert_w_up, expert_w_down, final_g):
    bsz, s, d = x.shape
    assert bsz == 1 and s % GRID_W == 0
    depth = ada_w.shape[0]
    na_w = NA_HEADS * NA_HEAD_DIM
    f_w = F_GROUPS * F_GROUP_DIM
    xs = x.reshape(s, d)
    mod_l = _ada_first(c, ada_w, ada_b, 1)[0]
    pad = ROUTER_LANES - N_GROUPS - N_EXPERTS
    wr_all = jnp.concatenate([router_group_w, router_expert_w, jnp.zeros((depth, d, pad), F32)], axis=2)
    wr_hi = wr_all.astype(BF16)
    wr_lo = (wr_all - wr_hi.astype(F32)).astype(BF16)
    wr_all = jnp.concatenate([wr_hi, wr_lo], axis=2)
    br_all = jnp.concatenate([router_group_b, router_expert_b, jnp.zeros((depth, pad), F32)], axis=1)
    moe = g2 = None
    for l in range(depth):
        sh1, sc1, g1, sh2, sc2, g2_l = [mod_l[:, i * d:(i + 1) * d] for i in range(N_MOD)]
        if moe is None:
            (h,) = _pre(xs, norm1_g[l], shift=sh1, scale=sc1)
        else:
            xs, h = _pre(xs, norm1_g[l], moe=moe, g2=g2, shift=sh1, scale=sc1)
        side = (c, ada_w, ada_b, l + 1) if l + 1 < depth else None
        qkv = _mm([(h, mix_in_w, l, 0)], [], lambda prod: prod(0), 3 * na_w, BF16, "mix_in_qkv", tn=MM_TN_WIDE,
                  side=side)
        if side is not None:
            qkv, mod_l = qkv
        u = _mm([(h, mix_in_w, l, 3 * na_w)], [], lambda prod: prod(0), f_w, jnp.uint32, "mix_in_fourier",
                tn=f_w)
        y_att = _attention(qkv, na_rpb[l])
        y_fft = _fourier(u)
        bias_bg = branch_gate_b[l].reshape(1, -1)
        mixed = _mm([(h, branch_gate_w, l, 0), (h, branch_gate_w, l, d),
                     (y_att, na_out_w, l, 0), (y_fft, fourier_out_w, l, 0)],
                    [(bias_bg, _rowvec, _vec_cols(0)), (bias_bg, _rowvec, _vec_cols(d))],
                    lambda prod, ba, bf: (_sigmoid(prod(0) + ba) * prod(2) + _sigmoid(prod(1) + bf) * prod(3)),
                    d, BF16, "gated_branch_mix", tn=MM_TN, tm=MM_TM_FUSED)
        xs = _mm([(mixed, mix_out_w, l, 0)],
                 [(xs, _tile, _at_cols(0)), (g1, _rowvec, _vec_cols(0))],
                 lambda prod, xr, g: xr + g * prod(0), d, F32, "mix_out", tn=MM_TN_WIDE)
        h2, wts, ids = _pre(xs, norm2_g[l], shift=sh2, scale=sc2, wr=wr_all[l], br=br_all[l].reshape(1, -1),
                            h_dtype=F32)
        yb, dest = _experts(h2, ids[:, :TOP_K], l, expert_w_gate, expert_w_up, expert_w_down)
        moe, g2 = (yb, dest, wts), g2_l
    _, out = _pre(xs, final_g, moe=moe, g2=g2, h_dtype=F32)
    return out.reshape(bsz, s, d)
```

```python
import functools
import math

import numpy as np
import jax
import jax.numpy as jnp
from jax import lax
from jax.experimental import pallas as pl
from jax.experimental.pallas import tpu as pltpu

F32 = jnp.float32
BF16 = jnp.bfloat16

GRID_W = 64
NA_HEADS = 16
NA_HEAD_DIM = 64
WIN_ROWS = 8
WIN_COLS = 16
F_GROUPS = 4
F_GROUP_DIM = 256
N_GROUPS = 4
EXPERTS_PER_GROUP = 8
N_EXPERTS = N_GROUPS * EXPERTS_PER_GROUP
TOP_K = 2
N_MOD = 6
EPS = 1e-6
NEG_INF = -1e30

LANES = 128
SUBLANES = 8

ADA_TN = 1536
ADA_ROWS = 256
PRE_TM = 512
COMBINE_TM = 256
MM_TM = 1024
MM_TM_FUSED = 512
MM_TN = 512
MM_TN_WIDE = 1024
ATTN_ROWS = 64
EXPERT_TB = 256
ROW_DMA_UNROLL = 8
FFT_JB = 8
FFT_KB = 16
ROUTER_LANES = LANES

VMEM_LIMIT = 56 * 1024 * 1024


def _cparams(*sem):
    return pltpu.CompilerParams(dimension_semantics=sem, vmem_limit_bytes=VMEM_LIMIT)


def _sigmoid(x):
    return 1.0 / (1.0 + jnp.exp(-x))


def _ada_kernel(c_ref, w_ref, b_ref, o_ref, cs_ref):
    c = c_ref[...]
    cs_ref[...] = c * _sigmoid(c)
    d, tn = w_ref.shape

    def body(i, acc):
        r0 = pl.multiple_of(i * ADA_ROWS, ADA_ROWS)
        prod = w_ref[pl.ds(r0, ADA_ROWS), :] * cs_ref[pl.ds(r0, ADA_ROWS), :]
        return acc + jnp.sum(prod.reshape(ADA_ROWS // SUBLANES, SUBLANES, tn), axis=0)

    acc = lax.fori_loop(0, d // ADA_ROWS, body, jnp.zeros((SUBLANES, tn), F32))
    o_ref[...] = jnp.sum(acc, axis=0, keepdims=True) + b_ref[...]


def _ada_first(c, ada_w, ada_b, n_layers):
    _, d, n = ada_w.shape
    nl = n_layers
    tn = min(ADA_TN, n)
    return pl.pallas_call(
        _ada_kernel,
        grid=(nl, n // tn),
        in_specs=[
            pl.BlockSpec((d, 1), lambda l, j: (0, 0)),
            pl.BlockSpec((None, d, tn), lambda l, j: (l, 0, j)),
            pl.BlockSpec((None, 1, tn), lambda l, j: (l, 0, j)),
        ],
        out_specs=pl.BlockSpec((None, 1, tn), lambda l, j: (l, 0, j)),
        out_shape=jax.ShapeDtypeStruct((nl, 1, n), F32),
        scratch_shapes=[pltpu.VMEM((d, 1), F32)],
        compiler_params=_cparams("parallel", "parallel"),
        name="ada_proj",
    )(c.reshape(d, 1), ada_w, ada_b.reshape(ada_b.shape[0], 1, n))


def _rms(x, g):
    return x * lax.rsqrt(jnp.mean(x * x, axis=-1, keepdims=True) + EPS) * g


def _router(h, wr_ref, br_ref):
    w_split = wr_ref[...]
    h_hi = h.astype(BF16)
    h_lo = (h - h_hi.astype(F32)).astype(BF16)
    dot = functools.partial(jnp.dot, preferred_element_type=F32)
    both = dot(h_hi, w_split)
    logits = both[:, :ROUTER_LANES] + (both[:, ROUTER_LANES:] + dot(h_lo, w_split[:, :ROUTER_LANES])) + br_ref[...]
    lane = lax.broadcasted_iota(jnp.int32, logits.shape, 1)
    big = jnp.int32(ROUTER_LANES)
    is_g = lane < N_GROUPS
    gl = jnp.where(is_g, logits, NEG_INF)
    gmax = jnp.max(gl, axis=-1, keepdims=True)
    gsum = jnp.sum(jnp.where(is_g, jnp.exp(gl - gmax), 0.0), axis=-1, keepdims=True)
    g_top_p = 1.0 / gsum
    g_top = jnp.min(jnp.where(is_g & (gl == gmax), lane, big), axis=-1, keepdims=True)
    e_lane = lane - N_GROUPS
    sel = (e_lane >= 0) & (e_lane < N_EXPERTS) & ((e_lane >> 3) == g_top)
    l1 = jnp.where(sel, logits, NEG_INF)
    m1 = jnp.max(l1, axis=-1, keepdims=True)
    i1 = jnp.min(jnp.where(sel & (l1 == m1), lane, big), axis=-1, keepdims=True)
    sel2 = sel & (lane != i1)
    l2 = jnp.where(sel2, logits, NEG_INF)
    m2 = jnp.max(l2, axis=-1, keepdims=True)
    i2 = jnp.min(jnp.where(sel2 & (l2 == m2), lane, big), axis=-1, keepdims=True)
    e21 = jnp.exp(m2 - m1)
    p1 = 1.0 / (1.0 + e21)
    p2 = e21 / (1.0 + e21)
    wts = jnp.where(lane == 0, g_top_p * p1, jnp.where(lane == 1, g_top_p * p2, 0.0))
    ids = jnp.where(lane == 0, i1 - N_GROUPS, jnp.where(lane == 1, i2 - N_GROUPS, 0))
    return wts, ids


def _load_token_major(ref, n_tok, chunks):
    return jnp.concatenate([ref[pl.ds(c, n_tok, stride=chunks), :] for c in range(chunks)], axis=1)


def _store_token_major(ref, val):
    n_tok, d = val.shape
    chunks = d // LANES
    for c in range(chunks):
        ref[pl.ds(c, n_tok, stride=chunks), :] = val[:, c * LANES:(c + 1) * LANES]


def _token_rows(tok, chunks):
    return pl.ds(pl.multiple_of(tok * chunks, chunks), chunks)


def _pre_kernel(*refs, combine, modulate, router, emit_h, h_dtype):
    it = iter(refs)
    if combine:
        dest_ref = next(it)
    x_ref = next(it)
    if combine:
        y_hbm, cw_ref, g2_ref = next(it), next(it), next(it)
    g_ref = next(it)
    if modulate:
        sh_ref, sc_ref = next(it), next(it)
    if router:
        wr_ref, br_ref = next(it), next(it)
    outs = [next(it) for _ in range(int(combine) + int(emit_h) + 2 * int(router))]
    x = x_ref[...]
    if combine:
        gbuf, gsem = next(it), next(it)
        i = pl.program_id(0)
        slot = i % 2
        tm, d = x_ref.shape
        chunks = d // LANES // 2

        def row_copy(step, sl, r, k):
            src = dest_ref[TOP_K * (step * tm + r) + k]
            return pltpu.make_async_copy(y_hbm.at[_token_rows(src, chunks)],
                                         gbuf.at[sl, k, _token_rows(r, chunks)], gsem.at[sl])

        def for_rows(fn):
            def body(r, carry):
                for k in range(TOP_K):
                    fn(r, k)
                return carry
            lax.fori_loop(0, tm, body, 0, unroll=ROW_DMA_UNROLL)

        @pl.when(i == 0)
        def _():
            for_rows(lambda r, k: row_copy(0, 0, r, k).start(priority=k))

        @pl.when(i + 1 < pl.num_programs(0))
        def _():
            for_rows(lambda r, k: row_copy(i + 1, 1 - slot, r, k).start(priority=k))

        for_rows(lambda r, k: row_copy(i, slot, r, k).wait())
        cw = cw_ref[...]
        y0 = _unpack_bf16_pairs(_load_token_major(gbuf.at[slot, 0], tm, chunks))
        y1 = _unpack_bf16_pairs(_load_token_major(gbuf.at[slot, 1], tm, chunks))
        x = x + g2_ref[...] * (cw[:, 0:1] * y0 + cw[:, 1:2] * y1)
        outs.pop(0)[...] = x
    h = _rms(x, g_ref[...])
    if modulate:
        h = h * (1.0 + sc_ref[...]) + sh_ref[...]
    if emit_h and router:
        _store_token_major(outs.pop(0), _pack_bf16_pairs(h))
    elif emit_h:
        outs.pop(0)[...] = h.astype(h_dtype)
    if router:
        wts, ids = _router(h, wr_ref, br_ref)
        outs.pop(0)[...] = wts
        outs.pop(0)[...] = ids


def _pre(x, g, *, moe=None, g2=None, shift=None, scale=None, wr=None, br=None, h_dtype=BF16):
    t, d = x.shape
    combine, modulate, router = moe is not None, shift is not None, wr is not None
    tm = min(COMBINE_TM if combine else PRE_TM, t)
    chunks = d // LANES
    row = pl.BlockSpec((tm, d), lambda i, *_: (i, 0))
    vec = pl.BlockSpec((1, d), lambda i, *_: (0, 0))
    lane_row = pl.BlockSpec((tm, ROUTER_LANES), lambda i, *_: (i, 0))
    prefetch, args, specs, scratch = [], [x], [row], []
    if combine:
        yb, dest, cw = moe
        prefetch.append(dest)
        args += [yb, cw, g2]
        specs += [pl.BlockSpec(memory_space=pl.ANY), lane_row, vec]
        scratch = [pltpu.VMEM((2, TOP_K, tm * chunks // 2, LANES), jnp.uint32), pltpu.SemaphoreType.DMA((2,))]
    args.append(g.reshape(1, d))
    specs.append(vec)
    if modulate:
        args += [shift, scale]
        specs += [vec, vec]
    if router:
        args += [wr, br]
        specs += [pl.BlockSpec(wr.shape, lambda i, *_: (0, 0)), pl.BlockSpec(br.shape, lambda i, *_: (0, 0))]
    out_shape, out_specs = [], []
    if combine:
        out_shape.append(jax.ShapeDtypeStruct((t, d), F32))
        out_specs.append(row)
    if router:
        out_shape.append(jax.ShapeDtypeStruct((t * chunks // 2, LANES), jnp.uint32))
        out_specs.append(pl.BlockSpec((tm * chunks // 2, LANES), lambda i, *_: (i, 0)))
    else:
        out_shape.append(jax.ShapeDtypeStruct((t, d), h_dtype))
        out_specs.append(row)
    if router:
        out_shape += [jax.ShapeDtypeStruct((t, ROUTER_LANES), F32), jax.ShapeDtypeStruct((t, ROUTER_LANES), jnp.int32)]
        out_specs += [lane_row, lane_row]
    kern = functools.partial(_pre_kernel, combine=combine, modulate=modulate, router=router,
                             emit_h=True, h_dtype=h_dtype)
    grid_spec = pltpu.PrefetchScalarGridSpec(
        num_scalar_prefetch=len(prefetch), grid=(t // tm,), in_specs=specs, out_specs=out_specs,
        scratch_shapes=scratch)
    return pl.pallas_call(
        kern, grid_spec=grid_spec, out_shape=out_shape,
        compiler_params=_cparams("arbitrary" if combine else "parallel"), name="pre_norm",
    )(*prefetch, *args)


def _mm_kernel(*refs, lhs_of, n_lhs, n_extras, epilogue, side):
    n_pairs = len(lhs_of)
    lhs = refs[:n_lhs]
    weights = refs[n_lhs:n_lhs + n_pairs]
    extras = refs[n_lhs + n_pairs:n_lhs + n_pairs + n_extras]
    pos = n_lhs + n_pairs + n_extras
    if side:
        c_ref, aw_ref, ab_ref = refs[pos:pos + 3]
        pos += 3
    o_ref = refs[pos]
    pos += 1
    if side:
        mod_ref = refs[pos]
        pos += 1
        c = c_ref[...]
        cs = c * _sigmoid(c)
        d, sn = aw_ref.shape
        acc = jnp.zeros((SUBLANES, sn), F32)
        for r0 in range(0, d, ADA_ROWS):
            prod = aw_ref[r0:r0 + ADA_ROWS, :] * cs[r0:r0 + ADA_ROWS]
            acc = acc + jnp.sum(prod.reshape(ADA_ROWS // SUBLANES, SUBLANES, sn), axis=0)
        mod_ref[...] = jnp.sum(acc, axis=0, keepdims=True) + ab_ref[...]
    caches = refs[pos:]

    @pl.when(pl.program_id(1) == 0)
    def _():
        for p in range(n_pairs):
            caches[p][...] = weights[p][...].astype(BF16)

    def product(p):
        a = lhs[lhs_of[p]][...]
        a = _unpack_bf16_pairs(a) if a.dtype == jnp.uint32 else a
        return jnp.dot(a.astype(BF16), caches[p][...], preferred_element_type=F32)

    out = epilogue(product, *[r[...] for r in extras])
    o_ref[...] = _pack_bf16_pairs(out) if o_ref.dtype == jnp.uint32 else out.astype(o_ref.dtype)


def _mm(pairs, extras, epilogue, n, out_dtype, name, tn=MM_TN, tm=MM_TM, side=None):
    m = pairs[0][0].shape[0]
    tm, tn = min(tm, m), min(tn, n)
    packed_out = out_dtype == jnp.uint32
    assert not packed_out or tn == n
    lhs, lhs_of, w_args, w_specs, scratch = [], [], [], [], []
    for a, w, layer, off in pairs:
        k = a.shape[1] * (2 if a.dtype == jnp.uint32 else 1)
        if not any(a is seen for seen in lhs):
            lhs.append(a)
        lhs_of.append([a is seen for seen in lhs].index(True))
        w_args.append(w)
        w_specs.append(pl.BlockSpec((None, k, tn), lambda j, i, layer=layer, off=off: (layer, 0, j + off // tn)))
        scratch.append(pltpu.VMEM((k, tn), BF16))
    args = lhs + w_args
    specs = [pl.BlockSpec((tm, a.shape[1]), lambda j, i: (i, 0)) for a in lhs] + w_specs
    for arr, bshape, imap in extras:
        args.append(arr)
        specs.append(pl.BlockSpec(bshape(tm, tn), imap(tn)))
    nj, ni = n // tn, m // tm
    out_specs = [pl.BlockSpec((tm, tn // 2 if packed_out else tn), lambda j, i: (i, j))]
    out_shape = [jax.ShapeDtypeStruct((m, n // 2 if packed_out else n), out_dtype)]
    if side is not None:
        c, ada_w, ada_b, layer = side
        d, width = ada_w.shape[1:]
        sn = width // (nj * ni)
        assert sn * nj * ni == width and sn % LANES == 0
        args += [c.reshape(d, 1), ada_w, ada_b.reshape(ada_b.shape[0], 1, width)]
        specs += [pl.BlockSpec((d, 1), lambda j, i: (0, 0)),
                  pl.BlockSpec((None, d, sn), lambda j, i: (layer, 0, j * ni + i)),
                  pl.BlockSpec((None, 1, sn), lambda j, i: (layer, 0, j * ni + i))]
        out_specs.append(pl.BlockSpec((1, sn), lambda j, i: (0, j * ni + i)))
        out_shape.append(jax.ShapeDtypeStruct((1, width), F32))
    outs = pl.pallas_call(
        functools.partial(_mm_kernel, lhs_of=tuple(lhs_of), n_lhs=len(lhs), n_extras=len(extras),
                          epilogue=epilogue, side=side is not None),
        grid=(nj, ni), in_specs=specs, out_specs=out_specs, out_shape=out_shape,
        scratch_shapes=scratch,
        compiler_params=_cparams("arbitrary", "arbitrary"), name=name,
    )(*args)
    return outs[0] if side is None else outs


def _tile(tm, tn):
    return (tm, tn)


def _rowvec(tm, tn):
    return (1, tn)


def _at_cols(off):
    return lambda tn: (lambda j, i: (i, j + off // tn))


def _vec_cols(off):
    return lambda tn: (lambda j, i: (0, j + off // tn))


def _attn_bias_fill(rpb_ref, b_ref, head0):
    n_dr, n_dc = 2 * WIN_ROWS - 1, 2 * WIN_COLS - 1
    qc = lax.broadcasted_iota(jnp.int32, (GRID_W, 2 * GRID_W), 0)
    kc = lax.broadcasted_iota(jnp.int32, (GRID_W, 2 * GRID_W), 1) & (GRID_W - 1)
    cs = jnp.clip(qc - WIN_COLS // 2, 0, GRID_W - WIN_COLS)
    inside = (kc >= cs) & (kc < cs + WIN_COLS)
    dc = jnp.clip(kc - qc, -(WIN_COLS - 1), WIN_COLS - 1) + (WIN_COLS - 1)
    lane = lax.broadcasted_iota(jnp.int32, (SUBLANES, 2 * GRID_W), 1)
    for h in range(2):
        base = (head0 + h) * (n_dr * n_dc)
        for dr in range(n_dr):
            row = jnp.zeros((SUBLANES, 2 * GRID_W), F32)
            for d in range(n_dc):
                row = jnp.where(lane == d, rpb_ref[base + dr * n_dc + d], row)
            table = jnp.concatenate([row] * (GRID_W // SUBLANES), axis=0)
            t = jnp.where(inside, jnp.take_along_axis(table, dc, axis=1), NEG_INF)
            for var in range(WIN_ROWS):
                j = dr - var
                if 0 <= j < WIN_ROWS:
                    c0 = j * GRID_W
                    lo = c0 % (2 * GRID_W)
                    b_ref[var, h * GRID_W:(h + 1) * GRID_W, c0:c0 + GRID_W] = t[:, lo:lo + GRID_W]


def _attn_kernel(rpb_ref, q_ref, k_ref, v_ref, o_ref, b_ref, s_ref, *, n_rows):
    rb = pl.program_id(1)
    n_q = q_ref.shape[0]
    wkeys = WIN_ROWS * GRID_W
    dh = NA_HEAD_DIM
    lane = lax.broadcasted_iota(jnp.int32, (GRID_W, 2 * dh), 1)
    first = lane < dh

    @pl.when(rb == 0)
    def _():
        _attn_bias_fill(rpb_ref, b_ref, 2 * pl.program_id(0))

    def window(i):
        r = rb * n_q + i
        rs = jnp.clip(r - WIN_ROWS // 2, 0, n_rows - WIN_ROWS)
        return rs, rs - r + (WIN_ROWS - 1)

    for i in range(n_q):
        rs, var = window(i)
        kw = k_ref[pl.ds(rs, WIN_ROWS)].reshape(wkeys, 2 * dh)
        q = q_ref[i] * jnp.asarray(dh ** -0.5, BF16)
        zero = jnp.zeros_like(q)
        q2 = jnp.concatenate([jnp.where(first, q, zero), jnp.where(first, zero, q)], axis=0)
        s = lax.dot_general(q2, kw, (((1,), (1,)), ((), ())), preferred_element_type=F32)
        b = b_ref[var]
        s_ref[i] = s + b
    for i in range(n_q):
        rs, _ = window(i)
        vw = v_ref[pl.ds(rs, WIN_ROWS)].reshape(wkeys, 2 * dh)
        s = s_ref[i]
        m = jnp.max(s, axis=-1, keepdims=True)
        p = jnp.exp(s - m)
        l = jnp.sum(p, axis=-1, keepdims=True)
        o = jnp.dot(p.astype(BF16), vw, preferred_element_type=F32) / l
        o_ref[i] = jnp.where(first, o[:GRID_W], o[GRID_W:]).astype(o_ref.dtype)


def _attention(proj, rpb):
    t, width = proj.shape
    n_rows = t // GRID_W
    pairs = NA_HEADS // 2
    p3 = proj.reshape(n_rows, GRID_W, width)
    blk = 2 * NA_HEAD_DIM
    rows = min(ATTN_ROWS, n_rows)
    wkeys = WIN_ROWS * GRID_W
    out = pl.pallas_call(
        functools.partial(_attn_kernel, n_rows=n_rows),
        grid_spec=pltpu.PrefetchScalarGridSpec(
            num_scalar_prefetch=1, grid=(pairs, n_rows // rows),
            in_specs=[
                pl.BlockSpec((rows, GRID_W, blk), lambda hp, rb, _: (rb, 0, hp)),
                pl.BlockSpec((n_rows, GRID_W, blk), lambda hp, rb, _: (0, 0, pairs + hp)),
                pl.BlockSpec((n_rows, GRID_W, blk), lambda hp, rb, _: (0, 0, 2 * pairs + hp)),
            ],
            out_specs=pl.BlockSpec((rows, GRID_W, blk), lambda hp, rb, _: (rb, 0, hp)),
            scratch_shapes=[pltpu.VMEM((WIN_ROWS, 2 * GRID_W, wkeys), F32),
                            pltpu.VMEM((rows, 2 * GRID_W, wkeys), F32)]),
        out_shape=jax.ShapeDtypeStruct((n_rows, GRID_W, NA_HEADS * NA_HEAD_DIM), BF16),
        compiler_params=_cparams("arbitrary", "arbitrary"), name="na_attention",
    )(rpb.reshape(-1), p3, p3, p3)
    return out.reshape(t, NA_HEADS * NA_HEAD_DIM)


def _dft_tables(n1, n2, dc):
    s = n1 * n2
    a = np.arange(dc, dtype=np.float64)
    ang_c = 2 * np.pi * np.outer(a, a) / dc
    w_c = np.concatenate([np.cos(ang_c), -np.sin(ang_c)], axis=1)
    a1 = np.arange(n1, dtype=np.float64)
    ang1 = 2 * np.pi * np.outer(a1, a1) / n1
    c1, s1 = np.cos(ang1), np.sin(ang1)
    m1 = np.block([[c1, s1], [-s1, c1]])
    a2 = np.arange(n2, dtype=np.float64)
    ang_t = 2 * np.pi * np.outer(a2, a1) / s
    tw_c = np.repeat(np.cos(ang_t)[:, :, None], LANES, axis=2)
    tw_s = np.repeat(np.sin(ang_t)[:, :, None], LANES, axis=2)
    ang2 = 2 * np.pi * np.outer(a2, a2) / n2
    m2 = np.concatenate([np.cos(ang2), np.sin(ang2)], axis=1)
    return (jnp.asarray(w_c, BF16), jnp.asarray(m1, BF16), jnp.asarray(tw_c, F32), jnp.asarray(tw_s, F32),
            jnp.asarray(m2, BF16))


def _fft1_kernel(u_hbm, wc_ref, m1_ref, tc_ref, ts_ref, br_hbm, bi_hbm,
                 ubuf, zz_ref, obuf, isem, osem, *, n1, dc):
    s = pl.program_id(0)
    ns = pl.num_programs(0)
    jb = ubuf.shape[1]
    slot = s % 2

    def in_copy(step, sl, jj):
        return pltpu.make_async_copy(u_hbm.at[:, step * jb + jj], ubuf.at[sl, jj], isem.at[sl])

    def out_copies(step, sl, jj):
        j = step * jb + jj
        return (pltpu.make_async_copy(obuf.at[sl, 0, jj], br_hbm.at[:, j], osem.at[sl]),
                pltpu.make_async_copy(obuf.at[sl, 1, jj], bi_hbm.at[:, j], osem.at[sl]))

    @pl.when(s == 0)
    def _():
        for jj in range(jb):
            in_copy(0, 0, jj).start()

    @pl.when(s + 1 < ns)
    def _():
        for jj in range(jb):
            in_copy(s + 1, 1 - slot, jj).start()

    for jj in range(jb):
        in_copy(s, slot, jj).wait()

    @pl.when(s >= 2)
    def _():
        for jj in range(jb):
            for cp in out_copies(s - 2, slot, jj):
                cp.wait()

    c = 2 * ubuf.shape[3]
    u = _unpack_bf16_pairs(ubuf[slot].reshape(jb * n1, c // 2)).astype(BF16)
    for g in range(c // dc):
        z = jnp.dot(u[:, g * dc:(g + 1) * dc], wc_ref[...], preferred_element_type=F32)
        z = z.astype(BF16)
        for jj in range(jb):
            zz_ref[jj, :n1, g * dc:(g + 1) * dc] = z[jj * n1:(jj + 1) * n1, :dc]
            zz_ref[jj, n1:, g * dc:(g + 1) * dc] = z[jj * n1:(jj + 1) * n1, dc:]
    for jj in range(jb):
        a = jnp.dot(m1_ref[...], zz_ref[jj], preferred_element_type=F32)
        ar, ai = a[:n1], a[n1:]
        tc = jnp.concatenate([tc_ref[jj]] * (c // LANES), axis=1)
        ts = jnp.concatenate([ts_ref[jj]] * (c // LANES), axis=1)
        obuf[slot, 0, jj] = _pack_bf16_pairs(ar * tc + ai * ts)
        obuf[slot, 1, jj] = _pack_bf16_pairs(ai * tc - ar * ts)
    for jj in range(jb):
        for cp in out_copies(s, slot, jj):
            cp.start()

    @pl.when(s == ns - 1)
    def _():
        for jj in range(jb):
            for cp in out_copies(s, slot, jj):
                cp.wait()

        @pl.when(s >= 1)
        def _():
            for jj in range(jb):
                for cp in out_copies(s - 1, 1 - slot, jj):
                    cp.wait()


def _fft2_kernel(br_ref, bi_ref, m2_ref, y_hbm, obuf, osem, *, n2, scale):
    s = pl.program_id(0)
    ns = pl.num_programs(0)
    kb = obuf.shape[1]
    slot = s % 2

    def out_copy(step, sl, kk):
        return pltpu.make_async_copy(obuf.at[sl, kk], y_hbm.at[:, step * kb + kk], osem.at[sl])

    @pl.when(s >= 2)
    def _():
        for kk in range(kb):
            out_copy(s - 2, slot, kk).wait()

    for kk in range(kb):
        rows = slice(kk * n2, (kk + 1) * n2)
        bb = _unpack_bf16_pairs(jnp.concatenate([br_ref[rows, :], bi_ref[rows, :]], axis=0)).astype(BF16)
        obuf[slot, kk] = _pack_bf16_pairs(jnp.dot(m2_ref[...], bb, preferred_element_type=F32) * scale)
    for kk in range(kb):
        out_copy(s, slot, kk).start()

    @pl.when(s == ns - 1)
    def _():
        for kk in range(kb):
            out_copy(s, slot, kk).wait()

        @pl.when(s >= 1)
        def _():
            for kk in range(kb):
                out_copy(s - 1, 1 - slot, kk).wait()


def _fourier(u):
    t, hw = u.shape
    fw = 2 * hw
    n2 = GRID_W
    n1 = t // n2
    dc = F_GROUP_DIM
    w_c, m1, tw_c, tw_s, m2 = _dft_tables(n1, n2, dc)
    jb = min(FFT_JB, n2)
    kb = min(FFT_KB, n1)
    any_spec = pl.BlockSpec(memory_space=pl.ANY)
    tw_spec = pl.BlockSpec((jb, n1, LANES), lambda s: (s, 0, 0))
    b_re, b_im = pl.pallas_call(
        functools.partial(_fft1_kernel, n1=n1, dc=dc),
        grid=(n2 // jb,),
        in_specs=[
            any_spec,
            pl.BlockSpec((dc, 2 * dc), lambda s: (0, 0)),
            pl.BlockSpec((2 * n1, 2 * n1), lambda s: (0, 0)),
            tw_spec, tw_spec,
        ],
        out_specs=[any_spec, any_spec],
        out_shape=[jax.ShapeDtypeStruct((n1, n2, hw), jnp.uint32)] * 2,
        scratch_shapes=[
            pltpu.VMEM((2, jb, n1, hw), jnp.uint32),
            pltpu.VMEM((jb, 2 * n1, fw), BF16),
            pltpu.VMEM((2, 2, jb, n1, hw), jnp.uint32),
            pltpu.SemaphoreType.DMA((2,)),
            pltpu.SemaphoreType.DMA((2,)),
        ],
        compiler_params=_cparams("arbitrary"), name="fourier_stage1",
    )(u.reshape(n1, n2, hw), w_c, m1, tw_c, tw_s)
    scale = 1.0 / math.sqrt(t * dc)
    in_blk = pl.BlockSpec((kb * n2, hw), lambda s: (s, 0))
    y = pl.pallas_call(
        functools.partial(_fft2_kernel, n2=n2, scale=scale),
        grid=(n1 // kb,),
        in_specs=[in_blk, in_blk, pl.BlockSpec((n2, 2 * n2), lambda s: (0, 0))],
        out_specs=any_spec,
        out_shape=jax.ShapeDtypeStruct((n2, n1, hw), jnp.uint32),
        scratch_shapes=[pltpu.VMEM((2, kb, n2, hw), jnp.uint32), pltpu.SemaphoreType.DMA((2,))],
        compiler_params=_cparams("arbitrary"), name="fourier_stage2",
    )(b_re.reshape(t, hw), b_im.reshape(t, hw), m2)
    return y.reshape(t, hw)


def _dispatch_kernel(dest_ref, pad0_ref, padn_ref, na_ref, h_ref, xb_hbm, zbuf, sem, zsem, *, chunks):
    i = pl.program_id(0)
    tm = h_ref.shape[0] // chunks
    tb = zbuf.shape[0] // chunks
    nb = xb_hbm.shape[0] // (tb * chunks)

    def tail_copies(fn):
        def body(b, carry):
            fn(pltpu.make_async_copy(zbuf, xb_hbm.at[_token_rows(b, tb * chunks)], zsem))
            return carry
        lax.fori_loop(na_ref[0], nb, body, 0)

    def pad_copies(e, fn):
        pos = pad0_ref[e]
        n = padn_ref[e]
        bit = tb // 2
        while bit >= 1:
            @pl.when((n & bit) != 0)
            def _(pos=pos, bit=bit):
                rows = pl.ds(pl.multiple_of(pos * chunks, chunks), bit * chunks)
                fn(pltpu.make_async_copy(zbuf.at[pl.ds(0, bit * chunks)], xb_hbm.at[rows], zsem))
            pos = pos + (n & bit)
            bit //= 2

    @pl.when(i == 0)
    def _():
        zbuf[...] = jnp.zeros_like(zbuf)
        lax.fori_loop(0, N_EXPERTS, lambda e, c: (pad_copies(e, lambda cp: cp.start()), c)[1], 0)
        tail_copies(lambda cp: cp.start())

    def row_copy(r, k):
        dst = dest_ref[TOP_K * (i * tm + r) + k]
        return pltpu.make_async_copy(h_ref.at[_token_rows(r, chunks)], xb_hbm.at[_token_rows(dst, chunks)], sem)

    def for_rows(fn):
        def body(r, carry):
            for k in range(TOP_K):
                fn(r, k)
            return carry
        lax.fori_loop(0, tm, body, 0, unroll=ROW_DMA_UNROLL)

    for_rows(lambda r, k: row_copy(r, k).start(priority=k))
    for_rows(lambda r, k: row_copy(r, k).wait())

    @pl.when(i == 0)
    def _():
        lax.fori_loop(0, N_EXPERTS, lambda e, c: (pad_copies(e, lambda cp: cp.wait()), c)[1], 0)
        tail_copies(lambda cp: cp.wait())


def _pack_bf16_pairs(h):
    half = h.shape[1] // 2
    bits = lax.bitcast_convert_type(h.astype(BF16).astype(F32), jnp.uint32)
    return bits[:, half:] | (bits[:, :half] >> 16)


def _unpack_bf16_pairs(p):
    lo = lax.bitcast_convert_type(p << 16, F32)
    hi = lax.bitcast_convert_type(p & jnp.uint32(0xFFFF0000), F32)
    return jnp.concatenate([lo, hi], axis=1)


def _expert_kernel(be_ref, bi_ref, first_ref, par_ref, nxt1_ref, nxt2_ref, na_ref,
                   x_ref, wg_hbm, wu_hbm, wd_hbm, y_ref,
                   wg_st, wu_st, wd_st, wg_c, wu_c, wd_c, wsem, *, layer):
    b = pl.program_id(0)
    active = b < na_ref[0]
    d = wg_c.shape[0]
    xchunks = d // LANES // 2
    tb = x_ref.shape[0] // xchunks

    def weight_copies(e, slot):
        return [pltpu.make_async_copy(src.at[layer, e], dst.at[slot], wsem.at[slot, n])
                for n, (src, dst) in enumerate(((wg_hbm, wg_st), (wu_hbm, wu_st), (wd_hbm, wd_st)))]

    def start_weights(e, slot):
        for cp, priority in zip(weight_copies(e, slot), (1, 1, 0)):
            cp.start(priority=priority)

    @pl.when(b == 0)
    def _():
        start_weights(be_ref[0], 0)

        @pl.when(nxt1_ref[0] >= 0)
        def _():
            start_weights(nxt1_ref[0], 1)

    is_first = first_ref[b] == 1

    def block(refresh):
        slot = par_ref[b]
        x = _unpack_bf16_pairs(_load_token_major(x_ref, tb, xchunks)).astype(BF16)
        if refresh:
            wg_c[...] = wg_st[slot].astype(BF16)
        gate = jnp.dot(x, wg_c[...], preferred_element_type=F32)
        if refresh:
            wu_c[...] = wu_st[slot].astype(BF16)
        up = jnp.dot(x, wu_c[...], preferred_element_type=F32)
        act = (gate * _sigmoid(gate) * up).astype(BF16)
        if refresh:
            wd_c[...] = wd_st[slot].astype(BF16)
        _store_token_major(y_ref, _pack_bf16_pairs(jnp.dot(act, wd_c[...], preferred_element_type=F32)))

    @pl.when(active & is_first)
    def _():
        slot = par_ref[b]
        for cp in weight_copies(be_ref[b], slot):
            cp.wait()
        block(True)

        @pl.when(nxt2_ref[b] >= 0)
        def _():
            start_weights(nxt2_ref[b], slot)

    @pl.when(active & jnp.logical_not(is_first))
    def _():
        block(False)

    @pl.when(jnp.logical_not(active))
    def _():
        y_ref[...] = jnp.zeros_like(y_ref)


def _moe_plan(experts, t, tb):
    a = t * TOP_K
    e_flat = experts.reshape(a)
    lanes = jnp.arange(N_EXPERTS, dtype=jnp.int32)[None, :]
    onehot = (e_flat[:, None] == lanes).astype(jnp.int32)
    csum = jnp.cumsum(onehot, axis=0)
    counts = csum[-1]
    pcounts = (counts + tb - 1) // tb * tb
    pends = jnp.cumsum(pcounts)
    pstarts = pends - pcounts
    dest = jnp.sum(onehot * (pstarts[None, :] + csum - 1), axis=1).astype(jnp.int32)
    nb = (a + N_EXPERTS * tb) // tb
    n_act = (pends[-1] // tb).astype(jnp.int32)
    blk = jnp.arange(nb, dtype=jnp.int32)
    blk_idx = jnp.minimum(blk, n_act - 1)
    be = jnp.sum((pends[None, :] <= (blk_idx * tb)[:, None]).astype(jnp.int32), axis=1)
    be = jnp.minimum(be, N_EXPERTS - 1).astype(jnp.int32)
    first = ((blk * tb == pstarts[be]) & (blk < n_act)).astype(jnp.int32)
    parity = ((jnp.cumsum(first) - 1) & 1).astype(jnp.int32)

    def next_expert(e):
        after = pends[jnp.maximum(e, 0)] // tb
        return jnp.where((e >= 0) & (after < n_act), be[jnp.minimum(after, nb - 1)], -1).astype(jnp.int32)

    nxt1 = next_expert(be)
    nxt2 = next_expert(nxt1)
    plan = dict(dest=dest, be=be, blk_idx=blk_idx, first=first, parity=parity, nxt1=nxt1, nxt2=nxt2,
                n_act=n_act.reshape(1),
                pad0=(pstarts + counts).astype(jnp.int32), padn=(pcounts - counts).astype(jnp.int32))
    return plan


def _experts(h, experts, layer, w_gate, w_up, w_down):
    d, de = w_gate.shape[2:]
    chunks = d // LANES
    xchunks = chunks // 2
    t = h.shape[0] // xchunks
    tb = EXPERT_TB
    tm = min(PRE_TM, t)
    plan = _moe_plan(experts, t, tb)
    nb = plan["be"].shape[0]
    p = nb * tb
    xb = pl.pallas_call(
        functools.partial(_dispatch_kernel, chunks=xchunks),
        grid_spec=pltpu.PrefetchScalarGridSpec(
            num_scalar_prefetch=4, grid=(t // tm,),
            in_specs=[pl.BlockSpec((tm * xchunks, LANES), lambda i, *_: (i, 0))],
            out_specs=pl.BlockSpec(memory_space=pl.ANY),
            scratch_shapes=[pltpu.VMEM((tb * xchunks, LANES), h.dtype),
                            pltpu.SemaphoreType.DMA(()), pltpu.SemaphoreType.DMA(())]),
        out_shape=jax.ShapeDtypeStruct((p * xchunks, LANES), h.dtype),
        compiler_params=_cparams("arbitrary"), name="moe_dispatch",
    )(plan["dest"], plan["pad0"], plan["padn"], plan["n_act"], h)
    hbm = pl.BlockSpec(memory_space=pl.ANY)
    yb = pl.pallas_call(
        functools.partial(_expert_kernel, layer=layer),
        grid_spec=pltpu.PrefetchScalarGridSpec(
            num_scalar_prefetch=7, grid=(nb,),
            in_specs=[pl.BlockSpec((tb * xchunks, LANES), lambda b, be, bi, *_: (bi[b], 0)), hbm, hbm, hbm],
            out_specs=pl.BlockSpec((tb * xchunks, LANES), lambda b, *_: (b, 0)),
            scratch_shapes=[pltpu.VMEM((2, d, de), F32), pltpu.VMEM((2, d, de), F32), pltpu.VMEM((2, de, d), F32),
                            pltpu.VMEM((d, de), BF16), pltpu.VMEM((d, de), BF16), pltpu.VMEM((de, d), BF16),
                            pltpu.SemaphoreType.DMA((2, 3))]),
        out_shape=jax.ShapeDtypeStruct((p * xchunks, LANES), jnp.uint32),
        compiler_params=_cparams("arbitrary"), name="moe_experts",
    )(plan["be"], plan["blk_idx"], plan["first"], plan["parity"], plan["nxt1"], plan["nxt2"], plan["n_act"],
      xb, w_gate, w_up, w_down)
    return yb, plan["dest"]


def kernel(x, c, norm1_g, norm2_g, ada_w, ada_b, mix_in_w, na_rpb, na_out_w, fourier_out_w, branch_gate_w,
           branch_gate_b, mix_out_w, router_group_w, router_group_b, router_expert_w, router_expert_b,
           expert_w_gate, expert_w_up, expert_w_down, final_g):
    bsz, s, d = x.shape
    assert bsz == 1 and s % GRID_W == 0
    depth = ada_w.shape[0]
    na_w = NA_HEADS * NA_HEAD_DIM
    f_w = F_GROUPS * F_GROUP_DIM
    xs = x.reshape(s, d)
    mod_l = _ada_first(c, ada_w, ada_b, 1)[0]
    pad = ROUTER_LANES - N_GROUPS - N_EXPERTS
    wr_all = jnp.concatenate([router_group_w, router_expert_w, jnp.zeros((depth, d, pad), F32)], axis=2)
    wr_hi = wr_all.astype(BF16)
    wr_lo = (wr_all - wr_hi.astype(F32)).astype(BF16)
    wr_all = jnp.concatenate([wr_hi, wr_lo], axis=2)
    br_all = jnp.concatenate([router_group_b, router_expert_b, jnp.zeros((depth, pad), F32)], axis=1)
    moe = g2 = None
    for l in range(depth):
        sh1, sc1, g1, sh2, sc2, g2_l = [mod_l[:, i * d:(i + 1) * d] for i in range(N_MOD)]
        if moe is None:
            (h,) = _pre(xs, norm1_g[l], shift=sh1, scale=sc1)
        else:
            xs, h = _pre(xs, norm1_g[l], moe=moe, g2=g2, shift=sh1, scale=sc1)
        side = (c, ada_w, ada_b, l + 1) if l + 1 < depth else None
        qkv = _mm([(h, mix_in_w, l, 0)], [], lambda prod: prod(0), 3 * na_w, BF16, "mix_in_qkv", tn=MM_TN_WIDE,
                  side=side)
        if side is not None:
            qkv, mod_l = qkv
        u = _mm([(h, mix_in_w, l, 3 * na_w)], [], lambda prod: prod(0), f_w, jnp.uint32, "mix_in_fourier",
                tn=f_w)
        y_att = _attention(qkv, na_rpb[l])
        y_fft = _fourier(u)
        bias_bg = branch_gate_b[l].reshape(1, -1)
        mixed = _mm([(h, branch_gate_w, l, 0), (h, branch_gate_w, l, d),
                     (y_att, na_out_w, l, 0), (y_fft, fourier_out_w, l, 0)],
                    [(bias_bg, _rowvec, _vec_cols(0)), (bias_bg, _rowvec, _vec_cols(d))],
                    lambda prod, ba, bf: (_sigmoid(prod(0) + ba) * prod(2) + _sigmoid(prod(1) + bf) * prod(3)),
                    d, BF16, "gated_branch_mix", tn=MM_TN, tm=MM_TM_FUSED)
        xs = _mm([(mixed, mix_out_w, l, 0)],
                 [(xs, _tile, _at_cols(0)), (g1, _rowvec, _vec_cols(0))],
                 lambda prod, xr, g: xr + g * prod(0), d, F32, "mix_out", tn=MM_TN_WIDE)
        h2, wts, ids = _pre(xs, norm2_g[l], shift=sh2, scale=sc2, wr=wr_all[l], br=br_all[l].reshape(1, -1),
                            h_dtype=F32)
        yb, dest = _experts(h2, ids[:, :TOP_K], l, expert_w_gate, expert_w_up, expert_w_down)
        moe, g2 = (yb, dest, wts), g2_l
    _, out = _pre(xs, final_g, moe=moe, g2=g2, h_dtype=F32)
    return out.reshape(bsz, s, d)
```

```python
import functools
import math

import numpy as np
import jax
import jax.numpy as jnp
from jax import lax
from jax.experimental import pallas as pl
from jax.experimental.pallas import tpu as pltpu

F32 = jnp.float32
BF16 = jnp.bfloat16

GRID_W = 64
NA_HEADS = 16
NA_HEAD_DIM = 64
WIN_ROWS = 8
WIN_COLS = 16
F_GROUPS = 4
F_GROUP_DIM = 256
N_GROUPS = 4
EXPERTS_PER_GROUP = 8
N_EXPERTS = N_GROUPS * EXPERTS_PER_GROUP
TOP_K = 2
N_MOD = 6
EPS = 1e-6
NEG_INF = -1e30

LANES = 128
SUBLANES = 8

ADA_TN = 1536
ADA_ROWS = 256
PRE_TM = 512
COMBINE_TM = 256
MM_TM = 1024
MM_TM_FUSED = 512
MM_TN = 512
MM_TN_WIDE = 1024
ATTN_ROWS = 64
EXPERT_TB = 256
ROW_DMA_UNROLL = 8
FFT_JB = 8
FFT_KB = 16
ROUTER_LANES = LANES

VMEM_LIMIT = 56 * 1024 * 1024


def _cparams(*sem):
    return pltpu.CompilerParams(dimension_semantics=sem, vmem_limit_bytes=VMEM_LIMIT)


def _sigmoid(x):
    return 1.0 / (1.0 + jnp.exp(-x))


def _ada_kernel(c_ref, w_ref, b_ref, o_ref, cs_ref):
    c = c_ref[...]
    cs_ref[...] = c * _sigmoid(c)
    d, tn = w_ref.shape

    def body(i, acc):
        r0 = pl.multiple_of(i * ADA_ROWS, ADA_ROWS)
        prod = w_ref[pl.ds(r0, ADA_ROWS), :] * cs_ref[pl.ds(r0, ADA_ROWS), :]
        return acc + jnp.sum(prod.reshape(ADA_ROWS // SUBLANES, SUBLANES, tn), axis=0)

    acc = lax.fori_loop(0, d // ADA_ROWS, body, jnp.zeros((SUBLANES, tn), F32))
    o_ref[...] = jnp.sum(acc, axis=0, keepdims=True) + b_ref[...]


def _ada_first(c, ada_w, ada_b, n_layers):
    _, d, n = ada_w.shape
    nl = n_layers
    tn = min(ADA_TN, n)
    return pl.pallas_call(
        _ada_kernel,
        grid=(nl, n // tn),
        in_specs=[
            pl.BlockSpec((d, 1), lambda l, j: (0, 0)),
            pl.BlockSpec((None, d, tn), lambda l, j: (l, 0, j)),
            pl.BlockSpec((None, 1, tn), lambda l, j: (l, 0, j)),
        ],
        out_specs=pl.BlockSpec((None, 1, tn), lambda l, j: (l, 0, j)),
        out_shape=jax.ShapeDtypeStruct((nl, 1, n), F32),
        scratch_shapes=[pltpu.VMEM((d, 1), F32)],
        compiler_params=_cparams("parallel", "parallel"),
        name="ada_proj",
    )(c.reshape(d, 1), ada_w, ada_b.reshape(ada_b.shape[0], 1, n))


def _rms(x, g):
    return x * lax.rsqrt(jnp.mean(x * x, axis=-1, keepdims=True) + EPS) * g


def _router(h, wr_ref, br_ref):
    w_split = wr_ref[...]
    h_hi = h.astype(BF16)
    h_lo = (h - h_hi.astype(F32)).astype(BF16)
    dot = functools.partial(jnp.dot, preferred_element_type=F32)
    both = dot(h_hi, w_split)
    logits = both[:, :ROUTER_LANES] + (both[:, ROUTER_LANES:] + dot(h_lo, w_split[:, :ROUTER_LANES])) + br_ref[...]
    lane = lax.broadcasted_iota(jnp.int32, logits.shape, 1)
    big = jnp.int32(ROUTER_LANES)
    is_g = lane < N_GROUPS
    gl = jnp.where(is_g, logits, NEG_INF)
    gmax = jnp.max(gl, axis=-1, keepdims=True)
    gsum = jnp.sum(jnp.where(is_g, jnp.exp(gl - gmax), 0.0), axis=-1, keepdims=True)
    g_top_p = 1.0 / gsum
    g_top = jnp.min(jnp.where(is_g & (gl == gmax), lane, big), axis=-1, keepdims=True)
    e_lane = lane - N_GROUPS
    sel = (e_lane >= 0) & (e_lane < N_EXPERTS) & ((e_lane >> 3) == g_top)
    l1 = jnp.where(sel, logits, NEG_INF)
    m1 = jnp.max(l1, axis=-1, keepdims=True)
    i1 = jnp.min(jnp.where(sel & (l1 == m1), lane, big), axis=-1, keepdims=True)
    sel2 = sel & (lane != i1)
    l2 = jnp.where(sel2, logits, NEG_INF)
    m2 = jnp.max(l2, axis=-1, keepdims=True)
    i2 = jnp.min(jnp.where(sel2 & (l2 == m2), lane, big), axis=-1, keepdims=True)
    e21 = jnp.exp(m2 - m1)
    p1 = 1.0 / (1.0 + e21)
    p2 = e21 / (1.0 + e21)
    wts = jnp.where(lane == 0, g_top_p * p1, jnp.where(lane == 1, g_top_p * p2, 0.0))
    ids = jnp.where(lane == 0, i1 - N_GROUPS, jnp.where(lane == 1, i2 - N_GROUPS, 0))
    return wts, ids


def _load_token_major(ref, n_tok, chunks):
    return jnp.concatenate([ref[pl.ds(c, n_tok, stride=chunks), :] for c in range(chunks)], axis=1)


def _store_token_major(ref, val):
    n_tok, d = val.shape
    chunks = d // LANES
    for c in range(chunks):
        ref[pl.ds(c, n_tok, stride=chunks), :] = val[:, c * LANES:(c + 1) * LANES]


def _token_rows(tok, chunks):
    return pl.ds(pl.multiple_of(tok * chunks, chunks), chunks)


def _pre_kernel(*refs, combine, modulate, router, emit_h, h_dtype):
    it = iter(refs)
    if combine:
        dest_ref = next(it)
    x_ref = next(it)
    if combine:
        y_hbm, cw_ref, g2_ref = next(it), next(it), next(it)
    g_ref = next(it)
    if modulate:
        sh_ref, sc_ref = next(it), next(it)
    if router:
        wr_ref, br_ref = next(it), next(it)
    outs = [next(it) for _ in range(int(combine) + int(emit_h) + 2 * int(router))]
    x = x_ref[...]
    if combine:
        gbuf, gsem = next(it), next(it)
        i = pl.program_id(0)
        slot = i % 2
        tm, d = x_ref.shape
        chunks = d // LANES // 2

        def row_copy(step, sl, r, k):
            src = dest_ref[TOP_K * (step * tm + r) + k]
            return pltpu.make_async_copy(y_hbm.at[_token_rows(src, chunks)],
                                         gbuf.at[sl, k, _token_rows(r, chunks)], gsem.at[sl])

        def for_rows(fn):
            def body(r, carry):
                for k in range(TOP_K):
                    fn(r, k)
                return carry
            lax.fori_loop(0, tm, body, 0, unroll=ROW_DMA_UNROLL)

        @pl.when(i == 0)
        def _():
            for_rows(lambda r, k: row_copy(0, 0, r, k).start(priority=k))

        @pl.when(i + 1 < pl.num_programs(0))
        def _():
            for_rows(lambda r, k: row_copy(i + 1, 1 - slot, r, k).start(priority=k))

        for_rows(lambda r, k: row_copy(i, slot, r, k).wait())
        cw = cw_ref[...]
        y0 = _unpack_bf16_pairs(_load_token_major(gbuf.at[slot, 0], tm, chunks))
        y1 = _unpack_bf16_pairs(_load_token_major(gbuf.at[slot, 1], tm, chunks))
        x = x + g2_ref[...] * (cw[:, 0:1] * y0 + cw[:, 1:2] * y1)
        outs.pop(0)[...] = x
    h = _rms(x, g_ref[...])
    if modulate:
        h = h * (1.0 + sc_ref[...]) + sh_ref[...]
    if emit_h and router:
        _store_token_major(outs.pop(0), _pack_bf16_pairs(h))
    elif emit_h:
        outs.pop(0)[...] = h.astype(h_dtype)
    if router:
        wts, ids = _router(h, wr_ref, br_ref)
        outs.pop(0)[...] = wts
        outs.pop(0)[...] = ids


def _pre(x, g, *, moe=None, g2=None, shift=None, scale=None, wr=None, br=None, h_dtype=BF16):
    t, d = x.shape
    combine, modulate, router = moe is not None, shift is not None, wr is not None
    tm = min(COMBINE_TM if combine else PRE_TM, t)
    chunks = d // LANES
    row = pl.BlockSpec((tm, d), lambda i, *_: (i, 0))
    vec = pl.BlockSpec((1, d), lambda i, *_: (0, 0))
    lane_row = pl.BlockSpec((tm, ROUTER_LANES), lambda i, *_: (i, 0))
    prefetch, args, specs, scratch = [], [x], [row], []
    if combine:
        yb, dest, cw = moe
        prefetch.append(dest)
        args += [yb, cw, g2]
        specs += [pl.BlockSpec(memory_space=pl.ANY), lane_row, vec]
        scratch = [pltpu.VMEM((2, TOP_K, tm * chunks // 2, LANES), jnp.uint32), pltpu.SemaphoreType.DMA((2,))]
    args.append(g.reshape(1, d))
    specs.append(vec)
    if modulate:
        args += [shift, scale]
        specs += [vec, vec]
    if router:
        args += [wr, br]
        specs += [pl.BlockSpec(wr.shape, lambda i, *_: (0, 0)), pl.BlockSpec(br.shape, lambda i, *_: (0, 0))]
    out_shape, out_specs = [], []
    if combine:
        out_shape.append(jax.ShapeDtypeStruct((t, d), F32))
        out_specs.append(row)
    if router:
        out_shape.append(jax.ShapeDtypeStruct((t * chunks // 2, LANES), jnp.uint32))
        out_specs.append(pl.BlockSpec((tm * chunks // 2, LANES), lambda i, *_: (i, 0)))
    else:
        out_shape.append(jax.ShapeDtypeStruct((t, d), h_dtype))
        out_specs.append(row)
    if router:
        out_shape += [jax.ShapeDtypeStruct((t, ROUTER_LANES), F32), jax.ShapeDtypeStruct((t, ROUTER_LANES), jnp.int32)]
        out_specs += [lane_row, lane_row]
    kern = functools.partial(_pre_kernel, combine=combine, modulate=modulate, router=router,
                             emit_h=True, h_dtype=h_dtype)
    grid_spec = pltpu.PrefetchScalarGridSpec(
        num_scalar_prefetch=len(prefetch), grid=(t // tm,), in_specs=specs, out_specs=out_specs,
        scratch_shapes=scratch)
    return pl.pallas_call(
        kern, grid_spec=grid_spec, out_shape=out_shape,
        compiler_params=_cparams("arbitrary" if combine else "parallel"), name="pre_norm",
    )(*prefetch, *args)


def _mm_kernel(*refs, lhs_of, n_lhs, n_extras, epilogue, side):
    n_pairs = len(lhs_of)
    lhs = refs[:n_lhs]
    weights = refs[n_lhs:n_lhs + n_pairs]
    extras = refs[n_lhs + n_pairs:n_lhs + n_pairs + n_extras]
    pos = n_lhs + n_pairs + n_extras
    if side:
        c_ref, aw_ref, ab_ref = refs[pos:pos + 3]
        pos += 3
    o_ref = refs[pos]
    pos += 1
    if side:
        mod_ref = refs[pos]
        pos += 1
        c = c_ref[...]
        cs = c * _sigmoid(c)
        d, sn = aw_ref.shape
        acc = jnp.zeros((SUBLANES, sn), F32)
        for r0 in range(0, d, ADA_ROWS):
            prod = aw_ref[r0:r0 + ADA_ROWS, :] * cs[r0:r0 + ADA_ROWS]
            acc = acc + jnp.sum(prod.reshape(ADA_ROWS // SUBLANES, SUBLANES, sn), axis=0)
        mod_ref[...] = jnp.sum(acc, axis=0, keepdims=True) + ab_ref[...]
    caches = refs[pos:]

    @pl.when(pl.program_id(1) == 0)
    def _():
        for p in range(n_pairs):
            caches[p][...] = weights[p][...].astype(BF16)

    def product(p):
        a = lhs[lhs_of[p]][...]
        a = _unpack_bf16_pairs(a) if a.dtype == jnp.uint32 else a
        return jnp.dot(a.astype(BF16), caches[p][...], preferred_element_type=F32)

    out = epilogue(product, *[r[...] for r in extras])
    o_ref[...] = _pack_bf16_pairs(out) if o_ref.dtype == jnp.uint32 else out.astype(o_ref.dtype)


def _mm(pairs, extras, epilogue, n, out_dtype, name, tn=MM_TN, tm=MM_TM, side=None):
    m = pairs[0][0].shape[0]
    tm, tn = min(tm, m), min(tn, n)
    packed_out = out_dtype == jnp.uint32
    assert not packed_out or tn == n
    lhs, lhs_of, w_args, w_specs, scratch = [], [], [], [], []
    for a, w, layer, off in pairs:
        k = a.shape[1] * (2 if a.dtype == jnp.uint32 else 1)
        if not any(a is seen for seen in lhs):
            lhs.append(a)
        lhs_of.append([a is seen for seen in lhs].index(True))
        w_args.append(w)
        w_specs.append(pl.BlockSpec((None, k, tn), lambda j, i, layer=layer, off=off: (layer, 0, j + off // tn)))
        scratch.append(pltpu.VMEM((k, tn), BF16))
    args = lhs + w_args
    specs = [pl.BlockSpec((tm, a.shape[1]), lambda j, i: (i, 0)) for a in lhs] + w_specs
    for arr, bshape, imap in extras:
        args.append(arr)
        specs.append(pl.BlockSpec(bshape(tm, tn), imap(tn)))
    nj, ni = n // tn, m // tm
    out_specs = [pl.BlockSpec((tm, tn // 2 if packed_out else tn), lambda j, i: (i, j))]
    out_shape = [jax.ShapeDtypeStruct((m, n // 2 if packed_out else n), out_dtype)]
    if side is not None:
        c, ada_w, ada_b, layer = side
        d, width = ada_w.shape[1:]
        sn = width // (nj * ni)
        assert sn * nj * ni == width and sn % LANES == 0
        args += [c.reshape(d, 1), ada_w, ada_b.reshape(ada_b.shape[0], 1, width)]
        specs += [pl.BlockSpec((d, 1), lambda j, i: (0, 0)),
                  pl.BlockSpec((None, d, sn), lambda j, i: (layer, 0, j * ni + i)),
                  pl.BlockSpec((None, 1, sn), lambda j, i: (layer, 0, j * ni + i))]
        out_specs.append(pl.BlockSpec((1, sn), lambda j, i: (0, j * ni + i)))
        out_shape.append(jax.ShapeDtypeStruct((1, width), F32))
    outs = pl.pallas_call(
        functools.partial(_mm_kernel, lhs_of=tuple(lhs_of), n_lhs=len(lhs), n_extras=len(extras),
                          epilogue=epilogue, side=side is not None),
        grid=(nj, ni), in_specs=specs, out_specs=out_specs, out_shape=out_shape,
        scratch_shapes=scratch,
        compiler_params=_cparams("arbitrary", "arbitrary"), name=name,
    )(*args)
    return outs[0] if side is None else outs


def _tile(tm, tn):
    return (tm, tn)


def _rowvec(tm, tn):
    return (1, tn)


def _at_cols(off):
    return lambda tn: (lambda j, i: (i, j + off // tn))


def _vec_cols(off):
    return lambda tn: (lambda j, i: (0, j + off // tn))


def _attn_bias_fill(rpb_ref, b_ref, head0):
    n_dr, n_dc = 2 * WIN_ROWS - 1, 2 * WIN_COLS - 1
    qc = lax.broadcasted_iota(jnp.int32, (GRID_W, 2 * GRID_W), 0)
    kc = lax.broadcasted_iota(jnp.int32, (GRID_W, 2 * GRID_W), 1) & (GRID_W - 1)
    cs = jnp.clip(qc - WIN_COLS // 2, 0, GRID_W - WIN_COLS)
    inside = (kc >= cs) & (kc < cs + WIN_COLS)
    dc = jnp.clip(kc - qc, -(WIN_COLS - 1), WIN_COLS - 1) + (WIN_COLS - 1)
    lane = lax.broadcasted_iota(jnp.int32, (SUBLANES, 2 * GRID_W), 1)
    for h in range(2):
        base = (head0 + h) * (n_dr * n_dc)
        for dr in range(n_dr):
            row = jnp.zeros((SUBLANES, 2 * GRID_W), F32)
            for d in range(n_dc):
                row = jnp.where(lane == d, rpb_ref[base + dr * n_dc + d], row)
            table = jnp.concatenate([row] * (GRID_W // SUBLANES), axis=0)
            t = jnp.where(inside, jnp.take_along_axis(table, dc, axis=1), NEG_INF)
            for var in range(WIN_ROWS):
                j = dr - var
                if 0 <= j < WIN_ROWS:
                    c0 = j * GRID_W
                    lo = c0 % (2 * GRID_W)
                    b_ref[var, h * GRID_W:(h + 1) * GRID_W, c0:c0 + GRID_W] = t[:, lo:lo + GRID_W]


def _attn_kernel(rpb_ref, q_ref, k_ref, v_ref, o_ref, b_ref, s_ref, *, n_rows):
    rb = pl.program_id(1)
    n_q = q_ref.shape[0]
    wkeys = WIN_ROWS * GRID_W
    dh = NA_HEAD_DIM
    lane = lax.broadcasted_iota(jnp.int32, (GRID_W, 2 * dh), 1)
    first = lane < dh

    @pl.when(rb == 0)
    def _():
        _attn_bias_fill(rpb_ref, b_ref, 2 * pl.program_id(0))

    def window(i):
        r = rb * n_q + i
        rs = jnp.clip(r - WIN_ROWS // 2, 0, n_rows - WIN_ROWS)
        return rs, rs - r + (WIN_ROWS - 1)

    for i in range(n_q):
        rs, var = window(i)
        kw = k_ref[pl.ds(rs, WIN_ROWS)].reshape(wkeys, 2 * dh)
        q = q_ref[i] * jnp.asarray(dh ** -0.5, BF16)
        zero = jnp.zeros_like(q)
        q2 = jnp.concatenate([jnp.where(first, q, zero), jnp.where(first, zero, q)], axis=0)
        s = lax.dot_general(q2, kw, (((1,), (1,)), ((), ())), preferred_element_type=F32)
        b = b_ref[var]
        s_ref[i] = s + b
    for i in range(n_q):
        rs, _ = window(i)
        vw = v_ref[pl.ds(rs, WIN_ROWS)].reshape(wkeys, 2 * dh)
        s = s_ref[i]
        m = jnp.max(s, axis=-1, keepdims=True)
        p = jnp.exp(s - m)
        l = jnp.sum(p, axis=-1, keepdims=True)
        o = jnp.dot(p.astype(BF16), vw, preferred_element_type=F32) / l
        o_ref[i] = jnp.where(first, o[:GRID_W], o[GRID_W:]).astype(o_ref.dtype)


def _attention(proj, rpb):
    t, width = proj.shape
    n_rows = t // GRID_W
    pairs = NA_HEADS // 2
    p3 = proj.reshape(n_rows, GRID_W, width)
    blk = 2 * NA_HEAD_DIM
    rows = min(ATTN_ROWS, n_rows)
    wkeys = WIN_ROWS * GRID_W
    out = pl.pallas_call(
        functools.partial(_attn_kernel, n_rows=n_rows),
        grid_spec=pltpu.PrefetchScalarGridSpec(
            num_scalar_prefetch=1, grid=(pairs, n_rows // rows),
            in_specs=[
                pl.BlockSpec((rows, GRID_W, blk), lambda hp, rb, _: (rb, 0, hp)),
                pl.BlockSpec((n_rows, GRID_W, blk), lambda hp, rb, _: (0, 0, pairs + hp)),
                pl.BlockSpec((n_rows, GRID_W, blk), lambda hp, rb, _: (0, 0, 2 * pairs + hp)),
            ],
            out_specs=pl.BlockSpec((rows, GRID_W, blk), lambda hp, rb, _: (rb, 0, hp)),
            scratch_shapes=[pltpu.VMEM((WIN_ROWS, 2 * GRID_W, wkeys), F32),
                            pltpu.VMEM((rows, 2 * GRID_W, wkeys), F32)]),
        out_shape=jax.ShapeDtypeStruct((n_rows, GRID_W, NA_HEADS * NA_HEAD_DIM), BF16),
        compiler_params=_cparams("arbitrary", "arbitrary"), name="na_attention",
    )(rpb.reshape(-1), p3, p3, p3)
    return out.reshape(t, NA_HEADS * NA_HEAD_DIM)


def _dft_tables(n1, n2, dc):
    s = n1 * n2
    a = np.arange(dc, dtype=np.float64)
    ang_c = 2 * np.pi * np.outer(a, a) / dc
    w_c = np.concatenate([np.cos(ang_c), -np.sin(ang_c)], axis=1)
    a1 = np.arange(n1, dtype=np.float64)
    ang1 = 2 * np.pi * np.outer(a1, a1) / n1
    c1, s1 = np.cos(ang1), np.sin(ang1)
    m1 = np.block([[c1, s1], [-s1, c1]])
    a2 = np.arange(n2, dtype=np.float64)
    ang_t = 2 * np.pi * np.outer(a2, a1) / s
    tw_c = np.repeat(np.cos(ang_t)[:, :, None], LANES, axis=2)
    tw_s = np.repeat(np.sin(ang_t)[:, :, None], LANES, axis=2)
    ang2 = 2 * np.pi * np.outer(a2, a2) / n2
    m2 = np.concatenate([np.cos(ang2), np.sin(ang2)], axis=1)
    return (jnp.asarray(w_c, BF16), jnp.asarray(m1, BF16), jnp.asarray(tw_c, F32), jnp.asarray(tw_s, F32),
            jnp.asarray(m2, BF16))


def _fft1_kernel(u_hbm, wc_ref, m1_ref, tc_ref, ts_ref, br_hbm, bi_hbm,
                 ubuf, zz_ref, obuf, isem, osem, *, n1, dc):
    s = pl.program_id(0)
    ns = pl.num_programs(0)
    jb = ubuf.shape[1]
    slot = s % 2

    def in_copy(step, sl, jj):
        return pltpu.make_async_copy(u_hbm.at[:, step * jb + jj], ubuf.at[sl, jj], isem.at[sl])

    def out_copies(step, sl, jj):
        j = step * jb + jj
        return (pltpu.make_async_copy(obuf.at[sl, 0, jj], br_hbm.at[:, j], osem.at[sl]),
                pltpu.make_async_copy(obuf.at[sl, 1, jj], bi_hbm.at[:, j], osem.at[sl]))

    @pl.when(s == 0)
    def _():
        for jj in range(jb):
            in_copy(0, 0, jj).start()

    @pl.when(s + 1 < ns)
    def _():
        for jj in range(jb):
            in_copy(s + 1, 1 - slot, jj).start()

    for jj in range(jb):
        in_copy(s, slot, jj).wait()

    @pl.when(s >= 2)
    def _():
        for jj in range(jb):
            for cp in out_copies(s - 2, slot, jj):
                cp.wait()

    c = 2 * ubuf.shape[3]
    u = _unpack_bf16_pairs(ubuf[slot].reshape(jb * n1, c // 2)).astype(BF16)
    for g in range(c // dc):
        z = jnp.dot(u[:, g * dc:(g + 1) * dc], wc_ref[...], preferred_element_type=F32)
        z = z.astype(BF16)
        for jj in range(jb):
            zz_ref[jj, :n1, g * dc:(g + 1) * dc] = z[jj * n1:(jj + 1) * n1, :dc]
            zz_ref[jj, n1:, g * dc:(g + 1) * dc] = z[jj * n1:(jj + 1) * n1, dc:]
    for jj in range(jb):
        a = jnp.dot(m1_ref[...], zz_ref[jj], preferred_element_type=F32)
        ar, ai = a[:n1], a[n1:]
        tc = jnp.concatenate([tc_ref[jj]] * (c // LANES), axis=1)
        ts = jnp.concatenate([ts_ref[jj]] * (c // LANES), axis=1)
        obuf[slot, 0, jj] = _pack_bf16_pairs(ar * tc + ai * ts)
        obuf[slot, 1, jj] = _pack_bf16_pairs(ai * tc - ar * ts)
    for jj in range(jb):
        for cp in out_copies(s, slot, jj):
            cp.start()

    @pl.when(s == ns - 1)
    def _():
        for jj in range(jb):
            for cp in out_copies(s, slot, jj):
                cp.wait()

        @pl.when(s >= 1)
        def _():
            for jj in range(jb):
                for cp in out_copies(s - 1, 1 - slot, jj):
                    cp.wait()


def _fft2_kernel(br_ref, bi_ref, m2_ref, y_hbm, obuf, osem, *, n2, scale):
    s = pl.program_id(0)
    ns = pl.num_programs(0)
    kb = obuf.shape[1]
    slot = s % 2

    def out_copy(step, sl, kk):
        return pltpu.make_async_copy(obuf.at[sl, kk], y_hbm.at[:, step * kb + kk], osem.at[sl])

    @pl.when(s >= 2)
    def _():
        for kk in range(kb):
            out_copy(s - 2, slot, kk).wait()

    for kk in range(kb):
        rows = slice(kk * n2, (kk + 1) * n2)
        bb = _unpack_bf16_pairs(jnp.concatenate([br_ref[rows, :], bi_ref[rows, :]], axis=0)).astype(BF16)
        obuf[slot, kk] = _pack_bf16_pairs(jnp.dot(m2_ref[...], bb, preferred_element_type=F32) * scale)
    for kk in range(kb):
        out_copy(s, slot, kk).start()

    @pl.when(s == ns - 1)
    def _():
        for kk in range(kb):
            out_copy(s, slot, kk).wait()

        @pl.when(s >= 1)
        def _():
            for kk in range(kb):
                out_copy(s - 1, 1 - slot, kk).wait()


def _fourier(u):
    t, hw = u.shape
    fw = 2 * hw
    n2 = GRID_W
    n1 = t // n2
    dc = F_GROUP_DIM
    w_c, m1, tw_c, tw_s, m2 = _dft_tables(n1, n2, dc)
    jb = min(FFT_JB, n2)
    kb = min(FFT_KB, n1)
    any_spec = pl.BlockSpec(memory_space=pl.ANY)
    tw_spec = pl.BlockSpec((jb, n1, LANES), lambda s: (s, 0, 0))
    b_re, b_im = pl.pallas_call(
        functools.partial(_fft1_kernel, n1=n1, dc=dc),
        grid=(n2 // jb,),
        in_specs=[
            any_spec,
            pl.BlockSpec((dc, 2 * dc), lambda s: (0, 0)),
            pl.BlockSpec((2 * n1, 2 * n1), lambda s: (0, 0)),
            tw_spec, tw_spec,
        ],
        out_specs=[any_spec, any_spec],
        out_shape=[jax.ShapeDtypeStruct((n1, n2, hw), jnp.uint32)] * 2,
        scratch_shapes=[
            pltpu.VMEM((2, jb, n1, hw), jnp.uint32),
            pltpu.VMEM((jb, 2 * n1, fw), BF16),
            pltpu.VMEM((2, 2, jb, n1, hw), jnp.uint32),
            pltpu.SemaphoreType.DMA((2,)),
            pltpu.SemaphoreType.DMA((2,)),
        ],
        compiler_params=_cparams("arbitrary"), name="fourier_stage1",
    )(u.reshape(n1, n2, hw), w_c, m1, tw_c, tw_s)
    scale = 1.0 / math.sqrt(t * dc)
    in_blk = pl.BlockSpec((kb * n2, hw), lambda s: (s, 0))
    y = pl.pallas_call(
        functools.partial(_fft2_kernel, n2=n2, scale=scale),
        grid=(n1 // kb,),
        in_specs=[in_blk, in_blk, pl.BlockSpec((n2, 2 * n2), lambda s: (0, 0))],
        out_specs=any_spec,
        out_shape=jax.ShapeDtypeStruct((n2, n1, hw), jnp.uint32),
        scratch_shapes=[pltpu.VMEM((2, kb, n2, hw), jnp.uint32), pltpu.SemaphoreType.DMA((2,))],
        compiler_params=_cparams("arbitrary"), name="fourier_stage2",
    )(b_re.reshape(t, hw), b_im.reshape(t, hw), m2)
    return y.reshape(t, hw)


def _dispatch_kernel(dest_ref, pad0_ref, padn_ref, na_ref, h_ref, xb_hbm, zbuf, sem, zsem, *, chunks):
    i = pl.program_id(0)
    tm = h_ref.shape[0] // chunks
    tb = zbuf.shape[0] // chunks
    nb = xb_hbm.shape[0] // (tb * chunks)

    def tail_copies(fn):
        def body(b, carry):
            fn(pltpu.make_async_copy(zbuf, xb_hbm.at[_token_rows(b, tb * chunks)], zsem))
            return carry
        lax.fori_loop(na_ref[0], nb, body, 0)

    def pad_copies(e, fn):
        pos = pad0_ref[e]
        n = padn_ref[e]
        bit = tb // 2
        while bit >= 1:
            @pl.when((n & bit) != 0)
            def _(pos=pos, bit=bit):
                rows = pl.ds(pl.multiple_of(pos * chunks, chunks), bit * chunks)
                fn(pltpu.make_async_copy(zbuf.at[pl.ds(0, bit * chunks)], xb_hbm.at[rows], zsem))
            pos = pos + (n & bit)
            bit //= 2

    @pl.when(i == 0)
    def _():
        zbuf[...] = jnp.zeros_like(zbuf)
        lax.fori_loop(0, N_EXPERTS, lambda e, c: (pad_copies(e, lambda cp: cp.start()), c)[1], 0)
        tail_copies(lambda cp: cp.start())

    def row_copy(r, k):
        dst = dest_ref[TOP_K * (i * tm + r) + k]
        return pltpu.make_async_copy(h_ref.at[_token_rows(r, chunks)], xb_hbm.at[_token_rows(dst, chunks)], sem)

    def for_rows(fn):
        def body(r, carry):
            for k in range(TOP_K):
                fn(r, k)
            return carry
        lax.fori_loop(0, tm, body, 0, unroll=ROW_DMA_UNROLL)

    for_rows(lambda r, k: row_copy(r, k).start(priority=k))
    for_rows(lambda r, k: row_copy(r, k).wait())

    @pl.when(i == 0)
    def _():
        lax.fori_loop(0, N_EXPERTS, lambda e, c: (pad_copies(e, lambda cp: cp.wait()), c)[1], 0)
        tail_copies(lambda cp: cp.wait())


def _pack_bf16_pairs(h):
    half = h.shape[1] // 2
    bits = lax.bitcast_convert_type(h.astype(BF16).astype(F32), jnp.uint32)
    return bits[:, half:] | (bits[:, :half] >> 16)


def _unpack_bf16_pairs(p):
    lo = lax.bitcast_convert_type(p << 16, F32)
    hi = lax.bitcast_convert_type(p & jnp.uint32(0xFFFF0000), F32)
    return jnp.concatenate([lo, hi], axis=1)


def _expert_kernel(be_ref, bi_ref, first_ref, par_ref, nxt1_ref, nxt2_ref, na_ref,
                   x_hbm, wg_hbm, wu_hbm, wd_hbm, y_ref,
                   wg_st, wu_st, wd_st, wg_c, wu_c, wd_c, wsem, xbuf, xsem, *, layer):
    b = pl.program_id(0)
    n_act = na_ref[0]
    active = b < n_act
    d = wg_c.shape[0]
    xchunks = d // LANES // 2
    x_rows = xbuf.shape[1]
    tb = x_rows // xchunks
    x_ref = xbuf.at[b % 3]

    def x_copy(blk):
        rows = pl.ds(pl.multiple_of(blk * x_rows, x_rows), x_rows)
        return pltpu.make_async_copy(x_hbm.at[rows], xbuf.at[blk % 3], xsem.at[blk % 3])

    @pl.when(b == 0)
    def _():
        x_copy(0).start()

        @pl.when(n_act > 1)
        def _():
            x_copy(1).start()

    @pl.when(active)
    def _():
        @pl.when(b + 2 < n_act)
        def _():
            x_copy(b + 2).start()

        x_copy(b).wait()

    def weight_copies(e, slot):
        return [pltpu.make_async_copy(src.at[layer, e], dst.at[slot], wsem.at[slot, n])
                for n, (src, dst) in enumerate(((wg_hbm, wg_st), (wu_hbm, wu_st), (wd_hbm, wd_st)))]

    def start_weights(e, slot):
        for cp, priority in zip(weight_copies(e, slot), (1, 1, 0)):
            cp.start(priority=priority)

    @pl.when(b == 0)
    def _():
        start_weights(be_ref[0], 0)

        @pl.when(nxt1_ref[0] >= 0)
        def _():
            start_weights(nxt1_ref[0], 1)

    is_first = first_ref[b] == 1

    def block(refresh):
        slot = par_ref[b]
        x = _unpack_bf16_pairs(_load_token_major(x_ref, tb, xchunks)).astype(BF16)
        if refresh:
            wg_c[...] = wg_st[slot].astype(BF16)
        gate = jnp.dot(x, wg_c[...], preferred_element_type=F32)
        if refresh:
            wu_c[...] = wu_st[slot].astype(BF16)
        up = jnp.dot(x, wu_c[...], preferred_element_type=F32)
        act = (gate * _sigmoid(gate) * up).astype(BF16)
        if refresh:
            wd_c[...] = wd_st[slot].astype(BF16)
        _store_token_major(y_ref, _pack_bf16_pairs(jnp.dot(act, wd_c[...], preferred_element_type=F32)))

    @pl.when(active & is_first)
    def _():
        slot = par_ref[b]
        for cp in weight_copies(be_ref[b], slot):
            cp.wait()
        block(True)

        @pl.when(nxt2_ref[b] >= 0)
        def _():
            start_weights(nxt2_ref[b], slot)

    @pl.when(active & jnp.logical_not(is_first))
    def _():
        block(False)

    @pl.when(jnp.logical_not(active))
    def _():
        y_ref[...] = jnp.zeros_like(y_ref)


def _moe_plan(experts, t, tb):
    a = t * TOP_K
    e_flat = experts.reshape(a)
    lanes = jnp.arange(N_EXPERTS, dtype=jnp.int32)[None, :]
    onehot = (e_flat[:, None] == lanes).astype(jnp.int32)
    csum = jnp.cumsum(onehot, axis=0)
    counts = csum[-1]
    pcounts = (counts + tb - 1) // tb * tb
    pends = jnp.cumsum(pcounts)
    pstarts = pends - pcounts
    dest = jnp.sum(onehot * (pstarts[None, :] + csum - 1), axis=1).astype(jnp.int32)
    nb = (a + N_EXPERTS * tb) // tb
    n_act = (pends[-1] // tb).astype(jnp.int32)
    blk = jnp.arange(nb, dtype=jnp.int32)
    blk_idx = jnp.minimum(blk, n_act - 1)
    be = jnp.sum((pends[None, :] <= (blk_idx * tb)[:, None]).astype(jnp.int32), axis=1)
    be = jnp.minimum(be, N_EXPERTS - 1).astype(jnp.int32)
    first = ((blk * tb == pstarts[be]) & (blk < n_act)).astype(jnp.int32)
    parity = ((jnp.cumsum(first) - 1) & 1).astype(jnp.int32)

    def next_expert(e):
        after = pends[jnp.maximum(e, 0)] // tb
        return jnp.where((e >= 0) & (after < n_act), be[jnp.minimum(after, nb - 1)], -1).astype(jnp.int32)

    nxt1 = next_expert(be)
    nxt2 = next_expert(nxt1)
    plan = dict(dest=dest, be=be, blk_idx=blk_idx, first=first, parity=parity, nxt1=nxt1, nxt2=nxt2,
                n_act=n_act.reshape(1),
                pad0=(pstarts + counts).astype(jnp.int32), padn=(pcounts - counts).astype(jnp.int32))
    return plan


def _experts(h, experts, layer, w_gate, w_up, w_down):
    d, de = w_gate.shape[2:]
    chunks = d // LANES
    xchunks = chunks // 2
    t = h.shape[0] // xchunks
    tb = EXPERT_TB
    tm = min(PRE_TM, t)
    plan = _moe_plan(experts, t, tb)
    nb = plan["be"].shape[0]
    p = nb * tb
    xb = pl.pallas_call(
        functools.partial(_dispatch_kernel, chunks=xchunks),
        grid_spec=pltpu.PrefetchScalarGridSpec(
            num_scalar_prefetch=4, grid=(t // tm,),
            in_specs=[pl.BlockSpec((tm * xchunks, LANES), lambda i, *_: (i, 0))],
            out_specs=pl.BlockSpec(memory_space=pl.ANY),
            scratch_shapes=[pltpu.VMEM((tb * xchunks, LANES), h.dtype),
                            pltpu.SemaphoreType.DMA(()), pltpu.SemaphoreType.DMA(())]),
        out_shape=jax.ShapeDtypeStruct((p * xchunks, LANES), h.dtype),
        compiler_params=_cparams("arbitrary"), name="moe_dispatch",
    )(plan["dest"], plan["pad0"], plan["padn"], plan["n_act"], h)
    hbm = pl.BlockSpec(memory_space=pl.ANY)
    yb = pl.pallas_call(
        functools.partial(_expert_kernel, layer=layer),
        grid_spec=pltpu.PrefetchScalarGridSpec(
            num_scalar_prefetch=7, grid=(nb,),
            in_specs=[hbm, hbm, hbm, hbm],
            out_specs=pl.BlockSpec((tb * xchunks, LANES), lambda b, *_: (b, 0)),
            scratch_shapes=[pltpu.VMEM((2, d, de), F32), pltpu.VMEM((2, d, de), F32), pltpu.VMEM((2, de, d), F32),
                            pltpu.VMEM((d, de), BF16), pltpu.VMEM((d, de), BF16), pltpu.VMEM((de, d), BF16),
                            pltpu.SemaphoreType.DMA((2, 3)),
                            pltpu.VMEM((3, tb * xchunks, LANES), jnp.uint32), pltpu.SemaphoreType.DMA((3,))]),
        out_shape=jax.ShapeDtypeStruct((p * xchunks, LANES), jnp.uint32),
        compiler_params=_cparams("arbitrary"), name="moe_experts",
    )(plan["be"], plan["blk_idx"], plan["first"], plan["parity"], plan["nxt1"], plan["nxt2"], plan["n_act"],
      xb, w_gate, w_up, w_down)
    return yb, plan["dest"]


def kernel(x, c, norm1_g, norm2_g, ada_w, ada_b, mix_in_w, na_rpb, na_out_w, fourier_out_w, branch_gate_w,
           branch_gate_b, mix_out_w, router_group_w, router_group_b, router_expert_w, router_expert_b,
           expert_w_gate, expert_w_up, expert_w_down, final_g):
    bsz, s, d = x.shape
    assert bsz == 1 and s % GRID_W == 0
    depth = ada_w.shape[0]
    na_w = NA_HEADS * NA_HEAD_DIM
    f_w = F_GROUPS * F_GROUP_DIM
    xs = x.reshape(s, d)
    mod_l = _ada_first(c, ada_w, ada_b, 1)[0]
    pad = ROUTER_LANES - N_GROUPS - N_EXPERTS
    wr_all = jnp.concatenate([router_group_w, router_expert_w, jnp.zeros((depth, d, pad), F32)], axis=2)
    wr_hi = wr_all.astype(BF16)
    wr_lo = (wr_all - wr_hi.astype(F32)).astype(BF16)
    wr_all = jnp.concatenate([wr_hi, wr_lo], axis=2)
    br_all = jnp.concatenate([router_group_b, router_expert_b, jnp.zeros((depth, pad), F32)], axis=1)
    moe = g2 = None
    for l in range(depth):
        sh1, sc1, g1, sh2, sc2, g2_l = [mod_l[:, i * d:(i + 1) * d] for i in range(N_MOD)]
        if moe is None:
            (h,) = _pre(xs, norm1_g[l], shift=sh1, scale=sc1)
        else:
            xs, h = _pre(xs, norm1_g[l], moe=moe, g2=g2, shift=sh1, scale=sc1)
        side = (c, ada_w, ada_b, l + 1) if l + 1 < depth else None
        qkv = _mm([(h, mix_in_w, l, 0)], [], lambda prod: prod(0), 3 * na_w, BF16, "mix_in_qkv", tn=MM_TN_WIDE,
                  side=side)
        if side is not None:
            qkv, mod_l = qkv
        u = _mm([(h, mix_in_w, l, 3 * na_w)], [], lambda prod: prod(0), f_w, jnp.uint32, "mix_in_fourier",
                tn=f_w)
        y_att = _attention(qkv, na_rpb[l])
        y_fft = _fourier(u)
        bias_bg = branch_gate_b[l].reshape(1, -1)
        mixed = _mm([(h, branch_gate_w, l, 0), (h, branch_gate_w, l, d),
                     (y_att, na_out_w, l, 0), (y_fft, fourier_out_w, l, 0)],
                    [(bias_bg, _rowvec, _vec_cols(0)), (bias_bg, _rowvec, _vec_cols(d))],
                    lambda prod, ba, bf: (_sigmoid(prod(0) + ba) * prod(2) + _sigmoid(prod(1) + bf) * prod(3)),
                    d, BF16, "gated_branch_mix", tn=MM_TN, tm=MM_TM_FUSED)
        xs = _mm([(mixed, mix_out_w, l, 0)],
                 [(xs, _tile, _at_cols(0)), (g1, _rowvec, _vec_cols(0))],
                 lambda prod, xr, g: xr + g * prod(0), d, F32, "mix_out", tn=MM_TN_WIDE)
        h2, wts, ids = _pre(xs, norm2_g[l], shift=sh2, scale=sc2, wr=wr_all[l], br=br_all[l].reshape(1, -1),
                            h_dtype=F32)
        yb, dest = _experts(h2, ids[:, :TOP_K], l, expert_w_gate, expert_w_up, expert_w_down)
        moe, g2 = (yb, dest, wts), g2_l
    _, out = _pre(xs, final_g, moe=moe, g2=g2, h_dtype=F32)
    return out.reshape(bsz, s, d)
```
